```python
import jax, jax.numpy as jnp
from jax import lax
import numpy as np

D_MODEL = 1024
BATCH = 8
SEQ = 4096
DEPTH = 1

MLA_HEADS = 8
NOPE_DIM = 64
ROPE_DIM = 32
QK_DIM = NOPE_DIM + ROPE_DIM
V_DIM = 64
Q_LORA = 512
KV_LORA = 256
ROPE_THETA = 10000.0
Q_BLOCK = 128
MLA_WIDTH = MLA_HEADS * V_DIM
HG_HEADS = 4
HG_KDIM = 128
HG_VDIM = 128
HG_WIDTH = HG_HEADS * HG_KDIM
HG_VWIDTH = HG_HEADS * HG_VDIM
CHUNK = 64
D_FF = 2816
CONV_W = 3
N_MOD = 6
EPS = 1e-6

IN_SPLITS = (Q_LORA, KV_LORA + ROPE_DIM, HG_WIDTH, HG_WIDTH, HG_VWIDTH, HG_VWIDTH, D_MODEL, D_MODEL)
IN_WIDTH = int(sum(IN_SPLITS))

kernel_name = "hybrid_mla_hgrn2_adaln_block"


def rms_norm(x, g):
    xf = x.astype(jnp.float32)
    y = xf * lax.rsqrt(jnp.mean(xf * xf, axis=-1, keepdims=True) + EPS)
    return (y * g.astype(jnp.float32)).astype(x.dtype)


def rope_tables(positions):
    inv_freq = ROPE_THETA ** (-jnp.arange(0, ROPE_DIM, 2, dtype=jnp.float32) / ROPE_DIM)
    ang = positions.astype(jnp.float32)[..., None] * inv_freq
    return jnp.cos(ang)[:, :, None, :], jnp.sin(ang)[:, :, None, :]


def rope_tail(t, cos, sin):
    t_pass, t_rot = t[..., :NOPE_DIM], t[..., NOPE_DIM:]
    tf = t_rot.astype(jnp.float32)
    t1, t2 = tf[..., : ROPE_DIM // 2], tf[..., ROPE_DIM // 2:]
    rot = jnp.concatenate([t1 * cos - t2 * sin, t2 * cos + t1 * sin], axis=-1)
    return jnp.concatenate([t_pass, rot.astype(t.dtype)], axis=-1)


def causal_attention_blocked(q, k, v):
    B, S, H, _ = q.shape
    nb = S // Q_BLOCK
    scale = 1.0 / float(np.sqrt(QK_DIM))
    qb = q.reshape(B, nb, Q_BLOCK, H, QK_DIM).transpose(1, 0, 2, 3, 4)
    k_pos = jnp.arange(S)

    def block(args):
        qi, i = args
        s = jnp.einsum('bqhd,bkhd->bhqk', qi, k).astype(jnp.float32) * scale
        q_pos = i * Q_BLOCK + jnp.arange(Q_BLOCK)
        mask = k_pos[None, :] <= q_pos[:, None]
        s = jnp.where(mask[None, None], s, -jnp.inf)
        p = jax.nn.softmax(s, axis=-1).astype(v.dtype)
        return jnp.einsum('bhqk,bkhd->bqhd', p, v)

    out = lax.map(block, (qb, jnp.arange(nb)))
    return out.transpose(1, 0, 2, 3, 4).reshape(B, S, H * V_DIM)


def mla_branch(c_q, c_kv_all, cos, sin, q_a_norm_g, w_uq, kv_a_norm_g, w_ukv, q_norm_g, k_norm_g):
    B, S, _ = c_q.shape
    c_q = rms_norm(c_q, q_a_norm_g)
    q = (c_q @ w_uq).reshape(B, S, MLA_HEADS, QK_DIM)
    c_kv, k_rope = c_kv_all[..., :KV_LORA], c_kv_all[..., KV_LORA:]
    c_kv = rms_norm(c_kv, kv_a_norm_g)
    kv = (c_kv @ w_ukv).reshape(B, S, MLA_HEADS, NOPE_DIM + V_DIM)
    k_nope, v = kv[..., :NOPE_DIM], kv[..., NOPE_DIM:]
    k_rope = jnp.broadcast_to(k_rope[:, :, None, :], (B, S, MLA_HEADS, ROPE_DIM))
    k = jnp.concatenate([k_nope, k_rope], axis=-1)
    q = rope_tail(rms_norm(q, q_norm_g), cos, sin)
    k = rope_tail(rms_norm(k, k_norm_g), cos, sin)
    return causal_attention_blocked(q, k, v)


def hgrn2_lower_bound(lb_table, layer):
    cum = jnp.cumsum(jax.nn.softmax(lb_table.astype(jnp.float32), axis=0), axis=0)
    return cum[layer + 1] - cum[0]


def hgrn2_branch(q_pre, f_pre, i_in, g_pre, lb, out_norm_g):
    B, S, _ = q_pre.shape
    nc = S // CHUNK
    f32 = jnp.float32
    q = jax.nn.silu(q_pre.astype(f32)).reshape(B, S, HG_HEADS, HG_KDIM)
    f = lb + (1.0 - lb) * jax.nn.sigmoid(f_pre.astype(f32))
    log_f = jnp.log(f).reshape(B, S, HG_HEADS, HG_KDIM)
    k = (1.0 - f).reshape(B, S, HG_HEADS, HG_KDIM)
    v = i_in.astype(f32).reshape(B, S, HG_HEADS, HG_VDIM)

    def to_chunks(t):
        return t.reshape(B, nc, CHUNK, HG_HEADS, t.shape[-1]).transpose(1, 0, 3, 2, 4)

    tri = jnp.tril(jnp.ones((CHUNK, CHUNK), dtype=bool))[None, None, :, :, None]

    def step(state, inp):
        qc, kc, vc, lfc = inp
        b = jnp.cumsum(lfc, axis=2)
        inter = jnp.einsum('bhck,bhkv->bhcv', qc * jnp.exp(b), state)
        diff = b[:, :, :, None, :] - b[:, :, None, :, :]
        decay = jnp.exp(jnp.where(tri, diff, -jnp.inf))
        attn = jnp.einsum('bhtk,bhsk,bhtsk->bhts', qc, kc, decay)
        intra = jnp.einsum('bhts,bhsv->bhtv', attn, vc)
        b_last = b[:, :, -1:, :]
        new_state = (jnp.exp(b_last[:, :, 0, :])[..., None] * state
                     + jnp.einsum('bhsk,bhsv->bhkv', kc * jnp.exp(b_last - b), vc))
        return new_state, inter + intra

    state0 = jnp.zeros((B, HG_HEADS, HG_KDIM, HG_VDIM), f32)
    _, o = lax.scan(step, state0, (to_chunks(q), to_chunks(k), to_chunks(v), to_chunks(log_f)))
    o = o.transpose(1, 0, 3, 2, 4).reshape(B, S, HG_HEADS, HG_VDIM)
    o = rms_norm(o, out_norm_g) * jax.nn.silu(g_pre.astype(f32)).reshape(B, S, HG_HEADS, HG_VDIM)
    return o.reshape(B, S, HG_VWIDTH).astype(q_pre.dtype)


def causal_depthwise_conv(u, w, b):
    S = u.shape[1]
    up = jnp.pad(u, ((0, 0), (CONV_W - 1, 0), (0, 0)))
    y = b
    for j in range(CONV_W):
        y = y + w[j] * up[:, j:j + S, :]
    return y


def setup_inputs(seed: int = 0) -> dict:
    key = jax.random.key(seed)
    ks = jax.random.split(key, 32)
    f32 = jnp.float32
    L = DEPTH

    def nrm(k, shape, scale):
        return jax.random.normal(k, shape, f32) * scale

    def gain(k, shape):
        return 1.0 + 0.02 * jax.random.normal(k, shape, f32)

    x = jax.random.normal(ks[0], (BATCH, SEQ, D_MODEL), f32)
    c = jax.random.normal(ks[1], (BATCH, D_MODEL), f32)
    offsets = jax.random.randint(ks[2], (BATCH, 1), 0, 1024, dtype=jnp.int32)
    positions = offsets + jnp.arange(SEQ, dtype=jnp.int32)[None, :]
    return {
        "x": x,
        "c": c,
        "positions": positions,
        "w_ada": nrm(ks[3], (L, D_MODEL, N_MOD * D_MODEL), 0.5 * D_MODEL ** -0.5),
        "b_ada": nrm(ks[4], (L, N_MOD * D_MODEL), 0.02),
        "norm1_g": gain(ks[5], (L, D_MODEL)),
        "w_in": nrm(ks[6], (L, D_MODEL, IN_WIDTH), D_MODEL ** -0.5),
        "q_a_norm_g": gain(ks[7], (L, Q_LORA)),
        "w_uq": nrm(ks[8], (L, Q_LORA, MLA_HEADS * QK_DIM), Q_LORA ** -0.5),
        "kv_a_norm_g": gain(ks[9], (L, KV_LORA)),
        "w_ukv": nrm(ks[10], (L, KV_LORA, MLA_HEADS * (NOPE_DIM + V_DIM)), KV_LORA ** -0.5),
        "q_norm_g": gain(ks[11], (L, QK_DIM)),
        "k_norm_g": gain(ks[12], (L, QK_DIM)),
        "hg_lower_bound": nrm(ks[13], (L + 1, HG_WIDTH), 0.5),
        "hg_out_norm_g": gain(ks[14], (L, HG_VDIM)),
        "w_branch_a": nrm(ks[15], (L, MLA_WIDTH, D_MODEL), MLA_WIDTH ** -0.5),
        "w_branch_b": nrm(ks[16], (L, HG_VWIDTH, D_MODEL), HG_VWIDTH ** -0.5),
        "w_out": nrm(ks[17], (L, D_MODEL, D_MODEL), D_MODEL ** -0.5),
        "norm2_g": gain(ks[18], (L, D_MODEL)),
        "w_up": nrm(ks[19], (L, D_MODEL, 2 * D_FF), D_MODEL ** -0.5),
        "conv_w": nrm(ks[20], (L, CONV_W, 2 * D_FF), CONV_W ** -0.5),
        "conv_b": nrm(ks[21], (L, 2 * D_FF), 0.02),
        "w_down": nrm(ks[22], (L, D_FF, D_MODEL), D_FF ** -0.5),
    }


def reference(x, c, positions, w_ada, b_ada, norm1_g, w_in, q_a_norm_g, w_uq, kv_a_norm_g,
              w_ukv, q_norm_g, k_norm_g, hg_lower_bound, hg_out_norm_g, w_branch_a,
              w_branch_b, w_out, norm2_g, w_up, conv_w, conv_b, w_down):
    cos, sin = rope_tables(positions)
    split_idx = np.cumsum(IN_SPLITS)[:-1].tolist()
    for l in range(DEPTH):
        mod = (c @ w_ada[l] + b_ada[l])[:, None, :]
        shift1, scale1, gate1, shift2, scale2, gate2 = jnp.split(mod, N_MOD, axis=-1)

        h = rms_norm(x, norm1_g[l]) * (1.0 + scale1) + shift1
        proj = h @ w_in[l]
        c_q, c_kv_all, hq, hf, hi, hg, ga, gb = jnp.split(proj, split_idx, axis=-1)

        y_a = mla_branch(c_q, c_kv_all, cos, sin, q_a_norm_g[l], w_uq[l], kv_a_norm_g[l],
                         w_ukv[l], q_norm_g[l], k_norm_g[l]) @ w_branch_a[l]
        lb = hgrn2_lower_bound(hg_lower_bound, l)
        y_b = hgrn2_branch(hq, hf, hi, hg, lb, hg_out_norm_g[l]) @ w_branch_b[l]

        merged = jax.nn.sigmoid(ga) * y_a + jax.nn.sigmoid(gb) * y_b
        x = x + gate1 * (merged @ w_out[l])

        h2 = rms_norm(x, norm2_g[l]) * (1.0 + scale2) + shift2
        up = causal_depthwise_conv(h2 @ w_up[l], conv_w[l], conv_b[l])
        u_gate, u_val = up[..., :D_FF], up[..., D_FF:]
        x = x + gate2 * ((jax.nn.silu(u_gate) * u_val) @ w_down[l])
    return x
```

```python
import functools

import numpy as np
import jax
import jax.numpy as jnp
from jax import lax
from jax.experimental import pallas as pl
from jax.experimental.pallas import tpu as pltpu

F32 = jnp.float32
BF16 = jnp.bfloat16

LANES = 128
SUBLANES = 8
VMEM_LIMIT = 56 * 1024 * 1024

MLA_HEADS = 8
NOPE_DIM = 64
ROPE_DIM = 32
HALF_ROPE = ROPE_DIM // 2
QK_DIM = NOPE_DIM + ROPE_DIM
V_DIM = 64
Q_LORA = 512
KV_LORA = 256
ROPE_THETA = 10000.0
HG_HEADS = 4
HG_DIM = 128
HG_WIDTH = HG_HEADS * HG_DIM
CONV_W = 3
EPS = 1e-6
HEAD_PAD = LANES
LOG2E = float(np.log2(np.e))
NEG_BIG = -1e30


def _dot(a, b):
    return jnp.dot(a, b, preferred_element_type=F32)


def _dot_nt(a, b):
    return lax.dot_general(a, b, (((1,), (1,)), ((), ())), preferred_element_type=F32)


def _dot_tn(a, b):
    return lax.dot_general(a, b, (((0,), (0,)), ((), ())), preferred_element_type=F32)


def _sigmoid(x):
    return 1.0 / (1.0 + jnp.exp(-x))


def _rms(x, width):
    return lax.rsqrt(jnp.sum(x * x, axis=-1, keepdims=True) * (1.0 / width) + EPS)


def _const_spec(shape):
    nd = len(shape)
    return pl.BlockSpec(shape, lambda *_: (0,) * nd, pipeline_mode=pl.Buffered(1))


def _params(sem):
    return pltpu.CompilerParams(dimension_semantics=sem, vmem_limit_bytes=VMEM_LIMIT)


def _ada_kernel(c_ref, w_ref, b_ref, o_ref):
    c = c_ref[...]
    w = w_ref[...]
    c_hi = c.astype(BF16)
    c_lo = (c - c_hi.astype(F32)).astype(BF16)
    w_hi = w.astype(BF16)
    w_lo = (w - w_hi.astype(F32)).astype(BF16)
    o_ref[...] = _dot(c_hi, w_hi) + _dot(c_hi, w_lo) + _dot(c_lo, w_hi) + b_ref[...]


def _ada_mod(c, w_ada, b_ada):
    bsz, d = c.shape
    n = w_ada.shape[1]
    bn = 1024
    return pl.pallas_call(
        _ada_kernel,
        grid=(n // bn,),
        in_specs=[pl.BlockSpec((bsz, d), lambda j: (0, 0)),
                  pl.BlockSpec((d, bn), lambda j: (0, j)),
                  pl.BlockSpec((1, bn), lambda j: (0, j))],
        out_specs=pl.BlockSpec((bsz, bn), lambda j: (0, j)),
        out_shape=jax.ShapeDtypeStruct((bsz, n), F32),
        compiler_params=_params(("arbitrary",)),
        name="ada_mod",
    )(c, w_ada, b_ada.reshape(1, n))


def _rope(t, cos_t, sin_lo, sin_hi):
    return (t * cos_t + pltpu.roll(t, LANES - HALF_ROPE, 1) * sin_lo
            + pltpu.roll(t, HALF_ROPE, 1) * sin_hi)


def _mixer_in_kernel(x_ref, pos_ref, invf_ref, mod_ref, g1_ref, wcq_ref, wckv_ref, wh_ref, wg_ref,
                     gqa_ref, wuq_ref, gkva_ref, wk_ref, wv_ref, gq_ref, gk_ref,
                     q_ref, k_ref, v_ref, hq_ref, hf_ref, hi_ref, hg_ref, ga_ref, gb_ref):
    tm, d = x_ref.shape
    x = x_ref[...]
    h = (x * _rms(x, d) * g1_ref[...]) * (1.0 + mod_ref[1:2, :]) + mod_ref[0:1, :]
    hb = h.astype(BF16)

    ang = invf_ref[...] * pos_ref[...].astype(F32)
    cos_h = jnp.cos(ang)
    sin_h = jnp.sin(ang)
    zeros = lambda r: jnp.zeros((r, tm), F32)
    cos_t = jnp.concatenate([jnp.ones((NOPE_DIM, tm), F32), cos_h, cos_h,
                             zeros(HEAD_PAD - QK_DIM)], axis=0).T
    sin_lo = jnp.concatenate([zeros(NOPE_DIM), -sin_h, zeros(HEAD_PAD - NOPE_DIM - HALF_ROPE)],
                             axis=0).T
    sin_hi = jnp.concatenate([zeros(NOPE_DIM + HALF_ROPE), sin_h, zeros(HEAD_PAD - QK_DIM)],
                             axis=0).T

    cq = _dot(hb, wcq_ref[...])
    cqn = (cq * _rms(cq, Q_LORA) * gqa_ref[...]).astype(BF16)
    qf = _dot(cqn, wuq_ref[...])
    gq = gq_ref[...]
    for hd in range(MLA_HEADS):
        sl = slice(hd * HEAD_PAD, (hd + 1) * HEAD_PAD)
        qh = qf[:, sl]
        qn = qh * _rms(qh, QK_DIM) * gq
        q_ref[:, sl] = (_rope(qn, cos_t, sin_lo, sin_hi) * (LOG2E / float(np.sqrt(QK_DIM)))).astype(BF16)

    ckv = _dot(hb, wckv_ref[...])
    ckv_c = ckv[:, :KV_LORA]
    kr = ckv[:, KV_LORA:]
    ckvn = (ckv_c * _rms(ckv_c, KV_LORA) * gkva_ref[...]).astype(BF16)
    kf = _dot(ckvn, wk_ref[...])
    vf = _dot(ckvn, wv_ref[...])
    gk = gk_ref[...]
    ss_rope = jnp.sum(kr * kr, axis=-1, keepdims=True)
    kr_rot = _rope(kr * gk, cos_t, sin_lo, sin_hi)
    for hd in range(MLA_HEADS):
        sl = slice(hd * HEAD_PAD, (hd + 1) * HEAD_PAD)
        kn = kf[:, sl]
        ss = jnp.sum(kn * kn, axis=-1, keepdims=True) + ss_rope
        k_ref[:, sl] = ((kn * gk + kr_rot) * lax.rsqrt(ss * (1.0 / QK_DIM) + EPS)).astype(BF16)
    lane = lax.broadcasted_iota(jnp.int32, (1, MLA_HEADS * HEAD_PAD), 1)
    ones_col = jnp.where((lane & (HEAD_PAD - 1)) == V_DIM, 1.0, 0.0)
    v_ref[...] = (vf + ones_col).astype(BF16)

    hh = _dot(hb, wh_ref[...])
    hq_ref[...] = hh[:, 0 * HG_WIDTH:1 * HG_WIDTH].astype(BF16)
    hf_ref[...] = hh[:, 1 * HG_WIDTH:2 * HG_WIDTH]
    hi_ref[...] = hh[:, 2 * HG_WIDTH:3 * HG_WIDTH].astype(BF16)
    hg_ref[...] = hh[:, 3 * HG_WIDTH:4 * HG_WIDTH].astype(BF16)
    gg = _dot(hb, wg_ref[...])
    ga_ref[...] = gg[:, :d].astype(BF16)
    gb_ref[...] = gg[:, d:].astype(BF16)


def _mixer_in(x2, pos, invf, mod, g1, wcq, wckv, wh, wg, gqa, wuq, gkva, wk, wv, gq, gk, *, seq, tm):
    t, d = x2.shape
    tps = seq // tm
    hw = MLA_HEADS * HEAD_PAD
    row = lambda w: pl.BlockSpec((tm, w), lambda i: (i, 0))
    consts = [g1, wcq, wckv, wh, wg, gqa, wuq, gkva, wk, wv, gq, gk]
    out_shapes = [jax.ShapeDtypeStruct((t, hw), BF16)] * 3 + [
        jax.ShapeDtypeStruct((t, HG_WIDTH), BF16), jax.ShapeDtypeStruct((t, HG_WIDTH), F32),
        jax.ShapeDtypeStruct((t, HG_WIDTH), BF16), jax.ShapeDtypeStruct((t, HG_WIDTH), BF16),
        jax.ShapeDtypeStruct((t, d), BF16), jax.ShapeDtypeStruct((t, d), BF16)]
    return pl.pallas_call(
        _mixer_in_kernel,
        grid=(t // tm,),
        in_specs=[row(d),
                  pl.BlockSpec((None, 1, tm), lambda i: (i, 0, 0)),
                  _const_spec(invf.shape),
                  pl.BlockSpec((None, 6, d), lambda i: (i // tps, 0, 0))]
                 + [_const_spec(a.shape) for a in consts],
        out_specs=[row(hw)] * 3 + [row(HG_WIDTH)] * 4 + [row(d)] * 2,
        out_shape=out_shapes,
        compiler_params=_params(("arbitrary",)),
        name="mixer_in",
    )(x2, pos, invf, mod, *consts)


def _attn_kernel(q_ref, k_ref, v_ref, o_ref, *, tq):
    i = pl.program_id(2)
    row = lax.broadcasted_iota(jnp.int32, (tq, tq), 0)
    col = lax.broadcasted_iota(jnp.int32, (tq, tq), 1)
    outs = []
    for hd in range(2):
        sl = slice(hd * HEAD_PAD, (hd + 1) * HEAD_PAD)
        q = q_ref[:, sl]

        def block(j, carry, diagonal):
            m, acc = carry
            start = pl.multiple_of(j * tq, tq)
            s = _dot_nt(q, k_ref[pl.ds(start, tq), sl])
            if diagonal:
                s = jnp.where(col <= row, s, NEG_BIG)
            m_new = jnp.maximum(m, jnp.max(s, axis=-1, keepdims=True))
            p = jnp.exp2(s - m_new).astype(BF16)
            acc = acc * jnp.exp2(m - m_new) + _dot(p, v_ref[pl.ds(start, tq), sl])
            return m_new, acc

        carry = (jnp.full((tq, 1), NEG_BIG, F32), jnp.zeros((tq, HEAD_PAD), F32))
        carry = lax.fori_loop(0, i, functools.partial(block, diagonal=False), carry)
        _, acc = block(i, carry, True)
        outs.append(acc[:, :V_DIM] / acc[:, V_DIM:V_DIM + 1])
    o_ref[...] = jnp.concatenate(outs, axis=-1).astype(o_ref.dtype)


def _mla_attention(q, k, v, *, tq):
    bsz, seq, hw = q.shape
    pairs = MLA_HEADS // 2
    pw = 2 * HEAD_PAD
    kv_spec = pl.BlockSpec((None, seq, pw), lambda b, h, i: (b, 0, h))
    return pl.pallas_call(
        functools.partial(_attn_kernel, tq=tq),
        grid=(bsz, pairs, seq // tq),
        in_specs=[pl.BlockSpec((None, tq, pw), lambda b, h, i: (b, i, h)), kv_spec, kv_spec],
        out_specs=pl.BlockSpec((None, tq, 2 * V_DIM), lambda b, h, i: (b, i, h)),
        out_shape=jax.ShapeDtypeStruct((bsz, seq, MLA_HEADS * V_DIM), BF16),
        compiler_params=_params(("arbitrary", "arbitrary", "arbitrary")),
        name="mla_attn",
    )(q, k, v)


def _hgrn_kernel(hq_ref, hf_ref, hi_ref, hg_ref, lbt_ref, gout_ref, o_ref,
                 st_ref, b_scr, kk_scr, ad_scr, *, layer):
    c, w = hq_ref.shape
    ci = pl.program_id(1)

    @pl.when(ci == 0)
    def _():
        st_ref[...] = jnp.zeros_like(st_ref)

    tab = lbt_ref[...]
    e = jnp.exp(tab - jnp.max(tab, axis=0, keepdims=True))
    lb = jnp.sum(e[1:layer + 2], axis=0, keepdims=True) / jnp.sum(e, axis=0, keepdims=True)

    hq = hq_ref[...].astype(F32)
    q = hq * _sigmoid(hq)
    f = lb + (1.0 - lb) * _sigmoid(hf_ref[...])
    lf = jnp.log(f)
    kk = 1.0 - f
    v = hi_ref[...]

    t_i = lax.broadcasted_iota(jnp.int32, (c, c), 0)
    s_i = lax.broadcasted_iota(jnp.int32, (c, c), 1)
    tri = jnp.where(t_i >= s_i, 1.0, 0.0).astype(BF16)
    lf_hi = lf.astype(BF16)
    rem = lf - lf_hi.astype(F32)
    lf_mid = rem.astype(BF16)
    lf_lo = (rem - lf_mid.astype(F32)).astype(BF16)
    b = _dot(tri, lf_hi) + _dot(tri, lf_mid) + _dot(tri, lf_lo)
    b_scr[...] = b
    kk_scr[...] = kk

    def brow(r, rows):
        return jnp.broadcast_to(b_scr[pl.ds(r, 1), :], (rows, w))

    levels = []
    blk = c // 2
    while blk >= SUBLANES:
        qs, ks = [], []
        for j in range(c // blk):
            rs = slice(j * blk, (j + 1) * blk)
            if j % 2:
                qs.append(q[rs] * jnp.exp(b[rs] - brow(j * blk - 1, blk)))
                ks.append(jnp.zeros((blk, w), F32))
            else:
                qs.append(jnp.zeros((blk, w), F32))
                ks.append(kk[rs] * jnp.exp(brow((j + 1) * blk - 1, blk) - b[rs]))
        levels.append((blk, jnp.concatenate(qs, axis=0).astype(BF16),
                       jnp.concatenate(ks, axis=0).astype(BF16)))
        blk //= 2

    lane = lax.broadcasted_iota(jnp.int32, (SUBLANES, HG_DIM), 1)
    rowi = lax.broadcasted_iota(jnp.int32, (SUBLANES, HG_DIM), 0)
    for rb in range(c // SUBLANES):
        rs = slice(rb * SUBLANES, (rb + 1) * SUBLANES)
        q_b = q[rs]
        b_b = b[rs]
        diag = [jnp.zeros((SUBLANES, HG_DIM), F32) for _ in range(HG_HEADS)]
        for sl in range(SUBLANES):
            s = rb * SUBLANES + sl
            dec = jnp.exp(jnp.minimum(b_b - b_scr[pl.ds(s, 1), :], 0.0))
            wgt = q_b * dec * kk_scr[pl.ds(s, 1), :]
            for hd in range(HG_HEADS):
                a = jnp.sum(wgt[:, hd * HG_DIM:(hd + 1) * HG_DIM], axis=-1, keepdims=True)
                diag[hd] = jnp.where(lane == s, jnp.where(rowi >= sl, a, 0.0), diag[hd])
        for hd in range(HG_HEADS):
            ad_scr[hd, rs, :] = diag[hd]

    b_last = b_scr[pl.ds(c - 1, 1), :]
    q_dec = (q * jnp.exp(b)).astype(BF16)
    k_last = (kk * jnp.exp(b_last - b)).astype(BF16)
    xor = t_i ^ s_i
    lower = t_i > s_i
    gout = gout_ref[...]
    for hd in range(HG_HEADS):
        hs = slice(hd * HG_DIM, (hd + 1) * HG_DIM)
        attn = ad_scr[hd]
        for blk, qc, kc in levels:
            shift = int(np.log2(blk))
            pc = _dot_nt(qc[:, hs], kc[:, hs])
            attn = jnp.where((xor >> shift) == 1, jnp.where(lower, pc, attn), attn)
        st = st_ref[hd]
        o = _dot_nt(q_dec[:, hs], st.astype(BF16)) + _dot(attn.astype(BF16), v[:, hs])
        hg = hg_ref[:, hs].astype(F32)
        o = o * _rms(o, HG_DIM) * gout * (hg * _sigmoid(hg))
        o_ref[:, hs] = o.astype(o_ref.dtype)
        st_ref[hd] = st * jnp.exp(b_last[:, hs]) + _dot_tn(v[:, hs], k_last[:, hs])


def _hgrn2(hq, hf, hi, hg, lb_table, gout, *, chunk, layer):
    bsz, seq, w = hq.shape
    blk = pl.BlockSpec((None, chunk, w), lambda b, c: (b, c, 0))
    return pl.pallas_call(
        functools.partial(_hgrn_kernel, layer=layer),
        grid=(bsz, seq // chunk),
        in_specs=[blk, blk, blk, blk, _const_spec(lb_table.shape), _const_spec(gout.shape)],
        out_specs=blk,
        out_shape=jax.ShapeDtypeStruct((bsz, seq, w), BF16),
        scratch_shapes=[pltpu.VMEM((HG_HEADS, HG_DIM, HG_DIM), F32),
                        pltpu.VMEM((chunk, w), F32),
                        pltpu.VMEM((chunk, w), F32),
                        pltpu.VMEM((HG_HEADS, chunk, chunk), F32)],
        compiler_params=_params(("arbitrary", "arbitrary")),
        name="hgrn2",
    )(hq, hf, hi, hg, lb_table, gout)


def _merge_kernel(a_ref, o_ref, ga_ref, gb_ref, x_ref, mod_ref, wa_ref, wb_ref, wo_ref, g2_ref,
                  x1_ref, h2_ref):
    d = x_ref.shape[1]
    ya = _dot(a_ref[...], wa_ref[...])
    yb = _dot(o_ref[...], wb_ref[...])
    merged = _sigmoid(ga_ref[...].astype(F32)) * ya + _sigmoid(gb_ref[...].astype(F32)) * yb
    x1 = x_ref[...] + mod_ref[2:3, :] * _dot(merged.astype(BF16), wo_ref[...])
    x1_ref[...] = x1
    h2 = (x1 * _rms(x1, d) * g2_ref[...]) * (1.0 + mod_ref[4:5, :]) + mod_ref[3:4, :]
    h2_ref[...] = h2.astype(BF16)


def _merge_out(attn, hgo, ga, gb, x2, mod, wa, wb, wo, g2, *, seq, tm):
    t, d = x2.shape
    tps = seq // tm
    row = lambda w: pl.BlockSpec((tm, w), lambda i: (i, 0))
    consts = [wa, wb, wo, g2]
    return pl.pallas_call(
        _merge_kernel,
        grid=(t // tm,),
        in_specs=[row(attn.shape[1]), row(hgo.shape[1]), row(d), row(d), row(d),
                  pl.BlockSpec((None, 6, d), lambda i: (i // tps, 0, 0))]
                 + [_const_spec(a.shape) for a in consts],
        out_specs=[row(d), row(d)],
        out_shape=[jax.ShapeDtypeStruct((t, d), F32), jax.ShapeDtypeStruct((t, d), BF16)],
        compiler_params=_params(("arbitrary",)),
        name="merge_out",
    )(attn, hgo, ga, gb, x2, mod, *consts)


def _ffn_kernel(h2_ref, x1_ref, mod_ref, wup_ref, cw_ref, cb_ref, wdn_ref, out_ref, up_scr, *, tps):
    tm = h2_ref.shape[0]
    dff = wdn_ref.shape[0]
    halo = SUBLANES
    i = pl.program_id(0)

    @pl.when(i % tps == 0)
    def _():
        up_scr[0:halo, :] = jnp.zeros((halo, up_scr.shape[1]), F32)

    up = _dot(h2_ref[...], wup_ref[...])
    up_scr[halo:halo + tm, :] = up
    y = cb_ref[...] + cw_ref[2:3, :] * up
    for j in range(CONV_W - 1):
        back = CONV_W - 1 - j
        y = y + cw_ref[j:j + 1, :] * up_scr[halo - back:halo - back + tm, :]
    up_scr[0:halo, :] = up[tm - halo:, :]
    gate = y[:, :dff]
    act = (gate * _sigmoid(gate) * y[:, dff:]).astype(BF16)
    out_ref[...] = x1_ref[...] + mod_ref[5:6, :] * _dot(act, wdn_ref[...])


def _conv_ffn(h2, x1, mod, wup, cw, cb, wdn, *, seq, tm):
    t, d = x1.shape
    tps = seq // tm
    row = lambda w: pl.BlockSpec((tm, w), lambda i: (i, 0))
    consts = [wup, cw, cb, wdn]
    return pl.pallas_call(
        functools.partial(_ffn_kernel, tps=tps),
        grid=(t // tm,),
        in_specs=[row(d), row(d), pl.BlockSpec((None, 6, d), lambda i: (i // tps, 0, 0))]
                 + [_const_spec(a.shape) for a in consts],
        out_specs=row(d),
        out_shape=jax.ShapeDtypeStruct((t, d), F32),
        scratch_shapes=[pltpu.VMEM((tm + SUBLANES, wup.shape[1]), F32)],
        compiler_params=_params(("arbitrary",)),
        name="conv_ffn",
    )(h2, x1, mod, *consts)


def _pad_heads(w, used, n_heads):
    rows = w.shape[0]
    w = w.reshape(rows, n_heads, used)
    w = jnp.pad(w, ((0, 0), (0, 0), (0, HEAD_PAD - used)))
    return w.reshape(rows, n_heads * HEAD_PAD)


def _layer(x2, pos, invf, mod, p, *, bsz, seq, layer):
    t, d = x2.shape
    w_in = p["w_in"]
    o_ckv = Q_LORA
    o_h = o_ckv + KV_LORA + ROPE_DIM
    o_g = o_h + 4 * HG_WIDTH
    wcq = w_in[:, :o_ckv].astype(BF16)
    wckv = jnp.concatenate([w_in[:, o_ckv:o_ckv + KV_LORA],
                            jnp.zeros((d, NOPE_DIM), F32),
                            w_in[:, o_ckv + KV_LORA:o_h],
                            jnp.zeros((d, HEAD_PAD - QK_DIM), F32)], axis=1).astype(BF16)
    wh = w_in[:, o_h:o_g].astype(BF16)
    wg = w_in[:, o_g:].astype(BF16)
    wuq = _pad_heads(p["w_uq"], QK_DIM, MLA_HEADS).astype(BF16)
    wukv = p["w_ukv"].reshape(KV_LORA, MLA_HEADS, NOPE_DIM + V_DIM)
    wk = _pad_heads(wukv[:, :, :NOPE_DIM].reshape(KV_LORA, -1), NOPE_DIM, MLA_HEADS).astype(BF16)
    wv = _pad_heads(wukv[:, :, NOPE_DIM:].reshape(KV_LORA, -1), V_DIM, MLA_HEADS).astype(BF16)
    pad_g = lambda g: jnp.pad(g, (0, HEAD_PAD - QK_DIM)).reshape(1, HEAD_PAD)

    tm = min(256, seq)
    q, k, v, hq, hf, hi, hg, ga, gb = _mixer_in(
        x2, pos.reshape(t // tm, 1, tm), invf, mod, p["norm1_g"].reshape(1, d), wcq, wckv, wh, wg,
        p["q_a_norm_g"].reshape(1, -1), wuq, p["kv_a_norm_g"].reshape(1, -1), wk, wv,
        pad_g(p["q_norm_g"]), pad_g(p["k_norm_g"]), seq=seq, tm=tm)

    hw = MLA_HEADS * HEAD_PAD
    attn = _mla_attention(q.reshape(bsz, seq, hw), k.reshape(bsz, seq, hw), v.reshape(bsz, seq, hw),
                          tq=min(512, seq))
    r3 = lambda a: a.reshape(bsz, seq, HG_WIDTH)
    hgo = _hgrn2(r3(hq), r3(hf), r3(hi), r3(hg), p["hg_lower_bound"],
                 p["hg_out_norm_g"].reshape(1, HG_DIM), chunk=min(128, seq), layer=layer)

    x1, h2 = _merge_out(attn.reshape(t, -1), hgo.reshape(t, -1), ga, gb, x2, mod,
                        p["w_branch_a"].astype(BF16), p["w_branch_b"].astype(BF16),
                        p["w_out"].astype(BF16), p["norm2_g"].reshape(1, d), seq=seq, tm=tm)
    return _conv_ffn(h2, x1, mod, p["w_up"].astype(BF16), p["conv_w"], p["conv_b"].reshape(1, -1),
                     p["w_down"].astype(BF16), seq=seq, tm=tm)


def kernel(x, c, positions, w_ada, b_ada, norm1_g, w_in, q_a_norm_g, w_uq, kv_a_norm_g, w_ukv,
           q_norm_g, k_norm_g, hg_lower_bound, hg_out_norm_g, w_branch_a, w_branch_b, w_out,
           norm2_g, w_up, conv_w, conv_b, w_down):
    bsz, seq, d = x.shape
    depth = w_ada.shape[0]
    invf = (ROPE_THETA ** (-jnp.arange(0, ROPE_DIM, 2, dtype=F32) / ROPE_DIM)).reshape(HALF_ROPE, 1)
    x2 = x.reshape(bsz * seq, d)
    pos = positions.reshape(bsz * seq)
    for l in range(depth):
        mod = _ada_mod(c, w_ada[l], b_ada[l]).reshape(bsz, 6, d)
        p = dict(norm1_g=norm1_g[l], w_in=w_in[l], q_a_norm_g=q_a_norm_g[l], w_uq=w_uq[l],
                 kv_a_norm_g=kv_a_norm_g[l], w_ukv=w_ukv[l], q_norm_g=q_norm_g[l],
                 k_norm_g=k_norm_g[l], hg_lower_bound=hg_lower_bound, hg_out_norm_g=hg_out_norm_g[l],
                 w_branch_a=w_branch_a[l], w_branch_b=w_branch_b[l], w_out=w_out[l],
                 norm2_g=norm2_g[l], w_up=w_up[l], conv_w=conv_w[l], conv_b=conv_b[l],
                 w_down=w_down[l])
        x2 = _layer(x2, pos, invf, mod, p, bsz=bsz, seq=seq, layer=l)
    return x2.reshape(bsz, seq, d)
```

```python
import functools

import numpy as np
import jax
import jax.numpy as jnp
from jax import lax
from jax.experimental import pallas as pl
from jax.experimental.pallas import tpu as pltpu

F32 = jnp.float32
BF16 = jnp.bfloat16

LANES = 128
SUBLANES = 8
VMEM_LIMIT = 56 * 1024 * 1024

MLA_HEADS = 8
NOPE_DIM = 64
ROPE_DIM = 32
HALF_ROPE = ROPE_DIM // 2
QK_DIM = NOPE_DIM + ROPE_DIM
V_DIM = 64
Q_LORA = 512
KV_LORA = 256
ROPE_THETA = 10000.0
HG_HEADS = 4
HG_DIM = 128
HG_WIDTH = HG_HEADS * HG_DIM
CONV_W = 3
EPS = 1e-6
HEAD_PAD = LANES
LOG2E = float(np.log2(np.e))
NEG_BIG = -1e30


def _dot(a, b):
    return jnp.dot(a, b, preferred_element_type=F32)


def _dot_nt(a, b):
    return lax.dot_general(a, b, (((1,), (1,)), ((), ())), preferred_element_type=F32)


def _dot_tn(a, b):
    return lax.dot_general(a, b, (((0,), (0,)), ((), ())), preferred_element_type=F32)


def _sigmoid(x):
    return 1.0 / (1.0 + jnp.exp(-x))


def _rms(x, width):
    return lax.rsqrt(jnp.sum(x * x, axis=-1, keepdims=True) * (1.0 / width) + EPS)


def _const_spec(shape):
    nd = len(shape)
    return pl.BlockSpec(shape, lambda *_: (0,) * nd, pipeline_mode=pl.Buffered(1))


def _params(sem):
    return pltpu.CompilerParams(dimension_semantics=sem, vmem_limit_bytes=VMEM_LIMIT)


def _ada_kernel(c_ref, w_ref, b_ref, o_ref):
    c = c_ref[...]
    w = w_ref[...]
    c_hi = c.astype(BF16)
    c_lo = (c - c_hi.astype(F32)).astype(BF16)
    w_hi = w.astype(BF16)
    w_lo = (w - w_hi.astype(F32)).astype(BF16)
    o_ref[...] = _dot(c_hi, w_hi) + _dot(c_hi, w_lo) + _dot(c_lo, w_hi) + b_ref[...]


def _ada_mod(c, w_ada, b_ada):
    bsz, d = c.shape
    n = w_ada.shape[1]
    bn = 1024
    return pl.pallas_call(
        _ada_kernel,
        grid=(n // bn,),
        in_specs=[pl.BlockSpec((bsz, d), lambda j: (0, 0)),
                  pl.BlockSpec((d, bn), lambda j: (0, j)),
                  pl.BlockSpec((1, bn), lambda j: (0, j))],
        out_specs=pl.BlockSpec((bsz, bn), lambda j: (0, j)),
        out_shape=jax.ShapeDtypeStruct((bsz, n), F32),
        compiler_params=_params(("arbitrary",)),
        name="ada_mod",
    )(c, w_ada, b_ada.reshape(1, n))


def _rope(t, cos_t, sin_lo, sin_hi):
    return (t * cos_t + pltpu.roll(t, LANES - HALF_ROPE, 1) * sin_lo
            + pltpu.roll(t, HALF_ROPE, 1) * sin_hi)


def _mixer_in_kernel(x_ref, pos_ref, invf_ref, mod_ref, g1_ref, wcq_ref, wckv_ref, wh_ref, wg_ref,
                     gqa_ref, wuq_ref, gkva_ref, wk_ref, wv_ref, gq_ref, gk_ref,
                     q_ref, k_ref, v_ref, hq_ref, hf_ref, hi_ref, hg_ref, ga_ref, gb_ref):
    tm, d = x_ref.shape
    x = x_ref[...]
    h = (x * _rms(x, d) * g1_ref[...]) * (1.0 + mod_ref[1:2, :]) + mod_ref[0:1, :]
    hb = h.astype(BF16)

    ang = invf_ref[...] * pos_ref[...].astype(F32)
    cos_h = jnp.cos(ang)
    sin_h = jnp.sin(ang)
    zeros = lambda r: jnp.zeros((r, tm), F32)
    cos_t = jnp.concatenate([jnp.ones((NOPE_DIM, tm), F32), cos_h, cos_h,
                             zeros(HEAD_PAD - QK_DIM)], axis=0).T
    sin_lo = jnp.concatenate([zeros(NOPE_DIM), -sin_h, zeros(HEAD_PAD - NOPE_DIM - HALF_ROPE)],
                             axis=0).T
    sin_hi = jnp.concatenate([zeros(NOPE_DIM + HALF_ROPE), sin_h, zeros(HEAD_PAD - QK_DIM)],
                             axis=0).T

    cq = _dot(hb, wcq_ref[...])
    cqn = (cq * _rms(cq, Q_LORA) * gqa_ref[...]).astype(BF16)
    qf = _dot(cqn, wuq_ref[...])
    gq = gq_ref[...]
    for hd in range(MLA_HEADS):
        sl = slice(hd * HEAD_PAD, (hd + 1) * HEAD_PAD)
        qh = qf[:, sl]
        qn = qh * _rms(qh, QK_DIM) * gq
        q_ref[:, sl] = (_rope(qn, cos_t, sin_lo, sin_hi) * (LOG2E / float(np.sqrt(QK_DIM)))).astype(BF16)

    ckv = _dot(hb, wckv_ref[...])
    ckv_c = ckv[:, :KV_LORA]
    kr = ckv[:, KV_LORA:]
    ckvn = (ckv_c * _rms(ckv_c, KV_LORA) * gkva_ref[...]).astype(BF16)
    kf = _dot(ckvn, wk_ref[...])
    vf = _dot(ckvn, wv_ref[...])
    gk = gk_ref[...]
    ss_rope = jnp.sum(kr * kr, axis=-1, keepdims=True)
    kr_rot = _rope(kr * gk, cos_t, sin_lo, sin_hi)
    for hd in range(MLA_HEADS):
        sl = slice(hd * HEAD_PAD, (hd + 1) * HEAD_PAD)
        kn = kf[:, sl]
        ss = jnp.sum(kn * kn, axis=-1, keepdims=True) + ss_rope
        k_ref[:, sl] = ((kn * gk + kr_rot) * lax.rsqrt(ss * (1.0 / QK_DIM) + EPS)).astype(BF16)
    lane = lax.broadcasted_iota(jnp.int32, (1, MLA_HEADS * HEAD_PAD), 1)
    ones_col = jnp.where((lane & (HEAD_PAD - 1)) == V_DIM, 1.0, 0.0)
    v_ref[...] = (vf + ones_col).astype(BF16)

    hh = _dot(hb, wh_ref[...])
    hq_ref[...] = hh[:, 0 * HG_WIDTH:1 * HG_WIDTH].astype(BF16)
    hf_ref[...] = hh[:, 1 * HG_WIDTH:2 * HG_WIDTH]
    hi_ref[...] = hh[:, 2 * HG_WIDTH:3 * HG_WIDTH].astype(BF16)
    hg_ref[...] = hh[:, 3 * HG_WIDTH:4 * HG_WIDTH].astype(BF16)
    gg = _dot(hb, wg_ref[...])
    ga_ref[...] = gg[:, :d].astype(BF16)
    gb_ref[...] = gg[:, d:].astype(BF16)


def _mixer_in(x2, pos, invf, mod, g1, wcq, wckv, wh, wg, gqa, wuq, gkva, wk, wv, gq, gk, *, seq, tm):
    t, d = x2.shape
    tps = seq // tm
    hw = MLA_HEADS * HEAD_PAD
    row = lambda w: pl.BlockSpec((tm, w), lambda i: (i, 0))
    consts = [g1, wcq, wckv, wh, wg, gqa, wuq, gkva, wk, wv, gq, gk]
    out_shapes = [jax.ShapeDtypeStruct((t, hw), BF16)] * 3 + [
        jax.ShapeDtypeStruct((t, HG_WIDTH), BF16), jax.ShapeDtypeStruct((t, HG_WIDTH), F32),
        jax.ShapeDtypeStruct((t, HG_WIDTH), BF16), jax.ShapeDtypeStruct((t, HG_WIDTH), BF16),
        jax.ShapeDtypeStruct((t, d), BF16), jax.ShapeDtypeStruct((t, d), BF16)]
    return pl.pallas_call(
        _mixer_in_kernel,
        grid=(t // tm,),
        in_specs=[row(d),
                  pl.BlockSpec((None, 1, tm), lambda i: (i, 0, 0)),
                  _const_spec(invf.shape),
                  pl.BlockSpec((None, 6, d), lambda i: (i // tps, 0, 0))]
                 + [_const_spec(a.shape) for a in consts],
        out_specs=[row(hw)] * 3 + [row(HG_WIDTH)] * 4 + [row(d)] * 2,
        out_shape=out_shapes,
        compiler_params=_params(("arbitrary",)),
        name="mixer_in",
    )(x2, pos, invf, mod, *consts)


def _attn_kernel(q_ref, k_ref, v_ref, o_ref, m_scr, acc_scr, *, tq, tk, heads):
    i = pl.program_id(2)
    nsub = tq // tk
    head_slices = [slice(hd * HEAD_PAD, (hd + 1) * HEAD_PAD) for hd in range(heads)]
    m_scr[...] = jnp.full(m_scr.shape, NEG_BIG, F32)
    acc_scr[...] = jnp.zeros(acc_scr.shape, F32)

    def block(j, row0, diagonal):
        start = pl.multiple_of(j * tk, tk)
        rows = slice(row0, tq)
        if diagonal:
            row = lax.broadcasted_iota(jnp.int32, (tq - row0, tk), 0)
            col = lax.broadcasted_iota(jnp.int32, (tq - row0, tk), 1)

        def scores(hd):
            sl = head_slices[hd]
            s = _dot_nt(q_ref[rows, sl], k_ref[pl.ds(start, tk), sl])
            return jnp.where(col <= row, s, NEG_BIG) if diagonal else s

        def update(hd, s):
            m_prev = m_scr[hd, rows, :]
            m_new = jnp.maximum(m_prev, jnp.max(s, axis=-1, keepdims=True))
            p = jnp.exp2(s - jnp.concatenate([m_new] * (tk // LANES), axis=1)).astype(BF16)
            acc_scr[hd, rows, :] = (acc_scr[hd, rows, :] * jnp.exp2(m_prev - m_new)
                                    + _dot(p, v_ref[pl.ds(start, tk), head_slices[hd]]))
            m_scr[hd, rows, :] = m_new

        s_next = scores(0)
        for hd in range(heads):
            s_cur = s_next
            if hd + 1 < heads:
                s_next = scores(hd + 1)
            update(hd, s_cur)

    def body(j, carry):
        block(j, 0, False)
        return carry

    lax.fori_loop(0, i * nsub, body, 0)
    for r in range(nsub):
        block(i * nsub + r, r * tk, True)
    outs = []
    for hd in range(heads):
        acc = acc_scr[hd]
        outs.append(acc[:, :V_DIM] / acc[:, V_DIM:V_DIM + 1])
    o_ref[...] = jnp.concatenate(outs, axis=-1).astype(o_ref.dtype)


def _mla_attention(q, k, v, *, tq, tk, heads):
    bsz, seq, hw = q.shape
    groups = MLA_HEADS // heads
    pw = heads * HEAD_PAD
    kv_spec = pl.BlockSpec((None, seq, pw), lambda b, h, i: (b, 0, h))
    return pl.pallas_call(
        functools.partial(_attn_kernel, tq=tq, tk=tk, heads=heads),
        grid=(bsz, groups, seq // tq),
        in_specs=[pl.BlockSpec((None, tq, pw), lambda b, h, i: (b, i, h)), kv_spec, kv_spec],
        out_specs=pl.BlockSpec((None, tq, heads * V_DIM), lambda b, h, i: (b, i, h)),
        out_shape=jax.ShapeDtypeStruct((bsz, seq, MLA_HEADS * V_DIM), BF16),
        scratch_shapes=[pltpu.VMEM((heads, tq, LANES), F32), pltpu.VMEM((heads, tq, HEAD_PAD), F32)],
        compiler_params=_params(("arbitrary", "arbitrary", "arbitrary")),
        name="mla_attn",
    )(q, k, v)


def _hgrn_kernel(hq_ref, hf_ref, hi_ref, hg_ref, lbt_ref, gout_ref, o_ref,
                 st_ref, b_scr, kk_scr, ad_scr, *, layer):
    c, w = hq_ref.shape
    ci = pl.program_id(1)

    @pl.when(ci == 0)
    def _():
        st_ref[...] = jnp.zeros_like(st_ref)

    tab = lbt_ref[...]
    e = jnp.exp(tab - jnp.max(tab, axis=0, keepdims=True))
    lb = jnp.sum(e[1:layer + 2], axis=0, keepdims=True) / jnp.sum(e, axis=0, keepdims=True)

    hq = hq_ref[...].astype(F32)
    q = hq * _sigmoid(hq)
    f = lb + (1.0 - lb) * _sigmoid(hf_ref[...])
    lf = jnp.log(f)
    kk = 1.0 - f
    v = hi_ref[...]

    t_i = lax.broadcasted_iota(jnp.int32, (c, c), 0)
    s_i = lax.broadcasted_iota(jnp.int32, (c, c), 1)
    tri = jnp.where(t_i >= s_i, 1.0, 0.0).astype(BF16)
    lf_hi = lf.astype(BF16)
    rem = lf - lf_hi.astype(F32)
    lf_mid = rem.astype(BF16)
    lf_lo = (rem - lf_mid.astype(F32)).astype(BF16)
    b = _dot(tri, lf_hi) + _dot(tri, lf_mid) + _dot(tri, lf_lo)
    b_scr[...] = b
    kk_scr[...] = kk

    def brow(r, rows):
        return jnp.broadcast_to(b_scr[pl.ds(r, 1), :], (rows, w))

    levels = []
    blk = c // 2
    while blk >= SUBLANES:
        qs, ks = [], []
        for j in range(c // blk):
            rs = slice(j * blk, (j + 1) * blk)
            if j % 2:
                qs.append(q[rs] * jnp.exp(b[rs] - brow(j * blk - 1, blk)))
                ks.append(jnp.zeros((blk, w), F32))
            else:
                qs.append(jnp.zeros((blk, w), F32))
                ks.append(kk[rs] * jnp.exp(brow((j + 1) * blk - 1, blk) - b[rs]))
        levels.append((blk, jnp.concatenate(qs, axis=0).astype(BF16),
                       jnp.concatenate(ks, axis=0).astype(BF16)))
        blk //= 2

    lane = lax.broadcasted_iota(jnp.int32, (SUBLANES, HG_DIM), 1)
    rowi = lax.broadcasted_iota(jnp.int32, (SUBLANES, HG_DIM), 0)
    for rb in range(c // SUBLANES):
        rs = slice(rb * SUBLANES, (rb + 1) * SUBLANES)
        q_b = q[rs]
        b_b = b[rs]
        diag = [jnp.zeros((SUBLANES, HG_DIM), F32) for _ in range(HG_HEADS)]
        for sl in range(SUBLANES):
            s = rb * SUBLANES + sl
            dec = jnp.exp(jnp.minimum(b_b - b_scr[pl.ds(s, 1), :], 0.0))
            wgt = q_b * dec * kk_scr[pl.ds(s, 1), :]
            for hd in range(HG_HEADS):
                a = jnp.sum(wgt[:, hd * HG_DIM:(hd + 1) * HG_DIM], axis=-1, keepdims=True)
                diag[hd] = jnp.where(lane == s, jnp.where(rowi >= sl, a, 0.0), diag[hd])
        for hd in range(HG_HEADS):
            ad_scr[hd, rs, :] = diag[hd]

    b_last = b_scr[pl.ds(c - 1, 1), :]
    q_dec = (q * jnp.exp(b)).astype(BF16)
    k_last = (kk * jnp.exp(b_last - b)).astype(BF16)
    xor = t_i ^ s_i
    lower = t_i > s_i
    gout = gout_ref[...]
    for hd in range(HG_HEADS):
        hs = slice(hd * HG_DIM, (hd + 1) * HG_DIM)
        attn = ad_scr[hd]
        for blk, qc, kc in levels:
            shift = int(np.log2(blk))
            pc = _dot_nt(qc[:, hs], kc[:, hs])
            attn = jnp.where((xor >> shift) == 1, jnp.where(lower, pc, attn), attn)
        st = st_ref[hd]
        o = _dot_nt(q_dec[:, hs], st.astype(BF16)) + _dot(attn.astype(BF16), v[:, hs])
        hg = hg_ref[:, hs].astype(F32)
        o = o * _rms(o, HG_DIM) * gout * (hg * _sigmoid(hg))
        o_ref[:, hs] = o.astype(o_ref.dtype)
        st_ref[hd] = st * jnp.exp(b_last[:, hs]) + _dot_tn(v[:, hs], k_last[:, hs])


def _hgrn2(hq, hf, hi, hg, lb_table, gout, *, chunk, layer):
    bsz, seq, w = hq.shape
    blk = pl.BlockSpec((None, chunk, w), lambda b, c: (b, c, 0))
    return pl.pallas_call(
        functools.partial(_hgrn_kernel, layer=layer),
        grid=(bsz, seq // chunk),
        in_specs=[blk, blk, blk, blk, _const_spec(lb_table.shape), _const_spec(gout.shape)],
        out_specs=blk,
        out_shape=jax.ShapeDtypeStruct((bsz, seq, w), BF16),
        scratch_shapes=[pltpu.VMEM((HG_HEADS, HG_DIM, HG_DIM), F32),
                        pltpu.VMEM((chunk, w), F32),
                        pltpu.VMEM((chunk, w), F32),
                        pltpu.VMEM((HG_HEADS, chunk, chunk), F32)],
        compiler_params=_params(("arbitrary", "arbitrary")),
        name="hgrn2",
    )(hq, hf, hi, hg, lb_table, gout)


def _merge_kernel(a_ref, o_ref, ga_ref, gb_ref, x_ref, mod_ref, wa_ref, wb_ref, wo_ref, g2_ref,
                  x1_ref, h2_ref):
    d = x_ref.shape[1]
    ya = _dot(a_ref[...], wa_ref[...])
    yb = _dot(o_ref[...], wb_ref[...])
    merged = _sigmoid(ga_ref[...].astype(F32)) * ya + _sigmoid(gb_ref[...].astype(F32)) * yb
    x1 = x_ref[...] + mod_ref[2:3, :] * _dot(merged.astype(BF16), wo_ref[...])
    x1_ref[...] = x1
    h2 = (x1 * _rms(x1, d) * g2_ref[...]) * (1.0 + mod_ref[4:5, :]) + mod_ref[3:4, :]
    h2_ref[...] = h2.astype(BF16)


def _merge_out(attn, hgo, ga, gb, x2, mod, wa, wb, wo, g2, *, seq, tm):
    t, d = x2.shape
    tps = seq // tm
    row = lambda w: pl.BlockSpec((tm, w), lambda i: (i, 0))
    consts = [wa, wb, wo, g2]
    return pl.pallas_call(
        _merge_kernel,
        grid=(t // tm,),
        in_specs=[row(attn.shape[1]), row(hgo.shape[1]), row(d), row(d), row(d),
                  pl.BlockSpec((None, 6, d), lambda i: (i // tps, 0, 0))]
                 + [_const_spec(a.shape) for a in consts],
        out_specs=[row(d), row(d)],
        out_shape=[jax.ShapeDtypeStruct((t, d), F32), jax.ShapeDtypeStruct((t, d), BF16)],
        compiler_params=_params(("arbitrary",)),
        name="merge_out",
    )(attn, hgo, ga, gb, x2, mod, *consts)


def _ffn_kernel(h2_ref, x1_ref, mod_ref, wup_ref, cw_ref, cb_ref, wdn_ref, out_ref, up_scr, *, tps):
    tm = h2_ref.shape[0]
    dff = wdn_ref.shape[0]
    halo = SUBLANES
    i = pl.program_id(0)

    @pl.when(i % tps == 0)
    def _():
        up_scr[0:halo, :] = jnp.zeros((halo, up_scr.shape[1]), F32)

    up = _dot(h2_ref[...], wup_ref[...])
    up_scr[halo:halo + tm, :] = up
    y = cb_ref[...] + cw_ref[2:3, :] * up
    for j in range(CONV_W - 1):
        back = CONV_W - 1 - j
        y = y + cw_ref[j:j + 1, :] * up_scr[halo - back:halo - back + tm, :]
    up_scr[0:halo, :] = up[tm - halo:, :]
    gate = y[:, :dff]
    act = (gate * _sigmoid(gate) * y[:, dff:]).astype(BF16)
    out_ref[...] = x1_ref[...] + mod_ref[5:6, :] * _dot(act, wdn_ref[...])


def _conv_ffn(h2, x1, mod, wup, cw, cb, wdn, *, seq, tm):
    t, d = x1.shape
    tps = seq // tm
    row = lambda w: pl.BlockSpec((tm, w), lambda i: (i, 0))
    consts = [wup, cw, cb, wdn]
    return pl.pallas_call(
        functools.partial(_ffn_kernel, tps=tps),
        grid=(t // tm,),
        in_specs=[row(d), row(d), pl.BlockSpec((None, 6, d), lambda i: (i // tps, 0, 0))]
                 + [_const_spec(a.shape) for a in consts],
        out_specs=row(d),
        out_shape=jax.ShapeDtypeStruct((t, d), F32),
        scratch_shapes=[pltpu.VMEM((tm + SUBLANES, wup.shape[1]), F32)],
        compiler_params=_params(("arbitrary",)),
        name="conv_ffn",
    )(h2, x1, mod, *consts)


def _pad_heads(w, used, n_heads):
    rows = w.shape[0]
    w = w.reshape(rows, n_heads, used)
    w = jnp.pad(w, ((0, 0), (0, 0), (0, HEAD_PAD - used)))
    return w.reshape(rows, n_heads * HEAD_PAD)


def _layer(x2, pos, invf, mod, p, *, bsz, seq, layer):
    t, d = x2.shape
    w_in = p["w_in"]
    o_ckv = Q_LORA
    o_h = o_ckv + KV_LORA + ROPE_DIM
    o_g = o_h + 4 * HG_WIDTH
    wcq = w_in[:, :o_ckv].astype(BF16)
    wckv = jnp.concatenate([w_in[:, o_ckv:o_ckv + KV_LORA],
                            jnp.zeros((d, NOPE_DIM), F32),
                            w_in[:, o_ckv + KV_LORA:o_h],
                            jnp.zeros((d, HEAD_PAD - QK_DIM), F32)], axis=1).astype(BF16)
    wh = w_in[:, o_h:o_g].astype(BF16)
    wg = w_in[:, o_g:].astype(BF16)
    wuq = _pad_heads(p["w_uq"], QK_DIM, MLA_HEADS).astype(BF16)
    wukv = p["w_ukv"].reshape(KV_LORA, MLA_HEADS, NOPE_DIM + V_DIM)
    wk = _pad_heads(wukv[:, :, :NOPE_DIM].reshape(KV_LORA, -1), NOPE_DIM, MLA_HEADS).astype(BF16)
    wv = _pad_heads(wukv[:, :, NOPE_DIM:].reshape(KV_LORA, -1), V_DIM, MLA_HEADS).astype(BF16)
    pad_g = lambda g: jnp.pad(g, (0, HEAD_PAD - QK_DIM)).reshape(1, HEAD_PAD)

    tm = min(256, seq)
    q, k, v, hq, hf, hi, hg, ga, gb = _mixer_in(
        x2, pos.reshape(t // tm, 1, tm), invf, mod, p["norm1_g"].reshape(1, d), wcq, wckv, wh, wg,
        p["q_a_norm_g"].reshape(1, -1), wuq, p["kv_a_norm_g"].reshape(1, -1), wk, wv,
        pad_g(p["q_norm_g"]), pad_g(p["k_norm_g"]), seq=seq, tm=tm)

    hw = MLA_HEADS * HEAD_PAD
    attn = _mla_attention(q.reshape(bsz, seq, hw), k.reshape(bsz, seq, hw), v.reshape(bsz, seq, hw),
                          tq=min(1024, seq), tk=min(512, seq), heads=4)
    r3 = lambda a: a.reshape(bsz, seq, HG_WIDTH)
    hgo = _hgrn2(r3(hq), r3(hf), r3(hi), r3(hg), p["hg_lower_bound"],
                 p["hg_out_norm_g"].reshape(1, HG_DIM), chunk=min(128, seq), layer=layer)

    x1, h2 = _merge_out(attn.reshape(t, -1), hgo.reshape(t, -1), ga, gb, x2, mod,
                        p["w_branch_a"].astype(BF16), p["w_branch_b"].astype(BF16),
                        p["w_out"].astype(BF16), p["norm2_g"].reshape(1, d), seq=seq, tm=tm)
    return _conv_ffn(h2, x1, mod, p["w_up"].astype(BF16), p["conv_w"], p["conv_b"].reshape(1, -1),
                     p["w_down"].astype(BF16), seq=seq, tm=tm)


def kernel(x, c, positions, w_ada, b_ada, norm1_g, w_in, q_a_norm_g, w_uq, kv_a_norm_g, w_ukv,
           q_norm_g, k_norm_g, hg_lower_bound, hg_out_norm_g, w_branch_a, w_branch_b, w_out,
           norm2_g, w_up, conv_w, conv_b, w_down):
    bsz, seq, d = x.shape
    depth = w_ada.shape[0]
    invf = (ROPE_THETA ** (-jnp.arange(0, ROPE_DIM, 2, dtype=F32) / ROPE_DIM)).reshape(HALF_ROPE, 1)
    x2 = x.reshape(bsz * seq, d)
    pos = positions.reshape(bsz * seq)
    for l in range(depth):
        mod = _ada_mod(c, w_ada[l], b_ada[l]).reshape(bsz, 6, d)
        p = dict(norm1_g=norm1_g[l], w_in=w_in[l], q_a_norm_g=q_a_norm_g[l], w_uq=w_uq[l],
                 kv_a_norm_g=kv_a_norm_g[l], w_ukv=w_ukv[l], q_norm_g=q_norm_g[l],
                 k_norm_g=k_norm_g[l], hg_lower_bound=hg_lower_bound, hg_out_norm_g=hg_out_norm_g[l],
                 w_branch_a=w_branch_a[l], w_branch_b=w_branch_b[l], w_out=w_out[l],
                 norm2_g=norm2_g[l], w_up=w_up[l], conv_w=conv_w[l], conv_b=conv_b[l],
                 w_down=w_down[l])
        x2 = _layer(x2, pos, invf, mod, p, bsz=bsz, seq=seq, layer=l)
    return x2.reshape(bsz, seq, d)
```

```python
import functools

import numpy as np
import jax
import jax.numpy as jnp
from jax import lax
from jax.experimental import pallas as pl
from jax.experimental.pallas import tpu as pltpu

F32 = jnp.float32
BF16 = jnp.bfloat16

LANES = 128
SUBLANES = 8
VMEM_LIMIT = 56 * 1024 * 1024

MLA_HEADS = 8
NOPE_DIM = 64
ROPE_DIM = 32
HALF_ROPE = ROPE_DIM // 2
QK_DIM = NOPE_DIM + ROPE_DIM
V_DIM = 64
Q_LORA = 512
KV_LORA = 256
ROPE_THETA = 10000.0
HG_HEADS = 4
HG_DIM = 128
HG_WIDTH = HG_HEADS * HG_DIM
CONV_W = 3
EPS = 1e-6
HEAD_PAD = LANES
LOG2E = float(np.log2(np.e))
NEG_BIG = -1e30


def _dot(a, b):
    return jnp.dot(a, b, preferred_element_type=F32)


def _dot_nt(a, b):
    return lax.dot_general(a, b, (((1,), (1,)), ((), ())), preferred_element_type=F32)


def _dot_tn(a, b):
    return lax.dot_general(a, b, (((0,), (0,)), ((), ())), preferred_element_type=F32)


def _sigmoid(x):
    return 0.5 * jnp.tanh(0.5 * x) + 0.5


def _rms(x, width):
    return lax.rsqrt(jnp.sum(x * x, axis=-1, keepdims=True) * (1.0 / width) + EPS)


def _const_spec(shape):
    nd = len(shape)
    return pl.BlockSpec(shape, lambda *_: (0,) * nd, pipeline_mode=pl.Buffered(1))


def _params(sem):
    return pltpu.CompilerParams(dimension_semantics=sem, vmem_limit_bytes=VMEM_LIMIT)


def _ada_kernel(c_ref, w_ref, b_ref, o_ref):
    c = c_ref[...]
    w = w_ref[...]
    c_hi = c.astype(BF16)
    c_lo = (c - c_hi.astype(F32)).astype(BF16)
    w_hi = w.astype(BF16)
    w_lo = (w - w_hi.astype(F32)).astype(BF16)
    o_ref[...] = _dot(c_hi, w_hi) + _dot(c_hi, w_lo) + _dot(c_lo, w_hi) + b_ref[...]


def _ada_mod(c, w_ada, b_ada):
    bsz, d = c.shape
    n = w_ada.shape[1]
    bn = 1024
    return pl.pallas_call(
        _ada_kernel,
        grid=(n // bn,),
        in_specs=[pl.BlockSpec((bsz, d), lambda j: (0, 0)),
                  pl.BlockSpec((d, bn), lambda j: (0, j)),
                  pl.BlockSpec((1, bn), lambda j: (0, j))],
        out_specs=pl.BlockSpec((bsz, bn), lambda j: (0, j)),
        out_shape=jax.ShapeDtypeStruct((bsz, n), F32),
        compiler_params=_params(("arbitrary",)),
        name="ada_mod",
    )(c, w_ada, b_ada.reshape(1, n))


def _rope(t, cos_t, sin_lo, sin_hi):
    return (t * cos_t + pltpu.roll(t, LANES - HALF_ROPE, 1) * sin_lo
            + pltpu.roll(t, HALF_ROPE, 1) * sin_hi)


def _mixer_in_kernel(x_ref, pos_ref, invf_ref, mod_ref, g1_ref, wcq_ref, wckv_ref, wh_ref, wg_ref,
                     gqa_ref, wuq_ref, gkva_ref, wk_ref, wv_ref, gq_ref, gk_ref,
                     q_ref, k_ref, v_ref, hq_ref, hf_ref, hi_ref, hg_ref, ga_ref, gb_ref):
    tm, d = x_ref.shape
    x = x_ref[...]
    h = (x * _rms(x, d) * g1_ref[...]) * (1.0 + mod_ref[1:2, :]) + mod_ref[0:1, :]
    hb = h.astype(BF16)

    ang = invf_ref[...] * pos_ref[...].astype(F32)
    cos_h = jnp.cos(ang)
    sin_h = jnp.sin(ang)
    zeros = lambda r: jnp.zeros((r, tm), F32)
    cos_t = jnp.concatenate([jnp.ones((NOPE_DIM, tm), F32), cos_h, cos_h,
                             zeros(HEAD_PAD - QK_DIM)], axis=0).T
    sin_lo = jnp.concatenate([zeros(NOPE_DIM), -sin_h, zeros(HEAD_PAD - NOPE_DIM - HALF_ROPE)],
                             axis=0).T
    sin_hi = jnp.concatenate([zeros(NOPE_DIM + HALF_ROPE), sin_h, zeros(HEAD_PAD - QK_DIM)],
                             axis=0).T

    cq = _dot(hb, wcq_ref[...])
    ckv = _dot(hb, wckv_ref[...])
    hh = _dot(hb, wh_ref[...])
    hq_ref[...] = hh[:, 0 * HG_WIDTH:1 * HG_WIDTH].astype(BF16)
    hf_ref[...] = hh[:, 1 * HG_WIDTH:2 * HG_WIDTH]
    hi_ref[...] = hh[:, 2 * HG_WIDTH:3 * HG_WIDTH].astype(BF16)
    hg_ref[...] = hh[:, 3 * HG_WIDTH:4 * HG_WIDTH].astype(BF16)

    cqn = (cq * _rms(cq, Q_LORA) * gqa_ref[...]).astype(BF16)
    ckv_c = ckv[:, :KV_LORA]
    kr = ckv[:, KV_LORA:]
    ckvn = (ckv_c * _rms(ckv_c, KV_LORA) * gkva_ref[...]).astype(BF16)
    qf = _dot(cqn, wuq_ref[...])
    kf = _dot(ckvn, wk_ref[...])
    vf = _dot(ckvn, wv_ref[...])
    gg = _dot(hb, wg_ref[...])
    ga_ref[...] = gg[:, :d].astype(BF16)
    gb_ref[...] = gg[:, d:].astype(BF16)

    gq = gq_ref[...]
    for hd in range(MLA_HEADS):
        sl = slice(hd * HEAD_PAD, (hd + 1) * HEAD_PAD)
        qh = qf[:, sl]
        qn = qh * _rms(qh, QK_DIM) * gq
        q_ref[:, sl] = (_rope(qn, cos_t, sin_lo, sin_hi) * (LOG2E / float(np.sqrt(QK_DIM)))).astype(BF16)

    gk = gk_ref[...]
    ss_rope = jnp.sum(kr * kr, axis=-1, keepdims=True)
    kr_rot = _rope(kr * gk, cos_t, sin_lo, sin_hi)
    for hd in range(MLA_HEADS):
        sl = slice(hd * HEAD_PAD, (hd + 1) * HEAD_PAD)
        kn = kf[:, sl]
        ss = jnp.sum(kn * kn, axis=-1, keepdims=True) + ss_rope
        k_ref[:, sl] = ((kn * gk + kr_rot) * lax.rsqrt(ss * (1.0 / QK_DIM) + EPS)).astype(BF16)
    lane = lax.broadcasted_iota(jnp.int32, (1, MLA_HEADS * HEAD_PAD), 1)
    ones_col = jnp.where((lane & (HEAD_PAD - 1)) == V_DIM, 1.0, 0.0)
    v_ref[...] = (vf + ones_col).astype(BF16)


def _mixer_in(x2, pos, invf, mod, g1, wcq, wckv, wh, wg, gqa, wuq, gkva, wk, wv, gq, gk, *, seq, tm):
    t, d = x2.shape
    tps = seq // tm
    hw = MLA_HEADS * HEAD_PAD
    row = lambda w: pl.BlockSpec((tm, w), lambda i: (i, 0))
    consts = [g1, wcq, wckv, wh, wg, gqa, wuq, gkva, wk, wv, gq, gk]
    out_shapes = [jax.ShapeDtypeStruct((t, hw), BF16)] * 3 + [
        jax.ShapeDtypeStruct((t, HG_WIDTH), BF16), jax.ShapeDtypeStruct((t, HG_WIDTH), F32),
        jax.ShapeDtypeStruct((t, HG_WIDTH), BF16), jax.ShapeDtypeStruct((t, HG_WIDTH), BF16),
        jax.ShapeDtypeStruct((t, d), BF16), jax.ShapeDtypeStruct((t, d), BF16)]
    return pl.pallas_call(
        _mixer_in_kernel,
        grid=(t // tm,),
        in_specs=[row(d),
                  pl.BlockSpec((None, 1, tm), lambda i: (i, 0, 0)),
                  _const_spec(invf.shape),
                  pl.BlockSpec((None, 6, d), lambda i: (i // tps, 0, 0))]
                 + [_const_spec(a.shape) for a in consts],
        out_specs=[row(hw)] * 3 + [row(HG_WIDTH)] * 4 + [row(d)] * 2,
        out_shape=out_shapes,
        compiler_params=_params(("arbitrary",)),
        name="mixer_in",
    )(x2, pos, invf, mod, *consts)


def _attn_kernel(q_ref, k_ref, v_ref, o_ref, m_scr, acc_scr, *, tq, tk, heads):
    i = pl.program_id(2)
    nsub = tq // tk
    head_slices = [slice(hd * HEAD_PAD, (hd + 1) * HEAD_PAD) for hd in range(heads)]
    m_scr[...] = jnp.full(m_scr.shape, NEG_BIG, F32)
    acc_scr[...] = jnp.zeros(acc_scr.shape, F32)

    def block(j, row0, diagonal):
        start = pl.multiple_of(j * tk, tk)
        rows = slice(row0, tq)
        if diagonal:
            row = lax.broadcasted_iota(jnp.int32, (tq - row0, tk), 0)
            col = lax.broadcasted_iota(jnp.int32, (tq - row0, tk), 1)

        def scores(hd):
            sl = head_slices[hd]
            s = _dot_nt(q_ref[rows, sl], k_ref[pl.ds(start, tk), sl])
            return jnp.where(col <= row, s, NEG_BIG) if diagonal else s

        def update(hd, s):
            m_prev = m_scr[hd, rows, :]
            m_new = jnp.maximum(m_prev, jnp.max(s, axis=-1, keepdims=True))
            p = jnp.exp2(s - jnp.concatenate([m_new] * (tk // LANES), axis=1)).astype(BF16)
            acc_scr[hd, rows, :] = (acc_scr[hd, rows, :] * jnp.exp2(m_prev - m_new)
                                    + _dot(p, v_ref[pl.ds(start, tk), head_slices[hd]]))
            m_scr[hd, rows, :] = m_new

        s_next = scores(0)
        for hd in range(heads):
            s_cur = s_next
            if hd + 1 < heads:
                s_next = scores(hd + 1)
            update(hd, s_cur)

    def body(j, carry):
        block(j, 0, False)
        return carry

    lax.fori_loop(0, i * nsub, body, 0)
    for r in range(nsub):
        block(i * nsub + r, r * tk, True)
    outs = []
    for hd in range(heads):
        acc = acc_scr[hd]
        outs.append(acc[:, :V_DIM] / acc[:, V_DIM:V_DIM + 1])
    o_ref[...] = jnp.concatenate(outs, axis=-1).astype(o_ref.dtype)


def _mla_attention(q, k, v, *, tq, tk, heads):
    bsz, seq, hw = q.shape
    groups = MLA_HEADS // heads
    pw = heads * HEAD_PAD
    kv_spec = pl.BlockSpec((None, seq, pw), lambda b, h, i: (b, 0, h))
    return pl.pallas_call(
        functools.partial(_attn_kernel, tq=tq, tk=tk, heads=heads),
        grid=(bsz, groups, seq // tq),
        in_specs=[pl.BlockSpec((None, tq, pw), lambda b, h, i: (b, i, h)), kv_spec, kv_spec],
        out_specs=pl.BlockSpec((None, tq, heads * V_DIM), lambda b, h, i: (b, i, h)),
        out_shape=jax.ShapeDtypeStruct((bsz, seq, MLA_HEADS * V_DIM), BF16),
        scratch_shapes=[pltpu.VMEM((heads, tq, LANES), F32), pltpu.VMEM((heads, tq, HEAD_PAD), F32)],
        compiler_params=_params(("arbitrary", "arbitrary", "arbitrary")),
        name="mla_attn",
    )(q, k, v)


def _hgrn_sum_masks(c):
    t = np.arange(c)[:, None]
    s = np.arange(c)[None, :]
    blocks = [s <= t, s > t]
    blk = c // 2
    while blk >= 2:
        bnd = (t & ~(2 * blk - 1)) + blk - 1
        blocks.append(((s > bnd) & (s <= t)) | ((s > t) & (s <= bnd)))
        blk //= 2
    em = np.concatenate(blocks, axis=0).astype(np.float32)
    return np.concatenate([em, em], axis=1)


def _hgrn_kernel(hq_ref, hf_ref, hi_ref, hg_ref, lbt_ref, gout_ref, em_ref, o_ref, st_ref, *, layer):
    c, w = hq_ref.shape
    ci = pl.program_id(1)

    @pl.when(ci == 0)
    def _():
        st_ref[...] = jnp.zeros_like(st_ref)

    tab = lbt_ref[...]
    e = jnp.exp(tab - jnp.max(tab, axis=0, keepdims=True))
    lb = jnp.sum(e[1:layer + 2], axis=0, keepdims=True) / jnp.sum(e, axis=0, keepdims=True)

    hq = hq_ref[...].astype(F32)
    q = hq * _sigmoid(hq)
    f = lb + (1.0 - lb) * _sigmoid(hf_ref[...])
    lf = jnp.log(f)
    kk = 1.0 - f
    v = hi_ref[...]

    lf_hi = lf.astype(BF16)
    lf_mid = (lf - lf_hi.astype(F32)).astype(BF16)
    sums = _dot(em_ref[...], jnp.concatenate([lf_hi, lf_mid], axis=0))
    b = sums[0:c]
    q_dec = (q * jnp.exp(b)).astype(BF16)
    k_last = (kk * jnp.exp(sums[c:2 * c])).astype(BF16)
    b_last = b[c - 1:c, :]

    n_levels = int(np.log2(c))
    sub = lax.broadcasted_iota(jnp.int32, (SUBLANES, w), 0)
    xs = []
    for lvl in range(n_levels):
        blk = 1 << lvl
        if blk >= SUBLANES:
            base = jnp.concatenate([(q if (j % 2) else kk)[j * blk:(j + 1) * blk]
                                    for j in range(c // blk)], axis=0)
        else:
            odd = jnp.concatenate([(sub & blk) != 0] * (c // SUBLANES), axis=0)
            base = jnp.where(odd, q * f, kk) if blk == 1 else jnp.where(odd, q, kk)
        if blk > 1:
            row0 = (2 + n_levels - 1 - lvl) * c
            base = base * jnp.exp(sums[row0:row0 + c])
        xs.append(base.astype(BF16))

    t_i = lax.broadcasted_iota(jnp.int32, (c, c), 0)
    s_i = lax.broadcasted_iota(jnp.int32, (c, c), 1)
    lower_xor = jnp.where(t_i > s_i, t_i ^ s_i, 0)
    level_of = [(lower_xor >> lvl) == 1 for lvl in range(n_levels)]
    on_diag = t_i == s_i
    qk = q * kk
    gout = gout_ref[...]
    head_slices = [slice(hd * HG_DIM, (hd + 1) * HG_DIM) for hd in range(HG_HEADS)]
    attns = [jnp.where(on_diag, jnp.sum(qk[:, hs], axis=-1, keepdims=True), 0.0)
             for hs in head_slices]
    for x, keep in zip(xs, level_of):
        attns = [jnp.where(keep, _dot_nt(x[:, hs], x[:, hs]), a) for hs, a in zip(head_slices, attns)]
    for hd, hs in enumerate(head_slices):
        attn = attns[hd]
        st = st_ref[hd]
        o = _dot_nt(q_dec[:, hs], st.astype(BF16)) + _dot(attn.astype(BF16), v[:, hs])
        hg = hg_ref[:, hs].astype(F32)
        o = o * _rms(o, HG_DIM) * gout * (hg * _sigmoid(hg))
        o_ref[:, hs] = o.astype(o_ref.dtype)
        st_ref[hd] = st * jnp.exp(b_last[:, hs]) + _dot_tn(v[:, hs], k_last[:, hs])


def _hgrn2(hq, hf, hi, hg, lb_table, gout, *, chunk, layer):
    bsz, seq, w = hq.shape
    blk = pl.BlockSpec((None, chunk, w), lambda b, c: (b, c, 0))
    em = jnp.asarray(_hgrn_sum_masks(chunk), dtype=BF16)
    return pl.pallas_call(
        functools.partial(_hgrn_kernel, layer=layer),
        grid=(bsz, seq // chunk),
        in_specs=[blk, blk, blk, blk, _const_spec(lb_table.shape), _const_spec(gout.shape),
                  _const_spec(em.shape)],
        out_specs=blk,
        out_shape=jax.ShapeDtypeStruct((bsz, seq, w), BF16),
        scratch_shapes=[pltpu.VMEM((HG_HEADS, HG_DIM, HG_DIM), F32)],
        compiler_params=_params(("arbitrary", "arbitrary")),
        name="hgrn2",
    )(hq, hf, hi, hg, lb_table, gout, em)


def _merge_kernel(a_ref, o_ref, ga_ref, gb_ref, x_ref, mod_ref, wa_ref, wb_ref, wo_ref, g2_ref,
                  x1_ref, h2_ref):
    d = x_ref.shape[1]
    ya = _dot(a_ref[...], wa_ref[...])
    yb = _dot(o_ref[...], wb_ref[...])
    merged = _sigmoid(ga_ref[...].astype(F32)) * ya + _sigmoid(gb_ref[...].astype(F32)) * yb
    x1 = x_ref[...] + mod_ref[2:3, :] * _dot(merged.astype(BF16), wo_ref[...])
    x1_ref[...] = x1
    h2 = (x1 * _rms(x1, d) * g2_ref[...]) * (1.0 + mod_ref[4:5, :]) + mod_ref[3:4, :]
    h2_ref[...] = h2.astype(BF16)


def _merge_out(attn, hgo, ga, gb, x2, mod, wa, wb, wo, g2, *, seq, tm):
    t, d = x2.shape
    tps = seq // tm
    row = lambda w: pl.BlockSpec((tm, w), lambda i: (i, 0))
    consts = [wa, wb, wo, g2]
    return pl.pallas_call(
        _merge_kernel,
        grid=(t // tm,),
        in_specs=[row(attn.shape[1]), row(hgo.shape[1]), row(d), row(d), row(d),
                  pl.BlockSpec((None, 6, d), lambda i: (i // tps, 0, 0))]
                 + [_const_spec(a.shape) for a in consts],
        out_specs=[row(d), row(d)],
        out_shape=[jax.ShapeDtypeStruct((t, d), F32), jax.ShapeDtypeStruct((t, d), BF16)],
        compiler_params=_params(("arbitrary",)),
        name="merge_out",
    )(attn, hgo, ga, gb, x2, mod, *consts)


def _ffn_kernel(h2_ref, x1_ref, mod_ref, wup_ref, cw_ref, cb_ref, wdn_ref, out_ref, up_scr, *, tps):
    tm = h2_ref.shape[0]
    dff = wdn_ref.shape[0]
    halo = SUBLANES
    i = pl.program_id(0)

    @pl.when(i % tps == 0)
    def _():
        up_scr[0:halo, :] = jnp.zeros((halo, up_scr.shape[1]), F32)

    up = _dot(h2_ref[...], wup_ref[...])
    up_scr[halo:halo + tm, :] = up
    y = cb_ref[...] + cw_ref[CONV_W - 1:CONV_W, :] * up
    for j in range(CONV_W - 1):
        back = CONV_W - 1 - j
        y = y + cw_ref[j:j + 1, :] * up_scr[halo - back:halo - back + tm, :]
    up_scr[0:halo, :] = up[tm - halo:, :]
    gate = y[:, :dff]
    act = (gate * _sigmoid(gate) * y[:, dff:]).astype(BF16)
    out_ref[...] = x1_ref[...] + mod_ref[5:6, :] * _dot(act, wdn_ref[...])


def _conv_ffn(h2, x1, mod, wup, cw, cb, wdn, *, seq, tm):
    t, d = x1.shape
    tps = seq // tm
    row = lambda w: pl.BlockSpec((tm, w), lambda i: (i, 0))
    consts = [wup, cw, cb, wdn]
    return pl.pallas_call(
        functools.partial(_ffn_kernel, tps=tps),
        grid=(t // tm,),
        in_specs=[row(d), row(d), pl.BlockSpec((None, 6, d), lambda i: (i // tps, 0, 0))]
                 + [_const_spec(a.shape) for a in consts],
        out_specs=row(d),
        out_shape=jax.ShapeDtypeStruct((t, d), F32),
        scratch_shapes=[pltpu.VMEM((tm + SUBLANES, wup.shape[1]), F32)],
        compiler_params=_params(("arbitrary",)),
        name="conv_ffn",
    )(h2, x1, mod, *consts)


def _pad_heads(w, used, n_heads):
    rows = w.shape[0]
    w = w.reshape(rows, n_heads, used)
    w = jnp.pad(w, ((0, 0), (0, 0), (0, HEAD_PAD - used)))
    return w.reshape(rows, n_heads * HEAD_PAD)


def _layer(x2, pos, invf, mod, p, *, bsz, seq, layer):
    t, d = x2.shape
    w_in = p["w_in"]
    o_ckv = Q_LORA
    o_h = o_ckv + KV_LORA + ROPE_DIM
    o_g = o_h + 4 * HG_WIDTH
    wcq = w_in[:, :o_ckv].astype(BF16)
    wckv = jnp.concatenate([w_in[:, o_ckv:o_ckv + KV_LORA],
                            jnp.zeros((d, NOPE_DIM), F32),
                            w_in[:, o_ckv + KV_LORA:o_h],
                            jnp.zeros((d, HEAD_PAD - QK_DIM), F32)], axis=1).astype(BF16)
    wh = w_in[:, o_h:o_g].astype(BF16)
    wg = w_in[:, o_g:].astype(BF16)
    wuq = _pad_heads(p["w_uq"], QK_DIM, MLA_HEADS).astype(BF16)
    wukv = p["w_ukv"].reshape(KV_LORA, MLA_HEADS, NOPE_DIM + V_DIM)
    wk = _pad_heads(wukv[:, :, :NOPE_DIM].reshape(KV_LORA, -1), NOPE_DIM, MLA_HEADS).astype(BF16)
    wv = _pad_heads(wukv[:, :, NOPE_DIM:].reshape(KV_LORA, -1), V_DIM, MLA_HEADS).astype(BF16)
    pad_g = lambda g: jnp.pad(g, (0, HEAD_PAD - QK_DIM)).reshape(1, HEAD_PAD)

    tm = min(256, seq)
    q, k, v, hq, hf, hi, hg, ga, gb = _mixer_in(
        x2, pos.reshape(t // tm, 1, tm), invf, mod, p["norm1_g"].reshape(1, d), wcq, wckv, wh, wg,
        p["q_a_norm_g"].reshape(1, -1), wuq, p["kv_a_norm_g"].reshape(1, -1), wk, wv,
        pad_g(p["q_norm_g"]), pad_g(p["k_norm_g"]), seq=seq, tm=tm)

    hw = MLA_HEADS * HEAD_PAD
    attn = _mla_attention(q.reshape(bsz, seq, hw), k.reshape(bsz, seq, hw), v.reshape(bsz, seq, hw),
                          tq=min(1024, seq), tk=min(512, seq), heads=4)
    r3 = lambda a: a.reshape(bsz, seq, HG_WIDTH)
    hgo = _hgrn2(r3(hq), r3(hf), r3(hi), r3(hg), p["hg_lower_bound"],
                 p["hg_out_norm_g"].reshape(1, HG_DIM), chunk=min(128, seq), layer=layer)

    x1, h2 = _merge_out(attn.reshape(t, -1), hgo.reshape(t, -1), ga, gb, x2, mod,
                        p["w_branch_a"].astype(BF16), p["w_branch_b"].astype(BF16),
                        p["w_out"].astype(BF16), p["norm2_g"].reshape(1, d), seq=seq, tm=tm)
    return _conv_ffn(h2, x1, mod, p["w_up"].astype(BF16), p["conv_w"], p["conv_b"].reshape(1, -1),
                     p["w_down"].astype(BF16), seq=seq, tm=tm)


def kernel(x, c, positions, w_ada, b_ada, norm1_g, w_in, q_a_norm_g, w_uq, kv_a_norm_g, w_ukv,
           q_norm_g, k_norm_g, hg_lower_bound, hg_out_norm_g, w_branch_a, w_branch_b, w_out,
           norm2_g, w_up, conv_w, conv_b, w_down):
    bsz, seq, d = x.shape
    depth = w_ada.shape[0]
    invf = (ROPE_THETA ** (-jnp.arange(0, ROPE_DIM, 2, dtype=F32) / ROPE_DIM)).reshape(HALF_ROPE, 1)
    x2 = x.reshape(bsz * seq, d)
    pos = positions.reshape(bsz * seq)
    for l in range(depth):
        mod = _ada_mod(c, w_ada[l], b_ada[l]).reshape(bsz, 6, d)
        p = dict(norm1_g=norm1_g[l], w_in=w_in[l], q_a_norm_g=q_a_norm_g[l], w_uq=w_uq[l],
                 kv_a_norm_g=kv_a_norm_g[l], w_ukv=w_ukv[l], q_norm_g=q_norm_g[l],
                 k_norm_g=k_norm_g[l], hg_lower_bound=hg_lower_bound, hg_out_norm_g=hg_out_norm_g[l],
                 w_branch_a=w_branch_a[l], w_branch_b=w_branch_b[l], w_out=w_out[l],
                 norm2_g=norm2_g[l], w_up=w_up[l], conv_w=conv_w[l], conv_b=conv_b[l],
                 w_down=w_down[l])
        x2 = _layer(x2, pos, invf, mod, p, bsz=bsz, seq=seq, layer=l)
    return x2.reshape(bsz, seq, d)
```

```python
import functools

import numpy as np
import jax
import jax.numpy as jnp
from jax import lax
from jax.experimental import pallas as pl
from jax.experimental.pallas import tpu as pltpu

F32 = jnp.float32
BF16 = jnp.bfloat16

LANES = 128
SUBLANES = 8
VMEM_LIMIT = 56 * 1024 * 1024

MLA_HEADS = 8
NOPE_DIM = 64
ROPE_DIM = 32
HALF_ROPE = ROPE_DIM // 2
QK_DIM = NOPE_DIM + ROPE_DIM
V_DIM = 64
Q_LORA = 512
KV_LORA = 256
ROPE_THETA = 10000.0
HG_HEADS = 4
HG_DIM = 128
HG_WIDTH = HG_HEADS * HG_DIM
CONV_W = 3
EPS = 1e-6
HEAD_PAD = LANES
LOG2E = float(np.log2(np.e))
NEG_BIG = -1e30


def _dot(a, b):
    return jnp.dot(a, b, preferred_element_type=F32)


def _dot_nt(a, b):
    return lax.dot_general(a, b, (((1,), (1,)), ((), ())), preferred_element_type=F32)


def _dot_tn(a, b):
    return lax.dot_general(a, b, (((0,), (0,)), ((), ())), preferred_element_type=F32)


def _sigmoid(x):
    return 0.5 * jnp.tanh(0.5 * x) + 0.5


def _rms(x, width):
    return lax.rsqrt(jnp.sum(x * x, axis=-1, keepdims=True) * (1.0 / width) + EPS)


def _const_spec(shape):
    nd = len(shape)
    return pl.BlockSpec(shape, lambda *_: (0,) * nd, pipeline_mode=pl.Buffered(1))


def _params(sem):
    return pltpu.CompilerParams(dimension_semantics=sem, vmem_limit_bytes=VMEM_LIMIT)


def _ada_kernel(c_ref, w_ref, b_ref, o_ref):
    c = c_ref[...]
    w = w_ref[...]
    c_hi = c.astype(BF16)
    c_lo = (c - c_hi.astype(F32)).astype(BF16)
    w_hi = w.astype(BF16)
    w_lo = (w - w_hi.astype(F32)).astype(BF16)
    o_ref[...] = _dot(c_hi, w_hi) + _dot(c_hi, w_lo) + _dot(c_lo, w_hi) + b_ref[...]


def _ada_mod(c, w_ada, b_ada, layer):
    bsz, d = c.shape
    depth, _, n = w_ada.shape
    bn = 1024
    return pl.pallas_call(
        _ada_kernel,
        grid=(n // bn,),
        in_specs=[pl.BlockSpec((bsz, d), lambda j: (0, 0)),
                  pl.BlockSpec((None, d, bn), lambda j: (layer, 0, j)),
                  pl.BlockSpec((None, 1, bn), lambda j: (layer, 0, j))],
        out_specs=pl.BlockSpec((bsz, bn), lambda j: (0, j)),
        out_shape=jax.ShapeDtypeStruct((bsz, n), F32),
        compiler_params=_params(("arbitrary",)),
        name="ada_mod",
    )(c, w_ada, b_ada.reshape(depth, 1, n))


def _rope(t, cos_t, sin_lo, sin_hi):
    return (t * cos_t + pltpu.roll(t, LANES - HALF_ROPE, 1) * sin_lo
            + pltpu.roll(t, HALF_ROPE, 1) * sin_hi)


def _mixer_in_kernel(x_ref, pos_ref, invf_ref, mod_ref, g1_ref, wcq_ref, wckv_ref, wh_ref, wg_ref,
                     gqa_ref, wuq_ref, gkva_ref, wk_ref, wv_ref, gq_ref, gk_ref,
                     q_ref, k_ref, v_ref, hq_ref, hf_ref, hi_ref, hg_ref, ga_ref, gb_ref):
    tm, d = x_ref.shape
    x = x_ref[...]
    h = (x * _rms(x, d) * g1_ref[...]) * (1.0 + mod_ref[1:2, :]) + mod_ref[0:1, :]
    hb = h.astype(BF16)

    ang = invf_ref[...] * pos_ref[...].astype(F32)
    cos_h = jnp.cos(ang)
    sin_h = jnp.sin(ang)
    zeros = lambda r: jnp.zeros((r, tm), F32)
    cos_t = jnp.concatenate([jnp.ones((NOPE_DIM, tm), F32), cos_h, cos_h,
                             zeros(HEAD_PAD - QK_DIM)], axis=0).T
    sin_lo = jnp.concatenate([zeros(NOPE_DIM), -sin_h, zeros(HEAD_PAD - NOPE_DIM - HALF_ROPE)],
                             axis=0).T
    sin_hi = jnp.concatenate([zeros(NOPE_DIM + HALF_ROPE), sin_h, zeros(HEAD_PAD - QK_DIM)],
                             axis=0).T

    cq = _dot(hb, wcq_ref[...])
    ckv = _dot(hb, wckv_ref[...])
    hh = _dot(hb, wh_ref[...])
    hq_ref[...] = hh[:, 0 * HG_WIDTH:1 * HG_WIDTH].astype(BF16)
    hf_ref[...] = hh[:, 1 * HG_WIDTH:2 * HG_WIDTH]
    hi_ref[...] = hh[:, 2 * HG_WIDTH:3 * HG_WIDTH].astype(BF16)
    hg_ref[...] = hh[:, 3 * HG_WIDTH:4 * HG_WIDTH].astype(BF16)

    cqn = (cq * _rms(cq, Q_LORA) * gqa_ref[...]).astype(BF16)
    ckv_c = ckv[:, :KV_LORA]
    kr = ckv[:, KV_LORA:]
    ckvn = (ckv_c * _rms(ckv_c, KV_LORA) * gkva_ref[...]).astype(BF16)
    qf = _dot(cqn, wuq_ref[...])
    kf = _dot(ckvn, wk_ref[...])
    vf = _dot(ckvn, wv_ref[...])
    gg = _dot(hb, wg_ref[...])
    ga_ref[...] = gg[:, :d].astype(BF16)
    gb_ref[...] = gg[:, d:].astype(BF16)

    gq = gq_ref[...]
    for hd in range(MLA_HEADS):
        sl = slice(hd * HEAD_PAD, (hd + 1) * HEAD_PAD)
        qh = qf[:, sl]
        qn = qh * _rms(qh, QK_DIM) * gq
        q_ref[:, sl] = (_rope(qn, cos_t, sin_lo, sin_hi) * (LOG2E / float(np.sqrt(QK_DIM)))).astype(BF16)

    gk = gk_ref[...]
    ss_rope = jnp.sum(kr * kr, axis=-1, keepdims=True)
    kr_rot = _rope(kr * gk, cos_t, sin_lo, sin_hi)
    for hd in range(MLA_HEADS):
        sl = slice(hd * HEAD_PAD, (hd + 1) * HEAD_PAD)
        kn = kf[:, sl]
        ss = jnp.sum(kn * kn, axis=-1, keepdims=True) + ss_rope
        k_ref[:, sl] = ((kn * gk + kr_rot) * lax.rsqrt(ss * (1.0 / QK_DIM) + EPS)).astype(BF16)
    lane = lax.broadcasted_iota(jnp.int32, (1, MLA_HEADS * HEAD_PAD), 1)
    ones_col = jnp.where((lane & (HEAD_PAD - 1)) == V_DIM, 1.0, 0.0)
    v_ref[...] = (vf + ones_col).astype(BF16)


def _mixer_in(x2, pos, invf, mod, g1, wcq, wckv, wh, wg, gqa, wuq, gkva, wk, wv, gq, gk, *, seq, tm):
    t, d = x2.shape
    tps = seq // tm
    hw = MLA_HEADS * HEAD_PAD
    row = lambda w: pl.BlockSpec((tm, w), lambda i: (i, 0))
    consts = [g1, wcq, wckv, wh, wg, gqa, wuq, gkva, wk, wv, gq, gk]
    out_shapes = [jax.ShapeDtypeStruct((t, hw), BF16)] * 3 + [
        jax.ShapeDtypeStruct((t, HG_WIDTH), BF16), jax.ShapeDtypeStruct((t, HG_WIDTH), F32),
        jax.ShapeDtypeStruct((t, HG_WIDTH), BF16), jax.ShapeDtypeStruct((t, HG_WIDTH), BF16),
        jax.ShapeDtypeStruct((t, d), BF16), jax.ShapeDtypeStruct((t, d), BF16)]
    return pl.pallas_call(
        _mixer_in_kernel,
        grid=(t // tm,),
        in_specs=[row(d),
                  pl.BlockSpec((None, 1, tm), lambda i: (i, 0, 0)),
                  _const_spec(invf.shape),
                  pl.BlockSpec((None, 6, d), lambda i: (i // tps, 0, 0))]
                 + [_const_spec(a.shape) for a in consts],
        out_specs=[row(hw)] * 3 + [row(HG_WIDTH)] * 4 + [row(d)] * 2,
        out_shape=out_shapes,
        compiler_params=_params(("arbitrary",)),
        name="mixer_in",
    )(x2, pos, invf, mod, *consts)


def _attn_kernel(q_ref, k_ref, v_ref, o_ref, m_scr, acc_scr, *, tq, tk, heads):
    i = pl.program_id(2)
    nsub = tq // tk
    head_slices = [slice(hd * HEAD_PAD, (hd + 1) * HEAD_PAD) for hd in range(heads)]
    m_scr[...] = jnp.full(m_scr.shape, NEG_BIG, F32)
    acc_scr[...] = jnp.zeros(acc_scr.shape, F32)

    def block(j, row0, diagonal):
        start = pl.multiple_of(j * tk, tk)
        rows = slice(row0, tq)
        if diagonal:
            row = lax.broadcasted_iota(jnp.int32, (tq - row0, tk), 0)
            col = lax.broadcasted_iota(jnp.int32, (tq - row0, tk), 1)

        def scores(hd):
            sl = head_slices[hd]
            s = _dot_nt(q_ref[rows, sl], k_ref[pl.ds(start, tk), sl])
            return jnp.where(col <= row, s, NEG_BIG) if diagonal else s

        def update(hd, s):
            m_prev = m_scr[hd, rows, :]
            m_new = jnp.maximum(m_prev, jnp.max(s, axis=-1, keepdims=True))
            p = jnp.exp2(s - jnp.concatenate([m_new] * (tk // LANES), axis=1)).astype(BF16)
            acc_scr[hd, rows, :] = (acc_scr[hd, rows, :] * jnp.exp2(m_prev - m_new)
                                    + _dot(p, v_ref[pl.ds(start, tk), head_slices[hd]]))
            m_scr[hd, rows, :] = m_new

        s_next = scores(0)
        for hd in range(heads):
            s_cur = s_next
            if hd + 1 < heads:
                s_next = scores(hd + 1)
            update(hd, s_cur)

    def body(j, carry):
        for r in range(nsub):
            block(j * nsub + r, 0, False)
        return carry

    lax.fori_loop(0, i, body, 0)
    for r in range(nsub):
        block(i * nsub + r, r * tk, True)
    outs = []
    for hd in range(heads):
        acc = acc_scr[hd]
        outs.append(acc[:, :V_DIM] / acc[:, V_DIM:V_DIM + 1])
    o_ref[...] = jnp.concatenate(outs, axis=-1).astype(o_ref.dtype)


def _mla_attention(q, k, v, *, tq, tk, heads):
    bsz, seq, hw = q.shape
    groups = MLA_HEADS // heads
    pw = heads * HEAD_PAD
    kv_spec = pl.BlockSpec((None, seq, pw), lambda b, h, i: (b, 0, h))
    return pl.pallas_call(
        functools.partial(_attn_kernel, tq=tq, tk=tk, heads=heads),
        grid=(bsz, groups, seq // tq),
        in_specs=[pl.BlockSpec((None, tq, pw), lambda b, h, i: (b, i, h)), kv_spec, kv_spec],
        out_specs=pl.BlockSpec((None, tq, heads * V_DIM), lambda b, h, i: (b, i, h)),
        out_shape=jax.ShapeDtypeStruct((bsz, seq, MLA_HEADS * V_DIM), BF16),
        scratch_shapes=[pltpu.VMEM((heads, tq, LANES), F32), pltpu.VMEM((heads, tq, HEAD_PAD), F32)],
        compiler_params=_params(("arbitrary", "arbitrary", "arbitrary")),
        name="mla_attn",
    )(q, k, v)


def _hgrn_sum_masks(c):
    t = np.arange(c)[:, None]
    s = np.arange(c)[None, :]
    blocks = [s <= t, s > t]
    blk = c // 2
    while blk >= 2:
        bnd = (t & ~(2 * blk - 1)) + blk - 1
        blocks.append(((s > bnd) & (s <= t)) | ((s > t) & (s <= bnd)))
        blk //= 2
    em = np.concatenate(blocks, axis=0).astype(np.float32)
    return np.concatenate([em, em], axis=1)


def _hgrn_kernel(hq_ref, hf_ref, hi_ref, hg_ref, lbt_ref, gout_ref, em_ref, o_ref, st_ref, *,
                 layer, chunk):
    @pl.when(pl.program_id(1) == 0)
    def _():
        st_ref[...] = jnp.zeros_like(st_ref)

    for sub in range(hq_ref.shape[0] // chunk):
        rs = pl.ds(sub * chunk, chunk)
        _hgrn_chunk(hq_ref.at[rs], hf_ref.at[rs], hi_ref.at[rs], hg_ref.at[rs], lbt_ref, gout_ref,
                    em_ref, o_ref.at[rs], st_ref, layer=layer)


def _hgrn_chunk(hq_ref, hf_ref, hi_ref, hg_ref, lbt_ref, gout_ref, em_ref, o_ref, st_ref, *, layer):
    c, w = hq_ref.shape
    tab = lbt_ref[...]
    e = jnp.exp(tab - jnp.max(tab, axis=0, keepdims=True))
    lb = jnp.sum(e[1:layer + 2], axis=0, keepdims=True) / jnp.sum(e, axis=0, keepdims=True)

    hq = hq_ref[...].astype(F32)
    q = hq * _sigmoid(hq)
    f = lb + (1.0 - lb) * _sigmoid(hf_ref[...])
    lf = jnp.log(f)
    kk = 1.0 - f
    v = hi_ref[...]

    lf_hi = lf.astype(BF16)
    lf_mid = (lf - lf_hi.astype(F32)).astype(BF16)
    sums = _dot(em_ref[...], jnp.concatenate([lf_hi, lf_mid], axis=0))
    b = sums[0:c]
    q_dec = (q * jnp.exp(b)).astype(BF16)
    k_last = (kk * jnp.exp(sums[c:2 * c])).astype(BF16)
    b_last = b[c - 1:c, :]

    n_levels = int(np.log2(c))
    sub = lax.broadcasted_iota(jnp.int32, (SUBLANES, w), 0)
    xs = []
    for lvl in range(n_levels):
        blk = 1 << lvl
        if blk >= SUBLANES:
            base = jnp.concatenate([(q if (j % 2) else kk)[j * blk:(j + 1) * blk]
                                    for j in range(c // blk)], axis=0)
        else:
            odd = jnp.concatenate([(sub & blk) != 0] * (c // SUBLANES), axis=0)
            base = jnp.where(odd, q * f, kk) if blk == 1 else jnp.where(odd, q, kk)
        if blk > 1:
            row0 = (2 + n_levels - 1 - lvl) * c
            base = base * jnp.exp(sums[row0:row0 + c])
        xs.append(base.astype(BF16))

    t_i = lax.broadcasted_iota(jnp.int32, (c, c), 0)
    s_i = lax.broadcasted_iota(jnp.int32, (c, c), 1)
    lower_xor = jnp.where(t_i > s_i, t_i ^ s_i, 0)
    level_of = [(lower_xor >> lvl) == 1 for lvl in range(n_levels)]
    on_diag = t_i == s_i
    qk = q * kk
    gout = gout_ref[...]
    head_slices = [slice(hd * HG_DIM, (hd + 1) * HG_DIM) for hd in range(HG_HEADS)]
    attns = [jnp.where(on_diag, jnp.sum(qk[:, hs], axis=-1, keepdims=True), 0.0)
             for hs in head_slices]
    for x, keep in zip(xs, level_of):
        attns = [jnp.where(keep, _dot_nt(x[:, hs], x[:, hs]), a) for hs, a in zip(head_slices, attns)]
    for hd, hs in enumerate(head_slices):
        attn = attns[hd]
        st = st_ref[hd]
        o = _dot_nt(q_dec[:, hs], st.astype(BF16)) + _dot(attn.astype(BF16), v[:, hs])
        hg = hg_ref[:, hs].astype(F32)
        o = o * _rms(o, HG_DIM) * gout * (hg * _sigmoid(hg))
        o_ref[:, hs] = o.astype(o_ref.dtype)
        st_ref[hd] = st * jnp.exp(b_last[:, hs]) + _dot_tn(v[:, hs], k_last[:, hs])


def _hgrn2(hq, hf, hi, hg, lb_table, gout, *, chunk, chunks_per_step, layer):
    bsz, seq, w = hq.shape
    rows = chunk * chunks_per_step
    blk = pl.BlockSpec((None, rows, w), lambda b, c: (b, c, 0))
    em = jnp.asarray(_hgrn_sum_masks(chunk), dtype=BF16)
    return pl.pallas_call(
        functools.partial(_hgrn_kernel, layer=layer, chunk=chunk),
        grid=(bsz, seq // rows),
        in_specs=[blk, blk, blk, blk, _const_spec(lb_table.shape), _const_spec(gout.shape),
                  _const_spec(em.shape)],
        out_specs=blk,
        out_shape=jax.ShapeDtypeStruct((bsz, seq, w), BF16),
        scratch_shapes=[pltpu.VMEM((HG_HEADS, HG_DIM, HG_DIM), F32)],
        compiler_params=_params(("arbitrary", "arbitrary")),
        name="hgrn2",
    )(hq, hf, hi, hg, lb_table, gout, em)


def _merge_ffn_kernel(a_ref, o_ref, ga_ref, gb_ref, x_ref, modn_ref, modp_ref, wa_ref, wb_ref, wo_ref,
                      g2_ref, wup_ref, cw_ref, cb_ref, wdn_ref, out_ref, x1_scr, h2_scr, up_scr, *, tps):
    tm, d = x_ref.shape
    dff = wdn_ref.shape[0]
    halo = SUBLANES
    i = pl.program_id(0)
    nxt = i % 2
    prv = 1 - nxt

    @pl.when(i == 0)
    def _():
        rows = 2 * SUBLANES

        def clear(r, carry):
            start = pl.multiple_of(r * rows, rows)
            x1_scr[prv, pl.ds(start, rows), :] = jnp.zeros((rows, d), F32)
            h2_scr[prv, pl.ds(start, rows), :] = jnp.zeros((rows, d), BF16)
            return carry

        lax.fori_loop(0, tm // rows, clear, 0)

    @pl.when(jnp.logical_or(i == 0, (i + tps - 1) % tps == 0))
    def _():
        up_scr[0:halo, :] = jnp.zeros((halo, up_scr.shape[1]), F32)

    ya = _dot(a_ref[...], wa_ref[...])
    yb = _dot(o_ref[...], wb_ref[...])
    up = _dot(h2_scr[prv], wup_ref[...])
    merged = _sigmoid(ga_ref[...].astype(F32)) * ya + _sigmoid(gb_ref[...].astype(F32)) * yb
    z1 = _dot(merged.astype(BF16), wo_ref[...])

    up_scr[halo:halo + tm, :] = up
    y = cb_ref[...] + cw_ref[CONV_W - 1:CONV_W, :] * up
    for j in range(CONV_W - 1):
        back = CONV_W - 1 - j
        y = y + cw_ref[j:j + 1, :] * up_scr[halo - back:halo - back + tm, :]
    up_scr[0:halo, :] = up[tm - halo:, :]
    gate = y[:, :dff]
    act = (gate * _sigmoid(gate) * y[:, dff:]).astype(BF16)
    out_ref[...] = x1_scr[prv] + modp_ref[5:6, :] * _dot(act, wdn_ref[...])

    x1 = x_ref[...] + modn_ref[2:3, :] * z1
    x1_scr[nxt] = x1
    h2 = (x1 * _rms(x1, d) * g2_ref[...]) * (1.0 + modn_ref[4:5, :]) + modn_ref[3:4, :]
    h2_scr[nxt] = h2.astype(BF16)


def _merge_ffn(attn, hgo, ga, gb, x2, mod, wa, wb, wo, g2, wup, cw, cb, wdn, *, seq, tm):
    t, d = x2.shape
    tps = seq // tm
    n = t // tm
    cur = lambda i: jnp.minimum(i, n - 1)
    prev = lambda i: jnp.maximum(i - 1, 0)
    row = lambda w: pl.BlockSpec((tm, w), lambda i: (cur(i), 0))
    consts = [wa, wb, wo, g2, wup, cw, cb, wdn]
    return pl.pallas_call(
        functools.partial(_merge_ffn_kernel, tps=tps),
        grid=(n + 1,),
        in_specs=[row(attn.shape[1]), row(hgo.shape[1]), row(d), row(d), row(d),
                  pl.BlockSpec((None, 6, d), lambda i: (cur(i) // tps, 0, 0)),
                  pl.BlockSpec((None, 6, d), lambda i: (prev(i) // tps, 0, 0))]
                 + [_const_spec(a.shape) for a in consts],
        out_specs=pl.BlockSpec((tm, d), lambda i: (prev(i), 0)),
        out_shape=jax.ShapeDtypeStruct((t, d), F32),
        scratch_shapes=[pltpu.VMEM((2, tm, d), F32), pltpu.VMEM((2, tm, d), BF16),
                        pltpu.VMEM((tm + SUBLANES, wup.shape[1]), F32)],
        compiler_params=_params(("arbitrary",)),
        name="merge_ffn",
    )(attn, hgo, ga, gb, x2, mod, mod, *consts)


def _pad_heads(w, used, n_heads):
    rows = w.shape[0]
    w = w.reshape(rows, n_heads, used)
    w = jnp.pad(w, ((0, 0), (0, 0), (0, HEAD_PAD - used)))
    return w.reshape(rows, n_heads * HEAD_PAD)


def _layer(x2, pos, invf, mod, p, *, bsz, seq, layer):
    t, d = x2.shape
    w_in = p["w_in"]
    o_ckv = Q_LORA
    o_h = o_ckv + KV_LORA + ROPE_DIM
    o_g = o_h + 4 * HG_WIDTH
    wcq = w_in[:, :o_ckv].astype(BF16)
    wckv = jnp.concatenate([w_in[:, o_ckv:o_ckv + KV_LORA],
                            jnp.zeros((d, NOPE_DIM), F32),
                            w_in[:, o_ckv + KV_LORA:o_h],
                            jnp.zeros((d, HEAD_PAD - QK_DIM), F32)], axis=1).astype(BF16)
    wh = w_in[:, o_h:o_g].astype(BF16)
    wg = w_in[:, o_g:].astype(BF16)
    wuq = _pad_heads(p["w_uq"], QK_DIM, MLA_HEADS).astype(BF16)
    wukv = p["w_ukv"].reshape(KV_LORA, MLA_HEADS, NOPE_DIM + V_DIM)
    wk = _pad_heads(wukv[:, :, :NOPE_DIM].reshape(KV_LORA, -1), NOPE_DIM, MLA_HEADS).astype(BF16)
    wv = _pad_heads(wukv[:, :, NOPE_DIM:].reshape(KV_LORA, -1), V_DIM, MLA_HEADS).astype(BF16)
    pad_g = lambda g: jnp.pad(g, (0, HEAD_PAD - QK_DIM)).reshape(1, HEAD_PAD)

    tm = min(256, seq)
    q, k, v, hq, hf, hi, hg, ga, gb = _mixer_in(
        x2, pos.reshape(t // tm, 1, tm), invf, mod, p["norm1_g"].reshape(1, d), wcq, wckv, wh, wg,
        p["q_a_norm_g"].reshape(1, -1), wuq, p["kv_a_norm_g"].reshape(1, -1), wk, wv,
        pad_g(p["q_norm_g"]), pad_g(p["k_norm_g"]), seq=seq, tm=tm)

    hw = MLA_HEADS * HEAD_PAD
    attn = _mla_attention(q.reshape(bsz, seq, hw), k.reshape(bsz, seq, hw), v.reshape(bsz, seq, hw),
                          tq=min(1024, seq), tk=min(512, seq), heads=4)
    r3 = lambda a: a.reshape(bsz, seq, HG_WIDTH)
    hgo = _hgrn2(r3(hq), r3(hf), r3(hi), r3(hg), p["hg_lower_bound"],
                 p["hg_out_norm_g"].reshape(1, HG_DIM), chunk=128, chunks_per_step=4, layer=layer)

    return _merge_ffn(attn.reshape(t, -1), hgo.reshape(t, -1), ga, gb, x2, mod,
                      p["w_branch_a"].astype(BF16), p["w_branch_b"].astype(BF16),
                      p["w_out"].astype(BF16), p["norm2_g"].reshape(1, d),
                      p["w_up"].astype(BF16), p["conv_w"], p["conv_b"].reshape(1, -1),
                      p["w_down"].astype(BF16), seq=seq, tm=tm)


def kernel(x, c, positions, w_ada, b_ada, norm1_g, w_in, q_a_norm_g, w_uq, kv_a_norm_g, w_ukv,
           q_norm_g, k_norm_g, hg_lower_bound, hg_out_norm_g, w_branch_a, w_branch_b, w_out,
           norm2_g, w_up, conv_w, conv_b, w_down):
    bsz, seq, d = x.shape
    depth = w_ada.shape[0]
    invf = (ROPE_THETA ** (-jnp.arange(0, ROPE_DIM, 2, dtype=F32) / ROPE_DIM)).reshape(HALF_ROPE, 1)
    x2 = x.reshape(bsz * seq, d)
    pos = positions.reshape(bsz * seq)
    for l in range(depth):
        mod = _ada_mod(c, w_ada, b_ada, l).reshape(bsz, 6, d)
        p = dict(norm1_g=norm1_g[l], w_in=w_in[l], q_a_norm_g=q_a_norm_g[l], w_uq=w_uq[l],
                 kv_a_norm_g=kv_a_norm_g[l], w_ukv=w_ukv[l], q_norm_g=q_norm_g[l],
                 k_norm_g=k_norm_g[l], hg_lower_bound=hg_lower_bound, hg_out_norm_g=hg_out_norm_g[l],
                 w_branch_a=w_branch_a[l], w_branch_b=w_branch_b[l], w_out=w_out[l],
                 norm2_g=norm2_g[l], w_up=w_up[l], conv_w=conv_w[l], conv_b=conv_b[l],
                 w_down=w_down[l])
        x2 = _layer(x2, pos, invf, mod, p, bsz=bsz, seq=seq, layer=l)
    return x2.reshape(bsz, seq, d)
```

```python
import functools

import numpy as np
import jax
import jax.numpy as jnp
from jax import lax
from jax.experimental import pallas as pl
from jax.experimental.pallas import tpu as pltpu

F32 = jnp.float32
BF16 = jnp.bfloat16

LANES = 128
SUBLANES = 8
VMEM_LIMIT = 56 * 1024 * 1024

MLA_HEADS = 8
NOPE_DIM = 64
ROPE_DIM = 32
HALF_ROPE = ROPE_DIM // 2
QK_DIM = NOPE_DIM + ROPE_DIM
V_DIM = 64
Q_LORA = 512
KV_LORA = 256
ROPE_THETA = 10000.0
HG_HEADS = 4
HG_DIM = 128
HG_WIDTH = HG_HEADS * HG_DIM
CONV_W = 3
EPS = 1e-6
HEAD_PAD = LANES
LOG2E = float(np.log2(np.e))
NEG_BIG = -1e30


def _dot(a, b):
    return jnp.dot(a, b, preferred_element_type=F32)


def _dot_nt(a, b):
    return lax.dot_general(a, b, (((1,), (1,)), ((), ())), preferred_element_type=F32)


def _dot_tn(a, b):
    return lax.dot_general(a, b, (((0,), (0,)), ((), ())), preferred_element_type=F32)


def _sigmoid(x):
    return 0.5 * jnp.tanh(0.5 * x) + 0.5


def _rms(x, width):
    return lax.rsqrt(jnp.sum(x * x, axis=-1, keepdims=True) * (1.0 / width) + EPS)


def _const_spec(shape):
    nd = len(shape)
    return pl.BlockSpec(shape, lambda *_: (0,) * nd, pipeline_mode=pl.Buffered(1))


def _params(sem):
    return pltpu.CompilerParams(dimension_semantics=sem, vmem_limit_bytes=VMEM_LIMIT)


def _ada_kernel(c_ref, w_ref, b_ref, o_ref):
    c = c_ref[...]
    w = w_ref[...]
    c_hi = c.astype(BF16)
    c_lo = (c - c_hi.astype(F32)).astype(BF16)
    w_hi = w.astype(BF16)
    w_lo = (w - w_hi.astype(F32)).astype(BF16)
    o_ref[...] = _dot(c_hi, w_hi) + _dot(c_hi, w_lo) + _dot(c_lo, w_hi) + b_ref[...]


def _ada_mod(c, w_ada, b_ada, layer):
    bsz, d = c.shape
    depth, _, n = w_ada.shape
    bn = 1024
    return pl.pallas_call(
        _ada_kernel,
        grid=(n // bn,),
        in_specs=[pl.BlockSpec((bsz, d), lambda j: (0, 0)),
                  pl.BlockSpec((None, d, bn), lambda j: (layer, 0, j)),
                  pl.BlockSpec((None, 1, bn), lambda j: (layer, 0, j))],
        out_specs=pl.BlockSpec((bsz, bn), lambda j: (0, j)),
        out_shape=jax.ShapeDtypeStruct((bsz, n), F32),
        compiler_params=_params(("arbitrary",)),
        name="ada_mod",
    )(c, w_ada, b_ada.reshape(depth, 1, n))


def _rope(t, cos_t, sin_lo, sin_hi):
    return (t * cos_t + pltpu.roll(t, LANES - HALF_ROPE, 1) * sin_lo
            + pltpu.roll(t, HALF_ROPE, 1) * sin_hi)


def _mixer_in_kernel(x_ref, pos_ref, invf_ref, mod_ref, g1_ref, wcq_ref, wckv_ref, wh_ref, wg_ref,
                     gqa_ref, wuq_ref, gkva_ref, wk_ref, wv_ref, gq_ref, gk_ref,
                     q_ref, k_ref, vt_ref, hq_ref, hf_ref, hi_ref, hg_ref, ga_ref, gb_ref):
    tm, d = x_ref.shape
    x = x_ref[...]
    h = (x * _rms(x, d) * g1_ref[...]) * (1.0 + mod_ref[1:2, :]) + mod_ref[0:1, :]
    hb = h.astype(BF16)

    ang = invf_ref[...] * pos_ref[...].astype(F32)
    cos_h = jnp.cos(ang)
    sin_h = jnp.sin(ang)
    zeros = lambda r: jnp.zeros((r, tm), F32)
    cos_t = jnp.concatenate([jnp.ones((NOPE_DIM, tm), F32), cos_h, cos_h,
                             zeros(HEAD_PAD - QK_DIM)], axis=0).T
    sin_lo = jnp.concatenate([zeros(NOPE_DIM), -sin_h, zeros(HEAD_PAD - NOPE_DIM - HALF_ROPE)],
                             axis=0).T
    sin_hi = jnp.concatenate([zeros(NOPE_DIM + HALF_ROPE), sin_h, zeros(HEAD_PAD - QK_DIM)],
                             axis=0).T

    cq = _dot(hb, wcq_ref[...])
    ckv = _dot(hb, wckv_ref[...])
    hh = _dot(hb, wh_ref[...])
    hq_ref[...] = hh[:, 0 * HG_WIDTH:1 * HG_WIDTH].astype(BF16)
    hf_ref[...] = hh[:, 1 * HG_WIDTH:2 * HG_WIDTH]
    hi_ref[...] = hh[:, 2 * HG_WIDTH:3 * HG_WIDTH].astype(BF16)
    hg_ref[...] = hh[:, 3 * HG_WIDTH:4 * HG_WIDTH].astype(BF16)

    cqn = (cq * _rms(cq, Q_LORA) * gqa_ref[...]).astype(BF16)
    ckv_c = ckv[:, :KV_LORA]
    kr = ckv[:, KV_LORA:]
    ckvn = (ckv_c * _rms(ckv_c, KV_LORA) * gkva_ref[...]).astype(BF16)
    qf = _dot(cqn, wuq_ref[...])
    kf = _dot(ckvn, wk_ref[...])
    vt = _dot_nt(wv_ref[...], ckvn)
    gg = _dot(hb, wg_ref[...])
    ga_ref[...] = gg[:, :d].astype(BF16)
    gb_ref[...] = gg[:, d:].astype(BF16)

    gq = gq_ref[...]
    for hd in range(MLA_HEADS):
        sl = slice(hd * HEAD_PAD, (hd + 1) * HEAD_PAD)
        qh = qf[:, sl]
        qn = qh * _rms(qh, QK_DIM) * gq
        q_ref[:, sl] = (_rope(qn, cos_t, sin_lo, sin_hi) * (LOG2E / float(np.sqrt(QK_DIM)))).astype(BF16)

    gk = gk_ref[...]
    ss_rope = jnp.sum(kr * kr, axis=-1, keepdims=True)
    kr_rot = _rope(kr * gk, cos_t, sin_lo, sin_hi)
    for hd in range(MLA_HEADS):
        sl = slice(hd * HEAD_PAD, (hd + 1) * HEAD_PAD)
        kn = kf[:, sl]
        ss = jnp.sum(kn * kn, axis=-1, keepdims=True) + ss_rope
        k_ref[:, sl] = ((kn * gk + kr_rot) * lax.rsqrt(ss * (1.0 / QK_DIM) + EPS)).astype(BF16)
    first_sub = lax.broadcasted_iota(jnp.int32, (SUBLANES, tm), 0) == 0
    pieces = []
    for hd in range(MLA_HEADS):
        r0 = hd * HEAD_PAD
        pieces += [vt[r0:r0 + V_DIM],
                   jnp.where(first_sub, 1.0, vt[r0 + V_DIM:r0 + V_DIM + SUBLANES]),
                   vt[r0 + V_DIM + SUBLANES:r0 + HEAD_PAD]]
    vt_ref[...] = jnp.concatenate(pieces, axis=0).astype(BF16)


def _mixer_in(x2, pos, invf, mod, g1, wcq, wckv, wh, wg, gqa, wuq, gkva, wk, wv, gq, gk, *,
              seq, tm, tk):
    t, d = x2.shape
    tps = seq // tm
    per_kv = tk // tm
    hw = MLA_HEADS * HEAD_PAD
    row = lambda w: pl.BlockSpec((tm, w), lambda i: (i, 0))
    vt_spec = pl.BlockSpec((None, None, hw, tm),
                           lambda i: (i // tps, (i % tps) // per_kv, 0, (i % tps) % per_kv))
    consts = [g1, wcq, wckv, wh, wg, gqa, wuq, gkva, wk, wv, gq, gk]
    out_shapes = [jax.ShapeDtypeStruct((t, hw), BF16)] * 2 + [
        jax.ShapeDtypeStruct((t // seq, seq // tk, hw, tk), BF16)] + [
        jax.ShapeDtypeStruct((t, HG_WIDTH), BF16), jax.ShapeDtypeStruct((t, HG_WIDTH), F32),
        jax.ShapeDtypeStruct((t, HG_WIDTH), BF16), jax.ShapeDtypeStruct((t, HG_WIDTH), BF16),
        jax.ShapeDtypeStruct((t, d), BF16), jax.ShapeDtypeStruct((t, d), BF16)]
    return pl.pallas_call(
        _mixer_in_kernel,
        grid=(t // tm,),
        in_specs=[row(d),
                  pl.BlockSpec((None, 1, tm), lambda i: (i, 0, 0)),
                  _const_spec(invf.shape),
                  pl.BlockSpec((None, 6, d), lambda i: (i // tps, 0, 0))]
                 + [_const_spec(a.shape) for a in consts],
        out_specs=[row(hw)] * 2 + [vt_spec] + [row(HG_WIDTH)] * 4 + [row(d)] * 2,
        out_shape=out_shapes,
        compiler_params=_params(("arbitrary",)),
        name="mixer_in",
    )(x2, pos, invf, mod, *consts)


def _attn_kernel(q_ref, k_ref, vt_ref, o_ref, m_scr, acc_scr, *, tq, tk, heads):
    i = pl.program_id(2)
    nsub = tq // tk
    head_slices = [slice(hd * HEAD_PAD, (hd + 1) * HEAD_PAD) for hd in range(heads)]
    m_scr[...] = jnp.full(m_scr.shape, NEG_BIG, F32)
    acc_scr[...] = jnp.zeros(acc_scr.shape, F32)

    def block(j, row0, diagonal):
        start = pl.multiple_of(j * tk, tk)
        rows = slice(row0, tq)
        nq = tq - row0
        stack = lambda a, n: jnp.concatenate([a] * (n // SUBLANES), axis=0)
        if diagonal:
            kv_i = lax.broadcasted_iota(jnp.int32, (tk, nq), 0)
            q_i = lax.broadcasted_iota(jnp.int32, (tk, nq), 1)

        def scores(hd):
            sl = head_slices[hd]
            s = _dot_nt(k_ref[pl.ds(start, tk), sl], q_ref[rows, sl])
            return jnp.where(kv_i <= q_i, s, NEG_BIG) if diagonal else s

        def update(hd, s):
            m_prev = m_scr[hd, :, rows]
            m_new = jnp.maximum(m_prev, jnp.max(s, axis=0, keepdims=True))
            p = jnp.exp2(s - stack(m_new, tk)).astype(BF16)
            acc_scr[hd, :, rows] = (acc_scr[hd, :, rows] * stack(jnp.exp2(m_prev - m_new), HEAD_PAD)
                                    + _dot(vt_ref[j, head_slices[hd], :], p))
            m_scr[hd, :, rows] = m_new

        ahead = 2
        pending = [scores(hd) for hd in range(min(ahead, heads))]
        for hd in range(heads):
            if hd + ahead < heads:
                pending.append(scores(hd + ahead))
            update(hd, pending.pop(0))

    def body(j, carry):
        for r in range(nsub):
            block(j * nsub + r, 0, False)
        return carry

    lax.fori_loop(0, i, body, 0)
    for r in range(nsub):
        block(i * nsub + r, r * tk, True)
    outs = []
    for hd in range(heads):
        acc = acc_scr[hd]
        outs.append((acc / acc[V_DIM:V_DIM + 1, :]).T[:, :V_DIM])
    o_ref[...] = jnp.concatenate(outs, axis=-1).astype(o_ref.dtype)


def _mla_attention(q, k, vt, *, tq, tk, heads):
    bsz, seq, hw = q.shape
    groups = MLA_HEADS // heads
    pw = heads * HEAD_PAD
    return pl.pallas_call(
        functools.partial(_attn_kernel, tq=tq, tk=tk, heads=heads),
        grid=(bsz, groups, seq // tq),
        in_specs=[pl.BlockSpec((None, tq, pw), lambda b, h, i: (b, i, h)),
                  pl.BlockSpec((None, seq, pw), lambda b, h, i: (b, 0, h)),
                  pl.BlockSpec((None, seq // tk, pw, tk), lambda b, h, i: (b, 0, h, 0))],
        out_specs=pl.BlockSpec((None, tq, heads * V_DIM), lambda b, h, i: (b, i, h)),
        out_shape=jax.ShapeDtypeStruct((bsz, seq, MLA_HEADS * V_DIM), BF16),
        scratch_shapes=[pltpu.VMEM((heads, SUBLANES, tq), F32),
                        pltpu.VMEM((heads, HEAD_PAD, tq), F32)],
        compiler_params=_params(("arbitrary", "arbitrary", "arbitrary")),
        name="mla_attn",
    )(q, k, vt)


def _hgrn_sum_masks(c):
    t = np.arange(c)[:, None]
    s = np.arange(c)[None, :]
    blocks = [s <= t, s > t]
    blk = c // 2
    while blk >= 2:
        bnd = (t & ~(2 * blk - 1)) + blk - 1
        blocks.append(((s > bnd) & (s <= t)) | ((s > t) & (s <= bnd)))
        blk //= 2
    em = np.concatenate(blocks, axis=0).astype(np.float32)
    return np.concatenate([em, em], axis=1)


def _hgrn_kernel(hq_ref, hf_ref, hi_ref, hg_ref, lbt_ref, gout_ref, em_ref, o_ref, st_ref, *,
                 layer, chunk):
    @pl.when(pl.program_id(1) == 0)
    def _():
        st_ref[...] = jnp.zeros_like(st_ref)

    for sub in range(hq_ref.shape[0] // chunk):
        rs = pl.ds(sub * chunk, chunk)
        _hgrn_chunk(hq_ref.at[rs], hf_ref.at[rs], hi_ref.at[rs], hg_ref.at[rs], lbt_ref, gout_ref,
                    em_ref, o_ref.at[rs], st_ref, layer=layer)


def _hgrn_chunk(hq_ref, hf_ref, hi_ref, hg_ref, lbt_ref, gout_ref, em_ref, o_ref, st_ref, *, layer):
    c, w = hq_ref.shape
    tab = lbt_ref[...]
    e = jnp.exp(tab - jnp.max(tab, axis=0, keepdims=True))
    lb = jnp.sum(e[1:layer + 2], axis=0, keepdims=True) / jnp.sum(e, axis=0, keepdims=True)

    hq = hq_ref[...].astype(F32)
    q = hq * _sigmoid(hq)
    f = lb + (1.0 - lb) * _sigmoid(hf_ref[...])
    lf = jnp.log(f)
    kk = 1.0 - f
    v = hi_ref[...]

    lf_hi = lf.astype(BF16)
    lf_mid = (lf - lf_hi.astype(F32)).astype(BF16)
    sums = _dot(em_ref[...], jnp.concatenate([lf_hi, lf_mid], axis=0))
    b = sums[0:c]
    q_dec = (q * jnp.exp(b)).astype(BF16)
    k_last = (kk * jnp.exp(sums[c:2 * c])).astype(BF16)
    b_last = b[c - 1:c, :]

    n_levels = int(np.log2(c))
    sub = lax.broadcasted_iota(jnp.int32, (SUBLANES, w), 0)
    xs = []
    for lvl in range(n_levels):
        blk = 1 << lvl
        if blk >= SUBLANES:
            base = jnp.concatenate([(q if (j % 2) else kk)[j * blk:(j + 1) * blk]
                                    for j in range(c // blk)], axis=0)
        else:
            odd = jnp.concatenate([(sub & blk) != 0] * (c // SUBLANES), axis=0)
            base = jnp.where(odd, q * f, kk) if blk == 1 else jnp.where(odd, q, kk)
        if blk > 1:
            row0 = (2 + n_levels - 1 - lvl) * c
            base = base * jnp.exp(sums[row0:row0 + c])
        xs.append(base.astype(BF16))

    t_i = lax.broadcasted_iota(jnp.int32, (c, c), 0)
    s_i = lax.broadcasted_iota(jnp.int32, (c, c), 1)
    lower_xor = jnp.where(t_i > s_i, t_i ^ s_i, 0)
    level_of = [(lower_xor >> lvl) == 1 for lvl in range(n_levels)]
    on_diag = t_i == s_i
    qk = q * kk
    gout = gout_ref[...]
    head_slices = [slice(hd * HG_DIM, (hd + 1) * HG_DIM) for hd in range(HG_HEADS)]
    attns = [jnp.where(on_diag, jnp.sum(qk[:, hs], axis=-1, keepdims=True), 0.0)
             for hs in head_slices]
    for x, keep in zip(xs, level_of):
        attns = [jnp.where(keep, _dot_nt(x[:, hs], x[:, hs]), a) for hs, a in zip(head_slices, attns)]
    for hd, hs in enumerate(head_slices):
        attn = attns[hd]
        st = st_ref[hd]
        o = _dot_nt(q_dec[:, hs], st.astype(BF16)) + _dot(attn.astype(BF16), v[:, hs])
        hg = hg_ref[:, hs].astype(F32)
        o = o * _rms(o, HG_DIM) * gout * (hg * _sigmoid(hg))
        o_ref[:, hs] = o.astype(o_ref.dtype)
        st_ref[hd] = st * jnp.exp(b_last[:, hs]) + _dot_tn(v[:, hs], k_last[:, hs])


def _hgrn2(hq, hf, hi, hg, lb_table, gout, *, chunk, chunks_per_step, layer):
    bsz, seq, w = hq.shape
    rows = chunk * chunks_per_step
    blk = pl.BlockSpec((None, rows, w), lambda b, c: (b, c, 0))
    em = jnp.asarray(_hgrn_sum_masks(chunk), dtype=BF16)
    return pl.pallas_call(
        functools.partial(_hgrn_kernel, layer=layer, chunk=chunk),
        grid=(bsz, seq // rows),
        in_specs=[blk, blk, blk, blk, _const_spec(lb_table.shape), _const_spec(gout.shape),
                  _const_spec(em.shape)],
        out_specs=blk,
        out_shape=jax.ShapeDtypeStruct((bsz, seq, w), BF16),
        scratch_shapes=[pltpu.VMEM((HG_HEADS, HG_DIM, HG_DIM), F32)],
        compiler_params=_params(("arbitrary", "arbitrary")),
        name="hgrn2",
    )(hq, hf, hi, hg, lb_table, gout, em)


def _merge_ffn_kernel(a_ref, o_ref, ga_ref, gb_ref, x_ref, modn_ref, modp_ref, wa_ref, wb_ref, wo_ref,
                      g2_ref, wup_ref, cw_ref, cb_ref, wdn_ref, out_ref, x1_scr, h2_scr, up_scr, *, tps):
    tm, d = x_ref.shape
    dff = wdn_ref.shape[0]
    halo = SUBLANES
    i = pl.program_id(0)
    nxt = i % 2
    prv = 1 - nxt

    @pl.when(i == 0)
    def _():
        rows = 2 * SUBLANES

        def clear(r, carry):
            start = pl.multiple_of(r * rows, rows)
            x1_scr[prv, pl.ds(start, rows), :] = jnp.zeros((rows, d), F32)
            h2_scr[prv, pl.ds(start, rows), :] = jnp.zeros((rows, d), BF16)
            return carry

        lax.fori_loop(0, tm // rows, clear, 0)

    @pl.when(jnp.logical_or(i == 0, (i + tps - 1) % tps == 0))
    def _():
        up_scr[0:halo, :] = jnp.zeros((halo, up_scr.shape[1]), F32)

    ya = _dot(a_ref[...], wa_ref[...])
    yb = _dot(o_ref[...], wb_ref[...])
    up = _dot(h2_scr[prv], wup_ref[...])
    merged = _sigmoid(ga_ref[...].astype(F32)) * ya + _sigmoid(gb_ref[...].astype(F32)) * yb
    z1 = _dot(merged.astype(BF16), wo_ref[...])

    up_scr[halo:halo + tm, :] = up
    y = cb_ref[...] + cw_ref[CONV_W - 1:CONV_W, :] * up
    for j in range(CONV_W - 1):
        back = CONV_W - 1 - j
        y = y + cw_ref[j:j + 1, :] * up_scr[halo - back:halo - back + tm, :]
    up_scr[0:halo, :] = up[tm - halo:, :]
    gate = y[:, :dff]
    act = (gate * _sigmoid(gate) * y[:, dff:]).astype(BF16)
    out_ref[...] = x1_scr[prv] + modp_ref[5:6, :] * _dot(act, wdn_ref[...])

    x1 = x_ref[...] + modn_ref[2:3, :] * z1
    x1_scr[nxt] = x1
    h2 = (x1 * _rms(x1, d) * g2_ref[...]) * (1.0 + modn_ref[4:5, :]) + modn_ref[3:4, :]
    h2_scr[nxt] = h2.astype(BF16)


def _merge_ffn(attn, hgo, ga, gb, x2, mod, wa, wb, wo, g2, wup, cw, cb, wdn, *, seq, tm):
    t, d = x2.shape
    tps = seq // tm
    n = t // tm
    cur = lambda i: jnp.minimum(i, n - 1)
    prev = lambda i: jnp.maximum(i - 1, 0)
    row = lambda w: pl.BlockSpec((tm, w), lambda i: (cur(i), 0))
    consts = [wa, wb, wo, g2, wup, cw, cb, wdn]
    return pl.pallas_call(
        functools.partial(_merge_ffn_kernel, tps=tps),
        grid=(n + 1,),
        in_specs=[row(attn.shape[1]), row(hgo.shape[1]), row(d), row(d), row(d),
                  pl.BlockSpec((None, 6, d), lambda i: (cur(i) // tps, 0, 0)),
                  pl.BlockSpec((None, 6, d), lambda i: (prev(i) // tps, 0, 0))]
                 + [_const_spec(a.shape) for a in consts],
        out_specs=pl.BlockSpec((tm, d), lambda i: (prev(i), 0)),
        out_shape=jax.ShapeDtypeStruct((t, d), F32),
        scratch_shapes=[pltpu.VMEM((2, tm, d), F32), pltpu.VMEM((2, tm, d), BF16),
                        pltpu.VMEM((tm + SUBLANES, wup.shape[1]), F32)],
        compiler_params=_params(("arbitrary",)),
        name="merge_ffn",
    )(attn, hgo, ga, gb, x2, mod, mod, *consts)


def _pad_heads(w, used, n_heads):
    rows = w.shape[0]
    w = w.reshape(rows, n_heads, used)
    w = jnp.pad(w, ((0, 0), (0, 0), (0, HEAD_PAD - used)))
    return w.reshape(rows, n_heads * HEAD_PAD)


def _layer(x2, pos, invf, mod, p, *, bsz, seq, layer):
    t, d = x2.shape
    w_in = p["w_in"]
    o_ckv = Q_LORA
    o_h = o_ckv + KV_LORA + ROPE_DIM
    o_g = o_h + 4 * HG_WIDTH
    wcq = w_in[:, :o_ckv].astype(BF16)
    wckv = jnp.concatenate([w_in[:, o_ckv:o_ckv + KV_LORA],
                            jnp.zeros((d, NOPE_DIM), F32),
                            w_in[:, o_ckv + KV_LORA:o_h],
                            jnp.zeros((d, HEAD_PAD - QK_DIM), F32)], axis=1).astype(BF16)
    wh = w_in[:, o_h:o_g].astype(BF16)
    wg = w_in[:, o_g:].astype(BF16)
    wuq = _pad_heads(p["w_uq"], QK_DIM, MLA_HEADS).astype(BF16)
    wukv = p["w_ukv"].reshape(KV_LORA, MLA_HEADS, NOPE_DIM + V_DIM)
    wk = _pad_heads(wukv[:, :, :NOPE_DIM].reshape(KV_LORA, -1), NOPE_DIM, MLA_HEADS).astype(BF16)
    wvt = _pad_heads(wukv[:, :, NOPE_DIM:].reshape(KV_LORA, -1), V_DIM, MLA_HEADS).T.astype(BF16)
    pad_g = lambda g: jnp.pad(g, (0, HEAD_PAD - QK_DIM)).reshape(1, HEAD_PAD)

    tm = min(256, seq)
    tq, tk = min(1024, seq), min(512, seq)
    q, k, vt, hq, hf, hi, hg, ga, gb = _mixer_in(
        x2, pos.reshape(t // tm, 1, tm), invf, mod, p["norm1_g"].reshape(1, d), wcq, wckv, wh, wg,
        p["q_a_norm_g"].reshape(1, -1), wuq, p["kv_a_norm_g"].reshape(1, -1), wk, wvt,
        pad_g(p["q_norm_g"]), pad_g(p["k_norm_g"]), seq=seq, tm=tm, tk=tk)

    hw = MLA_HEADS * HEAD_PAD
    attn = _mla_attention(q.reshape(bsz, seq, hw), k.reshape(bsz, seq, hw), vt,
                          tq=tq, tk=tk, heads=4)
    r3 = lambda a: a.reshape(bsz, seq, HG_WIDTH)
    hgo = _hgrn2(r3(hq), r3(hf), r3(hi), r3(hg), p["hg_lower_bound"],
                 p["hg_out_norm_g"].reshape(1, HG_DIM), chunk=128, chunks_per_step=4, layer=layer)

    return _merge_ffn(attn.reshape(t, -1), hgo.reshape(t, -1), ga, gb, x2, mod,
                      p["w_branch_a"].astype(BF16), p["w_branch_b"].astype(BF16),
                      p["w_out"].astype(BF16), p["norm2_g"].reshape(1, d),
                      p["w_up"].astype(BF16), p["conv_w"], p["conv_b"].reshape(1, -1),
                      p["w_down"].astype(BF16), seq=seq, tm=tm)


def kernel(x, c, positions, w_ada, b_ada, norm1_g, w_in, q_a_norm_g, w_uq, kv_a_norm_g, w_ukv,
           q_norm_g, k_norm_g, hg_lower_bound, hg_out_norm_g, w_branch_a, w_branch_b, w_out,
           norm2_g, w_up, conv_w, conv_b, w_down):
    bsz, seq, d = x.shape
    depth = w_ada.shape[0]
    invf = (ROPE_THETA ** (-jnp.arange(0, ROPE_DIM, 2, dtype=F32) / ROPE_DIM)).reshape(HALF_ROPE, 1)
    x2 = x.reshape(bsz * seq, d)
    pos = positions.reshape(bsz * seq)
    for l in range(depth):
        mod = _ada_mod(c, w_ada, b_ada, l).reshape(bsz, 6, d)
        p = dict(norm1_g=norm1_g[l], w_in=w_in[l], q_a_norm_g=q_a_norm_g[l], w_uq=w_uq[l],
                 kv_a_norm_g=kv_a_norm_g[l], w_ukv=w_ukv[l], q_norm_g=q_norm_g[l],
                 k_norm_g=k_norm_g[l], hg_lower_bound=hg_lower_bound, hg_out_norm_g=hg_out_norm_g[l],
                 w_branch_a=w_branch_a[l], w_branch_b=w_branch_b[l], w_out=w_out[l],
                 norm2_g=norm2_g[l], w_up=w_up[l], conv_w=conv_w[l], conv_b=conv_b[l],
                 w_down=w_down[l])
        x2 = _layer(x2, pos, invf, mod, p, bsz=bsz, seq=seq, layer=l)
    return x2.reshape(bsz, seq, d)
```

```python
import functools

import numpy as np
import jax
import jax.numpy as jnp
from jax import lax
from jax.experimental import pallas as pl
from jax.experimental.pallas import tpu as pltpu

F32 = jnp.float32
BF16 = jnp.bfloat16

LANES = 128
SUBLANES = 8
VMEM_LIMIT = 56 * 1024 * 1024

MLA_HEADS = 8
NOPE_DIM = 64
ROPE_DIM = 32
HALF_ROPE = ROPE_DIM // 2
QK_DIM = NOPE_DIM + ROPE_DIM
V_DIM = 64
Q_LORA = 512
KV_LORA = 256
ROPE_THETA = 10000.0
HG_HEADS = 4
HG_DIM = 128
HG_WIDTH = HG_HEADS * HG_DIM
CONV_W = 3
EPS = 1e-6
HEAD_PAD = LANES
FF_BLOCK = 2 * LANES
LOG2E = float(np.log2(np.e))
NEG_BIG = -1e30


def _dot(a, b):
    return jnp.dot(a, b, preferred_element_type=F32)


def _dot_nt(a, b):
    return lax.dot_general(a, b, (((1,), (1,)), ((), ())), preferred_element_type=F32)


def _dot_tn(a, b):
    return lax.dot_general(a, b, (((0,), (0,)), ((), ())), preferred_element_type=F32)


def _twice_sigmoid_of_twice(h):
    return 1.0 + jnp.tanh(h)


def _rms(x, width):
    return lax.rsqrt(jnp.sum(x * x, axis=-1, keepdims=True) * (1.0 / width) + EPS)


def _const_spec(shape):
    nd = len(shape)
    return pl.BlockSpec(shape, lambda *_: (0,) * nd, pipeline_mode=pl.Buffered(1))


def _params(sem):
    return pltpu.CompilerParams(dimension_semantics=sem, vmem_limit_bytes=VMEM_LIMIT)


def _ada_kernel(c_ref, w_ref, b_ref, o_ref):
    c = c_ref[...]
    w = w_ref[...]
    c_hi = c.astype(BF16)
    c_lo = (c - c_hi.astype(F32)).astype(BF16)
    w_hi = w.astype(BF16)
    w_lo = (w - w_hi.astype(F32)).astype(BF16)
    o_ref[...] = _dot(c_hi, w_hi) + _dot(c_hi, w_lo) + _dot(c_lo, w_hi) + b_ref[...]


def _ada_mod(c, w_ada, b_ada, layer):
    bsz, d = c.shape
    depth, _, n = w_ada.shape
    bn = 1024
    return pl.pallas_call(
        _ada_kernel,
        grid=(n // bn,),
        in_specs=[pl.BlockSpec((bsz, d), lambda j: (0, 0)),
                  pl.BlockSpec((None, d, bn), lambda j: (layer, 0, j)),
                  pl.BlockSpec((None, 1, bn), lambda j: (layer, 0, j))],
        out_specs=pl.BlockSpec((bsz, bn), lambda j: (0, j)),
        out_shape=jax.ShapeDtypeStruct((bsz, n), F32),
        compiler_params=_params(("arbitrary",)),
        name="ada_mod",
    )(c, w_ada, b_ada.reshape(depth, 1, n))


def _rope(t, cos_t, sin_lo, sin_hi):
    return (t * cos_t + pltpu.roll(t, LANES - HALF_ROPE, 1) * sin_lo
            + pltpu.roll(t, HALF_ROPE, 1) * sin_hi)


def _mixer_in_kernel(x_ref, pos_ref, invf_ref, mod_ref, g1_ref, wcq_ref, wckv_ref, wh_ref, wg_ref,
                     gqa_ref, wuq_ref, gkva_ref, wk_ref, wv_ref, gq_ref, gk_ref,
                     q_ref, k_ref, vt_ref, hq_ref, hf_ref, hi_ref, hg_ref, ga_ref, gb_ref):
    tm, d = x_ref.shape
    x = x_ref[...]
    h = (x * _rms(x, d)) * (g1_ref[...] * (1.0 + mod_ref[1:2, :])) + mod_ref[0:1, :]
    hb = h.astype(BF16)

    ang = invf_ref[...] * pos_ref[...].astype(F32)
    cos_h = jnp.cos(ang)
    sin_h = jnp.sin(ang)
    zeros = lambda r: jnp.zeros((r, tm), F32)
    cos_t = jnp.concatenate([jnp.ones((NOPE_DIM, tm), F32), cos_h, cos_h,
                             zeros(HEAD_PAD - QK_DIM)], axis=0).T
    sin_lo = jnp.concatenate([zeros(NOPE_DIM), -sin_h, zeros(HEAD_PAD - NOPE_DIM - HALF_ROPE)],
                             axis=0).T
    sin_hi = jnp.concatenate([zeros(NOPE_DIM + HALF_ROPE), sin_h, zeros(HEAD_PAD - QK_DIM)],
                             axis=0).T

    cq = _dot(hb, wcq_ref[...])
    ckv = _dot(hb, wckv_ref[...])
    hh = _dot(hb, wh_ref[...])
    hq_ref[...] = hh[:, 0 * HG_WIDTH:1 * HG_WIDTH].astype(BF16)
    hf_ref[...] = hh[:, 1 * HG_WIDTH:2 * HG_WIDTH]
    hi_ref[...] = hh[:, 2 * HG_WIDTH:3 * HG_WIDTH].astype(BF16)
    hg_ref[...] = hh[:, 3 * HG_WIDTH:4 * HG_WIDTH].astype(BF16)

    cqn = (cq * _rms(cq, Q_LORA) * gqa_ref[...]).astype(BF16)
    ckv_c = ckv[:, :KV_LORA]
    kr = ckv[:, KV_LORA:]
    ckvn = (ckv_c * _rms(ckv_c, KV_LORA) * gkva_ref[...]).astype(BF16)
    qf = _dot(cqn, wuq_ref[...])
    kf = _dot(ckvn, wk_ref[...])
    vt = _dot_nt(wv_ref[...], ckvn)
    gg = _dot(hb, wg_ref[...])
    ga_ref[...] = gg[:, :d].astype(BF16)
    gb_ref[...] = gg[:, d:].astype(BF16)

    gq = gq_ref[...]
    for hd in range(MLA_HEADS):
        sl = slice(hd * HEAD_PAD, (hd + 1) * HEAD_PAD)
        qh = qf[:, sl]
        qn = qh * _rms(qh, QK_DIM) * gq
        q_ref[:, sl] = (_rope(qn, cos_t, sin_lo, sin_hi) * (LOG2E / float(np.sqrt(QK_DIM)))).astype(BF16)

    gk = gk_ref[...]
    ss_rope = jnp.sum(kr * kr, axis=-1, keepdims=True)
    kr_rot = _rope(kr * gk, cos_t, sin_lo, sin_hi)
    for hd in range(MLA_HEADS):
        sl = slice(hd * HEAD_PAD, (hd + 1) * HEAD_PAD)
        kn = kf[:, sl]
        ss = jnp.sum(kn * kn, axis=-1, keepdims=True) + ss_rope
        k_ref[:, sl] = ((kn * gk + kr_rot) * lax.rsqrt(ss * (1.0 / QK_DIM) + EPS)).astype(BF16)
    first_sub = lax.broadcasted_iota(jnp.int32, (SUBLANES, tm), 0) == 0
    pieces = []
    for hd in range(MLA_HEADS):
        r0 = hd * HEAD_PAD
        pieces += [vt[r0:r0 + V_DIM],
                   jnp.where(first_sub, 1.0, vt[r0 + V_DIM:r0 + V_DIM + SUBLANES]),
                   vt[r0 + V_DIM + SUBLANES:r0 + HEAD_PAD]]
    vt_ref[...] = jnp.concatenate(pieces, axis=0).astype(BF16)


def _mixer_in(x2, pos, invf, mod, g1, wcq, wckv, wh, wg, gqa, wuq, gkva, wk, wv, gq, gk, *,
              seq, tm, tk):
    t, d = x2.shape
    tps = seq // tm
    per_kv = tk // tm
    hw = MLA_HEADS * HEAD_PAD
    row = lambda w: pl.BlockSpec((tm, w), lambda i: (i, 0))
    vt_spec = pl.BlockSpec((None, None, hw, tm),
                           lambda i: (i // tps, (i % tps) // per_kv, 0, (i % tps) % per_kv))
    consts = [g1, wcq, wckv, wh, wg, gqa, wuq, gkva, wk, wv, gq, gk]
    out_shapes = [jax.ShapeDtypeStruct((t, hw), BF16)] * 2 + [
        jax.ShapeDtypeStruct((t // seq, seq // tk, hw, tk), BF16)] + [
        jax.ShapeDtypeStruct((t, HG_WIDTH), BF16), jax.ShapeDtypeStruct((t, HG_WIDTH), F32),
        jax.ShapeDtypeStruct((t, HG_WIDTH), BF16), jax.ShapeDtypeStruct((t, HG_WIDTH), BF16),
        jax.ShapeDtypeStruct((t, d), BF16), jax.ShapeDtypeStruct((t, d), BF16)]
    return pl.pallas_call(
        _mixer_in_kernel,
        grid=(t // tm,),
        in_specs=[row(d),
                  pl.BlockSpec((None, 1, tm), lambda i: (i, 0, 0)),
                  _const_spec(invf.shape),
                  pl.BlockSpec((None, 6, d), lambda i: (i // tps, 0, 0))]
                 + [_const_spec(a.shape) for a in consts],
        out_specs=[row(hw)] * 2 + [vt_spec] + [row(HG_WIDTH)] * 4 + [row(d)] * 2,
        out_shape=out_shapes,
        compiler_params=_params(("arbitrary",)),
        name="mixer_in",
    )(x2, pos, invf, mod, *consts)


def _attn_kernel(q_ref, k_ref, vt_ref, o_ref, m_scr, acc_scr, *, tq, tk, heads):
    i = pl.program_id(2)
    nsub = tq // tk
    head_slices = [slice(hd * HEAD_PAD, (hd + 1) * HEAD_PAD) for hd in range(heads)]
    m_scr[...] = jnp.full(m_scr.shape, NEG_BIG, F32)
    acc_scr[...] = jnp.zeros(acc_scr.shape, F32)

    def block(j, row0, diagonal):
        start = pl.multiple_of(j * tk, tk)
        rows = slice(row0, tq)
        nq = tq - row0
        stack = lambda a, n: jnp.concatenate([a] * (n // SUBLANES), axis=0)
        if diagonal:
            kv_i = lax.broadcasted_iota(jnp.int32, (tk, nq), 0)
            q_i = lax.broadcasted_iota(jnp.int32, (tk, nq), 1)

        def scores(hd):
            sl = head_slices[hd]
            s = _dot_nt(k_ref[pl.ds(start, tk), sl], q_ref[rows, sl])
            return jnp.where(kv_i <= q_i, s, NEG_BIG) if diagonal else s

        def update(hd, s):
            m_prev = m_scr[hd, :, rows]
            m_new = jnp.maximum(m_prev, jnp.max(s, axis=0, keepdims=True))
            p = jnp.exp2(s - stack(m_new, tk)).astype(BF16)
            acc_scr[hd, :, rows] = (acc_scr[hd, :, rows] * stack(jnp.exp2(m_prev - m_new), HEAD_PAD)
                                    + _dot(vt_ref[j, head_slices[hd], :], p))
            m_scr[hd, :, rows] = m_new

        ahead = 2
        pending = [scores(hd) for hd in range(min(ahead, heads))]
        for hd in range(heads):
            if hd + ahead < heads:
                pending.append(scores(hd + ahead))
            update(hd, pending.pop(0))

    def body(j, carry):
        for r in range(nsub):
            block(j * nsub + r, 0, False)
        return carry

    lax.fori_loop(0, i, body, 0)
    for r in range(nsub):
        block(i * nsub + r, r * tk, True)
    outs = []
    for hd in range(heads):
        acc = acc_scr[hd]
        outs.append((acc / acc[V_DIM:V_DIM + 1, :]).T[:, :V_DIM])
    o_ref[...] = jnp.concatenate(outs, axis=-1).astype(o_ref.dtype)


def _mla_attention(q, k, vt, *, tq, tk, heads):
    bsz, seq, hw = q.shape
    groups = MLA_HEADS // heads
    pw = heads * HEAD_PAD
    return pl.pallas_call(
        functools.partial(_attn_kernel, tq=tq, tk=tk, heads=heads),
        grid=(bsz, groups, seq // tq),
        in_specs=[pl.BlockSpec((None, tq, pw), lambda b, h, i: (b, i, h)),
                  pl.BlockSpec((None, seq, pw), lambda b, h, i: (b, 0, h)),
                  pl.BlockSpec((None, seq // tk, pw, tk), lambda b, h, i: (b, 0, h, 0))],
        out_specs=pl.BlockSpec((None, tq, heads * V_DIM), lambda b, h, i: (b, i, h)),
        out_shape=jax.ShapeDtypeStruct((bsz, seq, MLA_HEADS * V_DIM), BF16),
        scratch_shapes=[pltpu.VMEM((heads, SUBLANES, tq), F32),
                        pltpu.VMEM((heads, HEAD_PAD, tq), F32)],
        compiler_params=_params(("arbitrary", "arbitrary", "arbitrary")),
        name="mla_attn",
    )(q, k, vt)


def _hgrn_sum_masks(c):
    t = np.arange(c)[:, None]
    s = np.arange(c)[None, :]
    blocks = [s <= t, s > t]
    blk = c // 2
    while blk >= 2:
        bnd = (t & ~(2 * blk - 1)) + blk - 1
        blocks.append(((s > bnd) & (s <= t)) | ((s > t) & (s <= bnd)))
        blk //= 2
    em = np.concatenate(blocks, axis=0).astype(np.float32)
    return np.concatenate([em, em], axis=1)


def _hgrn_kernel(hq_ref, hf_ref, hi_ref, hg_ref, lbt_ref, gout_ref, em_ref, o_ref, st_ref, *,
                 layer, chunk):
    @pl.when(pl.program_id(1) == 0)
    def _():
        st_ref[...] = jnp.zeros_like(st_ref)

    for sub in range(hq_ref.shape[0] // chunk):
        rs = pl.ds(sub * chunk, chunk)
        _hgrn_chunk(hq_ref.at[rs], hf_ref.at[rs], hi_ref.at[rs], hg_ref.at[rs], lbt_ref, gout_ref,
                    em_ref, o_ref.at[rs], st_ref, layer=layer)


def _hgrn_chunk(hq_ref, hf_ref, hi_ref, hg_ref, lbt_ref, gout_ref, em_ref, o_ref, st_ref, *, layer):
    c, w = hq_ref.shape
    tab = lbt_ref[...]
    e = jnp.exp(tab - jnp.max(tab, axis=0, keepdims=True))
    lb = jnp.sum(e[1:layer + 2], axis=0, keepdims=True) / jnp.sum(e, axis=0, keepdims=True)

    hq = hq_ref[...].astype(F32)
    q = hq * _twice_sigmoid_of_twice(hq)
    f = lb + (0.5 - 0.5 * lb) * _twice_sigmoid_of_twice(hf_ref[...])
    lf = jnp.log(f)
    kk = 1.0 - f
    v = hi_ref[...]

    lf_hi = lf.astype(BF16)
    lf_mid = (lf - lf_hi.astype(F32)).astype(BF16)
    sums = _dot(em_ref[...], jnp.concatenate([lf_hi, lf_mid], axis=0))
    b = sums[0:c]
    q_dec = (q * jnp.exp(b)).astype(BF16)
    k_last = (kk * jnp.exp(sums[c:2 * c])).astype(BF16)
    b_last = b[c - 1:c, :]

    n_levels = int(np.log2(c))
    sub = lax.broadcasted_iota(jnp.int32, (SUBLANES, w), 0)
    xs = []
    for lvl in range(n_levels):
        blk = 1 << lvl
        if blk >= SUBLANES:
            base = jnp.concatenate([(q if (j % 2) else kk)[j * blk:(j + 1) * blk]
                                    for j in range(c // blk)], axis=0)
        else:
            odd = jnp.concatenate([(sub & blk) != 0] * (c // SUBLANES), axis=0)
            base = jnp.where(odd, q * f, kk) if blk == 1 else jnp.where(odd, q, kk)
        if blk > 1:
            row0 = (2 + n_levels - 1 - lvl) * c
            base = base * jnp.exp(sums[row0:row0 + c])
        xs.append(base.astype(BF16))

    t_i = lax.broadcasted_iota(jnp.int32, (c, c), 0)
    s_i = lax.broadcasted_iota(jnp.int32, (c, c), 1)
    lower_xor = jnp.where(t_i > s_i, t_i ^ s_i, 0)
    level_of = [(lower_xor >> lvl) == 1 for lvl in range(n_levels)]
    on_diag = t_i == s_i
    qk = q * kk
    gout = gout_ref[...]
    head_slices = [slice(hd * HG_DIM, (hd + 1) * HG_DIM) for hd in range(HG_HEADS)]
    attns = [jnp.where(on_diag, jnp.sum(qk[:, hs], axis=-1, keepdims=True), 0.0)
             for hs in head_slices]
    for x, keep in zip(xs, level_of):
        attns = [jnp.where(keep, _dot_nt(x[:, hs], x[:, hs]), a) for hs, a in zip(head_slices, attns)]
    for hd, hs in enumerate(head_slices):
        attn = attns[hd]
        st = st_ref[hd]
        o = _dot_nt(q_dec[:, hs], st.astype(BF16)) + _dot(attn.astype(BF16), v[:, hs])
        hg = hg_ref[:, hs].astype(F32)
        o = o * _rms(o, HG_DIM) * gout * (hg * _twice_sigmoid_of_twice(hg))
        o_ref[:, hs] = o.astype(o_ref.dtype)
        st_ref[hd] = st * jnp.exp(b_last[:, hs]) + _dot_tn(v[:, hs], k_last[:, hs])


def _hgrn2(hq, hf, hi, hg, lb_table, gout, *, chunk, chunks_per_step, layer):
    bsz, seq, w = hq.shape
    rows = chunk * chunks_per_step
    blk = pl.BlockSpec((None, rows, w), lambda b, c: (b, c, 0))
    em = jnp.asarray(_hgrn_sum_masks(chunk), dtype=BF16)
    return pl.pallas_call(
        functools.partial(_hgrn_kernel, layer=layer, chunk=chunk),
        grid=(bsz, seq // rows),
        in_specs=[blk, blk, blk, blk, _const_spec(lb_table.shape), _const_spec(gout.shape),
                  _const_spec(em.shape)],
        out_specs=blk,
        out_shape=jax.ShapeDtypeStruct((bsz, seq, w), BF16),
        scratch_shapes=[pltpu.VMEM((HG_HEADS, HG_DIM, HG_DIM), F32)],
        compiler_params=_params(("arbitrary", "arbitrary")),
        name="hgrn2",
    )(hq, hf, hi, hg, lb_table, gout, em)


def _merge_ffn_kernel(a_ref, o_ref, ga_ref, gb_ref, x_ref, modn_ref, modp_ref, wa_ref, wb_ref, wo_ref,
                      g2_ref, wup_ref, cw_ref, cb_ref, wdn_ref, out_ref, x1_scr, h2_scr, up_scr, *, tps):
    tm, d = x_ref.shape
    dff = wdn_ref.shape[0]
    halo = SUBLANES
    i = pl.program_id(0)
    nxt = i % 2
    prv = 1 - nxt

    @pl.when(i == 0)
    def _():
        rows = 2 * SUBLANES

        def clear(r, carry):
            start = pl.multiple_of(r * rows, rows)
            x1_scr[prv, pl.ds(start, rows), :] = jnp.zeros((rows, d), F32)
            h2_scr[prv, pl.ds(start, rows), :] = jnp.zeros((rows, d), BF16)
            return carry

        lax.fori_loop(0, tm // rows, clear, 0)

    @pl.when(jnp.logical_or(i == 0, (i + tps - 1) % tps == 0))
    def _():
        up_scr[0:halo, :] = jnp.zeros((halo, up_scr.shape[1]), F32)

    ya = _dot(a_ref[...], wa_ref[...])
    yb = _dot(o_ref[...], wb_ref[...])
    up = _dot(h2_scr[prv], wup_ref[...])

    up_scr[halo:halo + tm, :] = up
    y = cb_ref[...] + cw_ref[CONV_W - 1:CONV_W, :] * up
    for j in range(CONV_W - 1):
        back = CONV_W - 1 - j
        y = y + cw_ref[j:j + 1, :] * up_scr[halo - back:halo - back + tm, :]
    up_scr[0:halo, :] = up[tm - halo:, :]
    acts = []
    for blk in range(dff // FF_BLOCK):
        gate = y[:, 2 * blk * FF_BLOCK:(2 * blk + 1) * FF_BLOCK]
        val = y[:, (2 * blk + 1) * FF_BLOCK:(2 * blk + 2) * FF_BLOCK]
        acts.append((gate * val * _twice_sigmoid_of_twice(gate)).astype(BF16))
    act = jnp.concatenate(acts, axis=1)
    out_ref[...] = x1_scr[prv] + modp_ref[5:6, :] * _dot(act, wdn_ref[...])

    merged = (_twice_sigmoid_of_twice(ga_ref[...].astype(F32)) * ya
              + _twice_sigmoid_of_twice(gb_ref[...].astype(F32)) * yb)
    z1 = _dot(merged.astype(BF16), wo_ref[...])
    x1 = x_ref[...] + modn_ref[2:3, :] * z1
    x1_scr[nxt] = x1
    h2 = (x1 * _rms(x1, d)) * (g2_ref[...] * (1.0 + modn_ref[4:5, :])) + modn_ref[3:4, :]
    h2_scr[nxt] = h2.astype(BF16)


def _merge_ffn(attn, hgo, ga, gb, x2, mod, wa, wb, wo, g2, wup, cw, cb, wdn, *, seq, tm):
    t, d = x2.shape
    tps = seq // tm
    n = t // tm
    cur = lambda i: jnp.minimum(i, n - 1)
    prev = lambda i: jnp.maximum(i - 1, 0)
    row = lambda w: pl.BlockSpec((tm, w), lambda i: (cur(i), 0))
    consts = [wa, wb, wo, g2, wup, cw, cb, wdn]
    return pl.pallas_call(
        functools.partial(_merge_ffn_kernel, tps=tps),
        grid=(n + 1,),
        in_specs=[row(attn.shape[1]), row(hgo.shape[1]), row(d), row(d), row(d),
                  pl.BlockSpec((None, 6, d), lambda i: (cur(i) // tps, 0, 0)),
                  pl.BlockSpec((None, 6, d), lambda i: (prev(i) // tps, 0, 0))]
                 + [_const_spec(a.shape) for a in consts],
        out_specs=pl.BlockSpec((tm, d), lambda i: (prev(i), 0)),
        out_shape=jax.ShapeDtypeStruct((t, d), F32),
        scratch_shapes=[pltpu.VMEM((2, tm, d), F32), pltpu.VMEM((2, tm, d), BF16),
                        pltpu.VMEM((tm + SUBLANES, wup.shape[1]), F32)],
        compiler_params=_params(("arbitrary",)),
        name="merge_ffn",
    )(attn, hgo, ga, gb, x2, mod, mod, *consts)


def _pad_heads(w, used, n_heads):
    rows = w.shape[0]
    w = w.reshape(rows, n_heads, used)
    w = jnp.pad(w, ((0, 0), (0, 0), (0, HEAD_PAD - used)))
    return w.reshape(rows, n_heads * HEAD_PAD)


def _layer(x2, pos, invf, mod, p, *, bsz, seq, layer):
    t, d = x2.shape
    w_in = p["w_in"]
    o_ckv = Q_LORA
    o_h = o_ckv + KV_LORA + ROPE_DIM
    o_g = o_h + 4 * HG_WIDTH
    wcq = w_in[:, :o_ckv].astype(BF16)
    wckv = jnp.concatenate([w_in[:, o_ckv:o_ckv + KV_LORA],
                            jnp.zeros((d, NOPE_DIM), F32),
                            w_in[:, o_ckv + KV_LORA:o_h],
                            jnp.zeros((d, HEAD_PAD - QK_DIM), F32)], axis=1).astype(BF16)
    halve_h = jnp.asarray(np.repeat(np.float32([0.5, 0.5, 1.0, 0.5]), HG_WIDTH))
    wh = (w_in[:, o_h:o_g] * halve_h).astype(BF16)
    wg = (w_in[:, o_g:] * 0.5).astype(BF16)
    wuq = _pad_heads(p["w_uq"], QK_DIM, MLA_HEADS).astype(BF16)
    wukv = p["w_ukv"].reshape(KV_LORA, MLA_HEADS, NOPE_DIM + V_DIM)
    wk = _pad_heads(wukv[:, :, :NOPE_DIM].reshape(KV_LORA, -1), NOPE_DIM, MLA_HEADS).astype(BF16)
    wvt = _pad_heads(wukv[:, :, NOPE_DIM:].reshape(KV_LORA, -1), V_DIM, MLA_HEADS).T.astype(BF16)
    pad_g = lambda g: jnp.pad(g, (0, HEAD_PAD - QK_DIM)).reshape(1, HEAD_PAD)

    tm = min(256, seq)
    tm_in = min(256, seq)
    tq, tk = min(1024, seq), min(512, seq)
    q, k, vt, hq, hf, hi, hg, ga, gb = _mixer_in(
        x2, pos.reshape(t // tm_in, 1, tm_in), invf, mod, p["norm1_g"].reshape(1, d), wcq, wckv, wh,
        wg, p["q_a_norm_g"].reshape(1, -1), wuq, p["kv_a_norm_g"].reshape(1, -1), wk, wvt,
        pad_g(p["q_norm_g"]), pad_g(p["k_norm_g"]), seq=seq, tm=tm_in, tk=tk)

    hw = MLA_HEADS * HEAD_PAD
    attn = _mla_attention(q.reshape(bsz, seq, hw), k.reshape(bsz, seq, hw), vt,
                          tq=tq, tk=tk, heads=4)
    r3 = lambda a: a.reshape(bsz, seq, HG_WIDTH)
    hgo = _hgrn2(r3(hq), r3(hf), r3(hi), r3(hg), p["hg_lower_bound"],
                 p["hg_out_norm_g"].reshape(1, HG_DIM), chunk=128, chunks_per_step=4, layer=layer)

    dff = p["w_down"].shape[0]

    def interleave(a):
        lead = a.shape[:-1]
        a = a.reshape(*lead, 2, dff // FF_BLOCK, FF_BLOCK)
        return jnp.swapaxes(a, -3, -2).reshape(*lead, 2 * dff)

    halve_gate = jnp.concatenate([jnp.full((dff,), 0.5, F32), jnp.ones((dff,), F32)])
    return _merge_ffn(attn.reshape(t, -1), hgo.reshape(t, -1), ga, gb, x2, mod,
                      p["w_branch_a"].astype(BF16), p["w_branch_b"].astype(BF16),
                      (p["w_out"] * 0.5).astype(BF16), p["norm2_g"].reshape(1, d),
                      interleave(p["w_up"]).astype(BF16), interleave(p["conv_w"] * halve_gate),
                      interleave(p["conv_b"] * halve_gate).reshape(1, -1),
                      p["w_down"].astype(BF16), seq=seq, tm=tm)


def kernel(x, c, positions, w_ada, b_ada, norm1_g, w_in, q_a_norm_g, w_uq, kv_a_norm_g, w_ukv,
           q_norm_g, k_norm_g, hg_lower_bound, hg_out_norm_g, w_branch_a, w_branch_b, w_out,
           norm2_g, w_up, conv_w, conv_b, w_down):
    bsz, seq, d = x.shape
    depth = w_ada.shape[0]
    invf = (ROPE_THETA ** (-jnp.arange(0, ROPE_DIM, 2, dtype=F32) / ROPE_DIM)).reshape(HALF_ROPE, 1)
    x2 = x.reshape(bsz * seq, d)
    pos = positions.reshape(bsz * seq)
    for l in range(depth):
        mod = _ada_mod(c, w_ada, b_ada, l).reshape(bsz, 6, d)
        p = dict(norm1_g=norm1_g[l], w_in=w_in[l], q_a_norm_g=q_a_norm_g[l], w_uq=w_uq[l],
                 kv_a_norm_g=kv_a_norm_g[l], w_ukv=w_ukv[l], q_norm_g=q_norm_g[l],
                 k_norm_g=k_norm_g[l], hg_lower_bound=hg_lower_bound, hg_out_norm_g=hg_out_norm_g[l],
                 w_branch_a=w_branch_a[l], w_branch_b=w_branch_b[l], w_out=w_out[l],
                 norm2_g=norm2_g[l], w_up=w_up[l], conv_w=conv_w[l], conv_b=conv_b[l],
                 w_down=w_down[l])
        x2 = _layer(x2, pos, invf, mod, p, bsz=bsz, seq=seq, layer=l)
    return x2.reshape(bsz, seq, d)
```

```python
import functools

import numpy as np
import jax
import jax.numpy as jnp
from jax import lax
from jax.experimental import pallas as pl
from jax.experimental.pallas import tpu as pltpu

F32 = jnp.float32
BF16 = jnp.bfloat16

LANES = 128
SUBLANES = 8
VMEM_LIMIT = 56 * 1024 * 1024

MLA_HEADS = 8
NOPE_DIM = 64
ROPE_DIM = 32
HALF_ROPE = ROPE_DIM // 2
QK_DIM = NOPE_DIM + ROPE_DIM
V_DIM = 64
Q_LORA = 512
KV_LORA = 256
ROPE_THETA = 10000.0
HG_HEADS = 4
HG_DIM = 128
HG_WIDTH = HG_HEADS * HG_DIM
CONV_W = 3
EPS = 1e-6
HEAD_PAD = LANES
FF_BLOCK = 2 * LANES
LOG2E = float(np.log2(np.e))
NEG_BIG = -1e30


def _dot(a, b):
    return jnp.dot(a, b, preferred_element_type=F32)


def _dot_nt(a, b):
    return lax.dot_general(a, b, (((1,), (1,)), ((), ())), preferred_element_type=F32)


def _dot_tn(a, b):
    return lax.dot_general(a, b, (((0,), (0,)), ((), ())), preferred_element_type=F32)


def _twice_sigmoid_of_twice(h):
    return 1.0 + jnp.tanh(h)


def _rms(x, width):
    return lax.rsqrt(jnp.sum(x * x, axis=-1, keepdims=True) * (1.0 / width) + EPS)


def _const_spec(shape):
    nd = len(shape)
    return pl.BlockSpec(shape, lambda *_: (0,) * nd, pipeline_mode=pl.Buffered(1))


def _params(sem):
    return pltpu.CompilerParams(dimension_semantics=sem, vmem_limit_bytes=VMEM_LIMIT)


def _ada_kernel(c_ref, w_ref, b_ref, o_ref):
    c = c_ref[...]
    w = w_ref[...]
    c_hi = c.astype(BF16)
    c_lo = (c - c_hi.astype(F32)).astype(BF16)
    w_hi = w.astype(BF16)
    w_lo = (w - w_hi.astype(F32)).astype(BF16)
    o_ref[...] = _dot(c_hi, w_hi) + _dot(c_hi, w_lo) + _dot(c_lo, w_hi) + b_ref[...]


def _ada_mod(c, w_ada, b_ada, layer):
    bsz, d = c.shape
    depth, _, n = w_ada.shape
    bn = 1024
    return pl.pallas_call(
        _ada_kernel,
        grid=(n // bn,),
        in_specs=[pl.BlockSpec((bsz, d), lambda j: (0, 0)),
                  pl.BlockSpec((None, d, bn), lambda j: (layer, 0, j)),
                  pl.BlockSpec((None, 1, bn), lambda j: (layer, 0, j))],
        out_specs=pl.BlockSpec((bsz, bn), lambda j: (0, j)),
        out_shape=jax.ShapeDtypeStruct((bsz, n), F32),
        compiler_params=_params(("arbitrary",)),
        name="ada_mod",
    )(c, w_ada, b_ada.reshape(depth, 1, n))


def _rope(t, cos_t, sin_lo, sin_hi):
    return (t * cos_t + pltpu.roll(t, LANES - HALF_ROPE, 1) * sin_lo
            + pltpu.roll(t, HALF_ROPE, 1) * sin_hi)


def _mixer_in_kernel(x_ref, pos_ref, invf_ref, mod_ref, g1_ref, wcq_ref, wckv_ref, wh_ref, wg_ref,
                     gqa_ref, wuq_ref, gkva_ref, wk_ref, wv_ref, gq_ref, gk_ref,
                     q_ref, k_ref, vt_ref, hq_ref, hf_ref, hi_ref, hg_ref, ga_ref, gb_ref):
    tm, d = x_ref.shape
    x = x_ref[...]
    h = (x * _rms(x, d)) * (g1_ref[...] * (1.0 + mod_ref[1:2, :])) + mod_ref[0:1, :]
    hb = h.astype(BF16)

    ang = invf_ref[...] * pos_ref[...].astype(F32)
    cos_h = jnp.cos(ang)
    sin_h = jnp.sin(ang)
    zeros = lambda r: jnp.zeros((r, tm), F32)
    cos_t = jnp.concatenate([jnp.ones((NOPE_DIM, tm), F32), cos_h, cos_h,
                             zeros(HEAD_PAD - QK_DIM)], axis=0).T
    sin_lo = jnp.concatenate([zeros(NOPE_DIM), -sin_h, zeros(HEAD_PAD - NOPE_DIM - HALF_ROPE)],
                             axis=0).T
    sin_hi = jnp.concatenate([zeros(NOPE_DIM + HALF_ROPE), sin_h, zeros(HEAD_PAD - QK_DIM)],
                             axis=0).T

    cq = _dot(hb, wcq_ref[...])
    ckv = _dot(hb, wckv_ref[...])
    hh = _dot(hb, wh_ref[...])
    hq_ref[...] = (0.5 * hh[:, 0 * HG_WIDTH:1 * HG_WIDTH]).astype(BF16)
    hf_ref[...] = 0.5 * hh[:, 1 * HG_WIDTH:2 * HG_WIDTH]
    hi_ref[...] = hh[:, 2 * HG_WIDTH:3 * HG_WIDTH].astype(BF16)
    hg_ref[...] = (0.5 * hh[:, 3 * HG_WIDTH:4 * HG_WIDTH]).astype(BF16)

    cqn = (cq * _rms(cq, Q_LORA) * gqa_ref[...]).astype(BF16)
    ckv_c = ckv[:, :KV_LORA]
    kr = ckv[:, KV_LORA:]
    ckvn = (ckv_c * _rms(ckv_c, KV_LORA) * gkva_ref[...]).astype(BF16)
    qf = _dot(cqn, wuq_ref[...])
    kf = _dot(ckvn, wk_ref[...])
    vt = _dot_nt(wv_ref[...], ckvn)
    gg = _dot(hb, wg_ref[...])
    ga_ref[...] = (0.5 * gg[:, :d]).astype(BF16)
    gb_ref[...] = (0.5 * gg[:, d:]).astype(BF16)

    gq = gq_ref[...]
    for hd in range(MLA_HEADS):
        sl = slice(hd * HEAD_PAD, (hd + 1) * HEAD_PAD)
        qh = qf[:, sl]
        qn = qh * _rms(qh, QK_DIM) * gq
        q_ref[:, sl] = (_rope(qn, cos_t, sin_lo, sin_hi) * (LOG2E / float(np.sqrt(QK_DIM)))).astype(BF16)

    gk = gk_ref[...]
    ss_rope = jnp.sum(kr * kr, axis=-1, keepdims=True)
    kr_rot = _rope(kr * gk, cos_t, sin_lo, sin_hi)
    for hd in range(MLA_HEADS):
        sl = slice(hd * HEAD_PAD, (hd + 1) * HEAD_PAD)
        kn = kf[:, sl]
        ss = jnp.sum(kn * kn, axis=-1, keepdims=True) + ss_rope
        k_ref[:, sl] = ((kn * gk + kr_rot) * lax.rsqrt(ss * (1.0 / QK_DIM) + EPS)).astype(BF16)
    first_sub = lax.broadcasted_iota(jnp.int32, (SUBLANES, tm), 0) == 0
    pieces = []
    for hd in range(MLA_HEADS):
        r0 = hd * HEAD_PAD
        pieces += [vt[r0:r0 + V_DIM],
                   jnp.where(first_sub, 1.0, vt[r0 + V_DIM:r0 + V_DIM + SUBLANES]),
                   vt[r0 + V_DIM + SUBLANES:r0 + HEAD_PAD]]
    vt_ref[...] = jnp.concatenate(pieces, axis=0).astype(BF16)


def _mixer_in(x2, pos, invf, mod, g1, wcq, wckv, wh, wg, gqa, wuq, gkva, wk, wv, gq, gk, *,
              seq, tm, tk):
    t, d = x2.shape
    tps = seq // tm
    per_kv = tk // tm
    hw = MLA_HEADS * HEAD_PAD
    row = lambda w: pl.BlockSpec((tm, w), lambda i: (i, 0))
    vt_spec = pl.BlockSpec((None, None, hw, tm),
                           lambda i: (i // tps, (i % tps) // per_kv, 0, (i % tps) % per_kv))
    consts = [g1, wcq, wckv, wh, wg, gqa, wuq, gkva, wk, wv, gq, gk]
    out_shapes = [jax.ShapeDtypeStruct((t, hw), BF16)] * 2 + [
        jax.ShapeDtypeStruct((t // seq, seq // tk, hw, tk), BF16)] + [
        jax.ShapeDtypeStruct((t, HG_WIDTH), BF16), jax.ShapeDtypeStruct((t, HG_WIDTH), F32),
        jax.ShapeDtypeStruct((t, HG_WIDTH), BF16), jax.ShapeDtypeStruct((t, HG_WIDTH), BF16),
        jax.ShapeDtypeStruct((t, d), BF16), jax.ShapeDtypeStruct((t, d), BF16)]
    return pl.pallas_call(
        _mixer_in_kernel,
        grid=(t // tm,),
        in_specs=[row(d),
                  pl.BlockSpec((None, 1, tm), lambda i: (i, 0, 0)),
                  _const_spec(invf.shape),
                  pl.BlockSpec((None, 6, d), lambda i: (i // tps, 0, 0))]
                 + [_const_spec(a.shape) for a in consts],
        out_specs=[row(hw)] * 2 + [vt_spec] + [row(HG_WIDTH)] * 4 + [row(d)] * 2,
        out_shape=out_shapes,
        compiler_params=_params(("arbitrary",)),
        name="mixer_in",
    )(x2, pos, invf, mod, *consts)


def _attn_kernel(q_ref, k_ref, vt_ref, o_ref, m_scr, acc_scr, *, tq, tk, heads):
    i = pl.program_id(2)
    nsub = tq // tk
    head_slices = [slice(hd * HEAD_PAD, (hd + 1) * HEAD_PAD) for hd in range(heads)]
    m_scr[...] = jnp.full(m_scr.shape, NEG_BIG, F32)
    acc_scr[...] = jnp.zeros(acc_scr.shape, F32)

    def block(j, row0, diagonal):
        start = pl.multiple_of(j * tk, tk)
        rows = slice(row0, tq)
        nq = tq - row0
        stack = lambda a, n: jnp.concatenate([a] * (n // SUBLANES), axis=0)
        if diagonal:
            kv_i = lax.broadcasted_iota(jnp.int32, (tk, nq), 0)
            q_i = lax.broadcasted_iota(jnp.int32, (tk, nq), 1)

        def scores(hd):
            sl = head_slices[hd]
            s = _dot_nt(k_ref[pl.ds(start, tk), sl], q_ref[rows, sl])
            return jnp.where(kv_i <= q_i, s, NEG_BIG) if diagonal else s

        def update(hd, s):
            m_prev = m_scr[hd, :, rows]
            m_new = jnp.maximum(m_prev, jnp.max(s, axis=0, keepdims=True))
            p = jnp.exp2(s - stack(m_new, tk)).astype(BF16)
            acc_scr[hd, :, rows] = (acc_scr[hd, :, rows] * stack(jnp.exp2(m_prev - m_new), HEAD_PAD)
                                    + _dot(vt_ref[j, head_slices[hd], :], p))
            m_scr[hd, :, rows] = m_new

        ahead = 2
        pending = [scores(hd) for hd in range(min(ahead, heads))]
        for hd in range(heads):
            if hd + ahead < heads:
                pending.append(scores(hd + ahead))
            update(hd, pending.pop(0))

    def body(j, carry):
        for r in range(nsub):
            block(j * nsub + r, 0, False)
        return carry

    lax.fori_loop(0, i, body, 0)
    for r in range(nsub):
        block(i * nsub + r, r * tk, True)
    outs = []
    for hd in range(heads):
        acc = acc_scr[hd]
        outs.append((acc / acc[V_DIM:V_DIM + 1, :]).T[:, :V_DIM])
    o_ref[...] = jnp.concatenate(outs, axis=-1).astype(o_ref.dtype)


def _mla_attention(q, k, vt, *, tq, tk, heads):
    bsz, seq, hw = q.shape
    groups = MLA_HEADS // heads
    pw = heads * HEAD_PAD
    return pl.pallas_call(
        functools.partial(_attn_kernel, tq=tq, tk=tk, heads=heads),
        grid=(bsz, groups, seq // tq),
        in_specs=[pl.BlockSpec((None, tq, pw), lambda b, h, i: (b, i, h)),
                  pl.BlockSpec((None, seq, pw), lambda b, h, i: (b, 0, h)),
                  pl.BlockSpec((None, seq // tk, pw, tk), lambda b, h, i: (b, 0, h, 0))],
        out_specs=pl.BlockSpec((None, tq, heads * V_DIM), lambda b, h, i: (b, i, h)),
        out_shape=jax.ShapeDtypeStruct((bsz, seq, MLA_HEADS * V_DIM), BF16),
        scratch_shapes=[pltpu.VMEM((heads, SUBLANES, tq), F32),
                        pltpu.VMEM((heads, HEAD_PAD, tq), F32)],
        compiler_params=_params(("arbitrary", "arbitrary", "arbitrary")),
        name="mla_attn",
    )(q, k, vt)


def _hgrn_sum_masks(c):
    t = np.arange(c)[:, None]
    s = np.arange(c)[None, :]
    blocks = [s <= t, s > t]
    blk = c // 2
    while blk >= 2:
        bnd = (t & ~(2 * blk - 1)) + blk - 1
        blocks.append(((s > bnd) & (s <= t)) | ((s > t) & (s <= bnd)))
        blk //= 2
    em = np.concatenate(blocks, axis=0).astype(np.float32)
    return np.concatenate([em, em], axis=1)


def _hgrn_kernel(hq_ref, hf_ref, hi_ref, hg_ref, lbt_ref, gout_ref, em_ref, o_ref, st_ref, *,
                 layer, chunk):
    @pl.when(pl.program_id(1) == 0)
    def _():
        st_ref[...] = jnp.zeros_like(st_ref)

    for sub in range(hq_ref.shape[0] // chunk):
        rs = pl.ds(sub * chunk, chunk)
        _hgrn_chunk(hq_ref.at[rs], hf_ref.at[rs], hi_ref.at[rs], hg_ref.at[rs], lbt_ref, gout_ref,
                    em_ref, o_ref.at[rs], st_ref, layer=layer)


def _hgrn_chunk(hq_ref, hf_ref, hi_ref, hg_ref, lbt_ref, gout_ref, em_ref, o_ref, st_ref, *, layer):
    c, w = hq_ref.shape
    tab = lbt_ref[...]
    e = jnp.exp(tab - jnp.max(tab, axis=0, keepdims=True))
    lb = jnp.sum(e[1:layer + 2], axis=0, keepdims=True) / jnp.sum(e, axis=0, keepdims=True)

    hq = hq_ref[...].astype(F32)
    q = hq * _twice_sigmoid_of_twice(hq)
    f = lb + (0.5 - 0.5 * lb) * _twice_sigmoid_of_twice(hf_ref[...])
    lf = jnp.log(f)
    kk = 1.0 - f
    v = hi_ref[...]

    lf_hi = lf.astype(BF16)
    lf_mid = (lf - lf_hi.astype(F32)).astype(BF16)
    sums = _dot(em_ref[...], jnp.concatenate([lf_hi, lf_mid], axis=0))
    b = sums[0:c]
    q_dec = (q * jnp.exp(b)).astype(BF16)
    k_last = (kk * jnp.exp(sums[c:2 * c])).astype(BF16)
    b_last = b[c - 1:c, :]

    n_levels = int(np.log2(c))
    sub = lax.broadcasted_iota(jnp.int32, (SUBLANES, w), 0)
    xs = []
    for lvl in range(n_levels):
        blk = 1 << lvl
        if blk >= SUBLANES:
            base = jnp.concatenate([(q if (j % 2) else kk)[j * blk:(j + 1) * blk]
                                    for j in range(c // blk)], axis=0)
        else:
            odd = jnp.concatenate([(sub & blk) != 0] * (c // SUBLANES), axis=0)
            base = jnp.where(odd, q * f, kk) if blk == 1 else jnp.where(odd, q, kk)
        if blk > 1:
            row0 = (2 + n_levels - 1 - lvl) * c
            base = base * jnp.exp(sums[row0:row0 + c])
        xs.append(base.astype(BF16))

    t_i = lax.broadcasted_iota(jnp.int32, (c, c), 0)
    s_i = lax.broadcasted_iota(jnp.int32, (c, c), 1)
    lower_xor = jnp.where(t_i > s_i, t_i ^ s_i, 0)
    level_of = [(lower_xor >> lvl) == 1 for lvl in range(n_levels)]
    on_diag = t_i == s_i
    qk = q * kk
    gout = gout_ref[...]
    head_slices = [slice(hd * HG_DIM, (hd + 1) * HG_DIM) for hd in range(HG_HEADS)]
    attns = [jnp.where(on_diag, jnp.sum(qk[:, hs], axis=-1, keepdims=True), 0.0)
             for hs in head_slices]
    for x, keep in zip(xs, level_of):
        attns = [jnp.where(keep, _dot_nt(x[:, hs], x[:, hs]), a) for hs, a in zip(head_slices, attns)]
    for hd, hs in enumerate(head_slices):
        attn = attns[hd]
        st = st_ref[hd]
        o = _dot_nt(q_dec[:, hs], st.astype(BF16)) + _dot(attn.astype(BF16), v[:, hs])
        hg = hg_ref[:, hs].astype(F32)
        o = o * _rms(o, HG_DIM) * gout * (hg * _twice_sigmoid_of_twice(hg))
        o_ref[:, hs] = o.astype(o_ref.dtype)
        st_ref[hd] = st * jnp.exp(b_last[:, hs]) + _dot_tn(v[:, hs], k_last[:, hs])


def _hgrn2(hq, hf, hi, hg, lb_table, gout, *, chunk, chunks_per_step, layer):
    bsz, seq, w = hq.shape
    rows = chunk * chunks_per_step
    blk = pl.BlockSpec((None, rows, w), lambda b, c: (b, c, 0))
    em = jnp.asarray(_hgrn_sum_masks(chunk), dtype=BF16)
    return pl.pallas_call(
        functools.partial(_hgrn_kernel, layer=layer, chunk=chunk),
        grid=(bsz, seq // rows),
        in_specs=[blk, blk, blk, blk, _const_spec(lb_table.shape), _const_spec(gout.shape),
                  _const_spec(em.shape)],
        out_specs=blk,
        out_shape=jax.ShapeDtypeStruct((bsz, seq, w), BF16),
        scratch_shapes=[pltpu.VMEM((HG_HEADS, HG_DIM, HG_DIM), F32)],
        compiler_params=_params(("arbitrary", "arbitrary")),
        name="hgrn2",
    )(hq, hf, hi, hg, lb_table, gout, em)


def _merge_ffn_kernel(a_ref, o_ref, ga_ref, gb_ref, x_ref, modn_ref, modp_ref, wa_ref, wb_ref, wo_ref,
                      g2_ref, wup_ref, cw_ref, cb_ref, wdn_ref, out_ref, x1_scr, h2_scr, up_scr, *, tps):
    tm, d = x_ref.shape
    dff = wdn_ref.shape[0]
    halo = SUBLANES
    i = pl.program_id(0)
    nxt = i % 2
    prv = 1 - nxt

    @pl.when(i == 0)
    def _():
        rows = 2 * SUBLANES

        def clear(r, carry):
            start = pl.multiple_of(r * rows, rows)
            x1_scr[prv, pl.ds(start, rows), :] = jnp.zeros((rows, d), F32)
            h2_scr[prv, pl.ds(start, rows), :] = jnp.zeros((rows, d), BF16)
            return carry

        lax.fori_loop(0, tm // rows, clear, 0)

    @pl.when(jnp.logical_or(i == 0, (i + tps - 1) % tps == 0))
    def _():
        up_scr[0:halo, :] = jnp.zeros((halo, up_scr.shape[1]), F32)

    ya = _dot(a_ref[...], wa_ref[...])
    yb = _dot(o_ref[...], wb_ref[...])
    h2p = h2_scr[prv]

    def conv_block(cols):
        up = _dot(h2p, wup_ref[:, cols])
        up_scr[halo:halo + tm, cols] = up
        y = cb_ref[:, cols] + cw_ref[CONV_W - 1:CONV_W, cols] * up
        for j in range(CONV_W - 1):
            back = CONV_W - 1 - j
            y = y + cw_ref[j:j + 1, cols] * up_scr[halo - back:halo - back + tm, cols]
        up_scr[0:halo, cols] = up[tm - halo:, :]
        return y

    acts = []
    for off in range(0, dff, FF_BLOCK):
        gate = conv_block(slice(off, off + FF_BLOCK))
        val = conv_block(slice(dff + off, dff + off + FF_BLOCK))
        acts.append((gate * val * _twice_sigmoid_of_twice(gate)).astype(BF16))
    act = jnp.concatenate(acts, axis=1)
    out_ref[...] = x1_scr[prv] + modp_ref[5:6, :] * _dot(act, wdn_ref[...])

    merged = (_twice_sigmoid_of_twice(ga_ref[...].astype(F32)) * ya
              + _twice_sigmoid_of_twice(gb_ref[...].astype(F32)) * yb)
    z1 = _dot(merged.astype(BF16), wo_ref[...])
    x1 = x_ref[...] + modn_ref[2:3, :] * z1
    x1_scr[nxt] = x1
    h2 = (x1 * _rms(x1, d)) * (g2_ref[...] * (1.0 + modn_ref[4:5, :])) + modn_ref[3:4, :]
    h2_scr[nxt] = h2.astype(BF16)


def _merge_ffn(attn, hgo, ga, gb, x2, mod, wa, wb, wo, g2, wup, cw, cb, wdn, *, seq, tm):
    t, d = x2.shape
    tps = seq // tm
    n = t // tm
    cur = lambda i: jnp.minimum(i, n - 1)
    prev = lambda i: jnp.maximum(i - 1, 0)
    row = lambda w: pl.BlockSpec((tm, w), lambda i: (cur(i), 0))
    consts = [wa, wb, wo, g2, wup, cw, cb, wdn]
    return pl.pallas_call(
        functools.partial(_merge_ffn_kernel, tps=tps),
        grid=(n + 1,),
        in_specs=[row(attn.shape[1]), row(hgo.shape[1]), row(d), row(d), row(d),
                  pl.BlockSpec((None, 6, d), lambda i: (cur(i) // tps, 0, 0)),
                  pl.BlockSpec((None, 6, d), lambda i: (prev(i) // tps, 0, 0))]
                 + [_const_spec(a.shape) for a in consts],
        out_specs=pl.BlockSpec((tm, d), lambda i: (prev(i), 0)),
        out_shape=jax.ShapeDtypeStruct((t, d), F32),
        scratch_shapes=[pltpu.VMEM((2, tm, d), F32), pltpu.VMEM((2, tm, d), BF16),
                        pltpu.VMEM((tm + SUBLANES, wup.shape[1]), F32)],
        compiler_params=_params(("arbitrary",)),
        name="merge_ffn",
    )(attn, hgo, ga, gb, x2, mod, mod, *consts)


def _pad_heads(w, used, n_heads):
    rows = w.shape[0]
    w = w.reshape(rows, n_heads, used)
    w = jnp.pad(w, ((0, 0), (0, 0), (0, HEAD_PAD - used)))
    return w.reshape(rows, n_heads * HEAD_PAD)


def _layer(x2, pos, invf, mod, p, *, bsz, seq, layer):
    t, d = x2.shape
    w_in = p["w_in"]
    o_ckv = Q_LORA
    o_h = o_ckv + KV_LORA + ROPE_DIM
    o_g = o_h + 4 * HG_WIDTH
    wcq = w_in[:, :o_ckv].astype(BF16)
    wckv = jnp.concatenate([w_in[:, o_ckv:o_ckv + KV_LORA],
                            jnp.zeros((d, NOPE_DIM), F32),
                            w_in[:, o_ckv + KV_LORA:o_h],
                            jnp.zeros((d, HEAD_PAD - QK_DIM), F32)], axis=1).astype(BF16)
    wh = w_in[:, o_h:o_g].astype(BF16)
    wg = w_in[:, o_g:].astype(BF16)
    wuq = _pad_heads(p["w_uq"], QK_DIM, MLA_HEADS).astype(BF16)
    wukv = p["w_ukv"].reshape(KV_LORA, MLA_HEADS, NOPE_DIM + V_DIM)
    wk = _pad_heads(wukv[:, :, :NOPE_DIM].reshape(KV_LORA, -1), NOPE_DIM, MLA_HEADS).astype(BF16)
    wvt = _pad_heads(wukv[:, :, NOPE_DIM:].reshape(KV_LORA, -1), V_DIM, MLA_HEADS).T.astype(BF16)
    pad_g = lambda g: jnp.pad(g, (0, HEAD_PAD - QK_DIM)).reshape(1, HEAD_PAD)

    tm = min(256, seq)
    tm_in = min(256, seq)
    tq, tk = min(1024, seq), min(512, seq)
    q, k, vt, hq, hf, hi, hg, ga, gb = _mixer_in(
        x2, pos.reshape(t // tm_in, 1, tm_in), invf, mod, p["norm1_g"].reshape(1, d), wcq, wckv, wh,
        wg, p["q_a_norm_g"].reshape(1, -1), wuq, p["kv_a_norm_g"].reshape(1, -1), wk, wvt,
        pad_g(p["q_norm_g"]), pad_g(p["k_norm_g"]), seq=seq, tm=tm_in, tk=tk)

    hw = MLA_HEADS * HEAD_PAD
    attn = _mla_attention(q.reshape(bsz, seq, hw), k.reshape(bsz, seq, hw), vt,
                          tq=tq, tk=tk, heads=4)
    r3 = lambda a: a.reshape(bsz, seq, HG_WIDTH)
    hgo = _hgrn2(r3(hq), r3(hf), r3(hi), r3(hg), p["hg_lower_bound"],
                 p["hg_out_norm_g"].reshape(1, HG_DIM), chunk=128, chunks_per_step=4, layer=layer)

    dff = p["w_down"].shape[0]
    halve_gate = jnp.asarray(np.repeat(np.float32([0.5, 1.0]), dff))
    return _merge_ffn(attn.reshape(t, -1), hgo.reshape(t, -1), ga, gb, x2, mod,
                      p["w_branch_a"].astype(BF16), p["w_branch_b"].astype(BF16),
                      (p["w_out"] * 0.5).astype(BF16), p["norm2_g"].reshape(1, d),
                      p["w_up"].astype(BF16), p["conv_w"] * halve_gate,
                      (p["conv_b"] * halve_gate).reshape(1, -1),
                      p["w_down"].astype(BF16), seq=seq, tm=tm)


def kernel(x, c, positions, w_ada, b_ada, norm1_g, w_in, q_a_norm_g, w_uq, kv_a_norm_g, w_ukv,
           q_norm_g, k_norm_g, hg_lower_bound, hg_out_norm_g, w_branch_a, w_branch_b, w_out,
           norm2_g, w_up, conv_w, conv_b, w_down):
    bsz, seq, d = x.shape
    depth = w_ada.shape[0]
    invf = (ROPE_THETA ** (-jnp.arange(0, ROPE_DIM, 2, dtype=F32) / ROPE_DIM)).reshape(HALF_ROPE, 1)
    x2 = x.reshape(bsz * seq, d)
    pos = positions.reshape(bsz * seq)
    for l in range(depth):
        mod = _ada_mod(c, w_ada, b_ada, l).reshape(bsz, 6, d)
        p = dict(norm1_g=norm1_g[l], w_in=w_in[l], q_a_norm_g=q_a_norm_g[l], w_uq=w_uq[l],
                 kv_a_norm_g=kv_a_norm_g[l], w_ukv=w_ukv[l], q_norm_g=q_norm_g[l],
                 k_norm_g=k_norm_g[l], hg_lower_bound=hg_lower_bound, hg_out_norm_g=hg_out_norm_g[l],
                 w_branch_a=w_branch_a[l], w_branch_b=w_branch_b[l], w_out=w_out[l],
                 norm2_g=norm2_g[l], w_up=w_up[l], conv_w=conv_w[l], conv_b=conv_b[l],
                 w_down=w_down[l])
        x2 = _layer(x2, pos, invf, mod, p, bsz=bsz, seq=seq, layer=l)
    return x2.reshape(bsz, seq, d)
```

```python
import functools

import numpy as np
import jax
import jax.numpy as jnp
from jax import lax
from jax.experimental import pallas as pl
from jax.experimental.pallas import tpu as pltpu

F32 = jnp.float32
BF16 = jnp.bfloat16

LANES = 128
SUBLANES = 8
VMEM_LIMIT = 56 * 1024 * 1024

MLA_HEADS = 8
NOPE_DIM = 64
ROPE_DIM = 32
HALF_ROPE = ROPE_DIM // 2
QK_DIM = NOPE_DIM + ROPE_DIM
V_DIM = 64
Q_LORA = 512
KV_LORA = 256
ROPE_THETA = 10000.0
HG_HEADS = 4
HG_DIM = 128
HG_WIDTH = HG_HEADS * HG_DIM
CONV_W = 3
EPS = 1e-6
HEAD_PAD = LANES
FF_BLOCK = 2 * LANES
LOG2E = float(np.log2(np.e))
NEG_BIG = -1e30
Q_SCALE = LOG2E / float(np.sqrt(QK_DIM))
BF16_MARGIN = (1.0 + 2.0 ** -8) ** 2
SAFE_EXP2_SPAN = 80.0


def _dot(a, b):
    return jnp.dot(a, b, preferred_element_type=F32)


def _dot_nt(a, b):
    return lax.dot_general(a, b, (((1,), (1,)), ((), ())), preferred_element_type=F32)


def _dot_tn(a, b):
    return lax.dot_general(a, b, (((0,), (0,)), ((), ())), preferred_element_type=F32)


def _twice_sigmoid_of_twice(h):
    return 1.0 + jnp.tanh(h)


def _rms(x, width):
    return lax.rsqrt(jnp.sum(x * x, axis=-1, keepdims=True) * (1.0 / width) + EPS)


def _const_spec(shape):
    nd = len(shape)
    return pl.BlockSpec(shape, lambda *_: (0,) * nd, pipeline_mode=pl.Buffered(1))


def _params(sem):
    return pltpu.CompilerParams(dimension_semantics=sem, vmem_limit_bytes=VMEM_LIMIT)


def _ada_kernel(c_ref, w_ref, b_ref, o_ref):
    c = c_ref[...]
    w = w_ref[...]
    c_hi = c.astype(BF16)
    c_lo = (c - c_hi.astype(F32)).astype(BF16)
    w_hi = w.astype(BF16)
    w_lo = (w - w_hi.astype(F32)).astype(BF16)
    o_ref[...] = _dot(c_hi, w_hi) + _dot(c_hi, w_lo) + _dot(c_lo, w_hi) + b_ref[...]


def _ada_mod(c, w_ada, b_ada, layer):
    bsz, d = c.shape
    depth, _, n = w_ada.shape
    bn = 1024
    return pl.pallas_call(
        _ada_kernel,
        grid=(n // bn,),
        in_specs=[pl.BlockSpec((bsz, d), lambda j: (0, 0)),
                  pl.BlockSpec((None, d, bn), lambda j: (layer, 0, j)),
                  pl.BlockSpec((None, 1, bn), lambda j: (layer, 0, j))],
        out_specs=pl.BlockSpec((bsz, bn), lambda j: (0, j)),
        out_shape=jax.ShapeDtypeStruct((bsz, n), F32),
        compiler_params=_params(("arbitrary",)),
        name="ada_mod",
    )(c, w_ada, b_ada.reshape(depth, 1, n))


def _rope(t, cos_t, sin_lo, sin_hi):
    return (t * cos_t + pltpu.roll(t, LANES - HALF_ROPE, 1) * sin_lo
            + pltpu.roll(t, HALF_ROPE, 1) * sin_hi)


def _mixer_in_kernel(x_ref, pos_ref, invf_ref, mod_ref, g1_ref, wcq_ref, wckv_ref, wh_ref, wg_ref,
                     gqa_ref, wuq_ref, gkva_ref, wk_ref, wv_ref, gq_ref, gk_ref,
                     q_ref, k_ref, vt_ref, hq_ref, hf_ref, hi_ref, hg_ref, ga_ref, gb_ref):
    tm, d = x_ref.shape
    x = x_ref[...]
    h = (x * _rms(x, d)) * (g1_ref[...] * (1.0 + mod_ref[1:2, :])) + mod_ref[0:1, :]
    hb = h.astype(BF16)

    ang = invf_ref[...] * pos_ref[...].astype(F32)
    cos_h = jnp.cos(ang)
    sin_h = jnp.sin(ang)
    zeros = lambda r: jnp.zeros((r, tm), F32)
    cos_t = jnp.concatenate([jnp.ones((NOPE_DIM, tm), F32), cos_h, cos_h,
                             zeros(HEAD_PAD - QK_DIM)], axis=0).T
    sin_lo = jnp.concatenate([zeros(NOPE_DIM), -sin_h, zeros(HEAD_PAD - NOPE_DIM - HALF_ROPE)],
                             axis=0).T
    sin_hi = jnp.concatenate([zeros(NOPE_DIM + HALF_ROPE), sin_h, zeros(HEAD_PAD - QK_DIM)],
                             axis=0).T

    cq = _dot(hb, wcq_ref[...])
    ckv = _dot(hb, wckv_ref[...])
    hh = _dot(hb, wh_ref[:, :2 * HG_WIDTH])
    hq_ref[...] = (0.5 * hh[:, :HG_WIDTH]).astype(BF16)
    hf_ref[...] = 0.5 * hh[:, HG_WIDTH:]

    cqn = (cq * _rms(cq, Q_LORA) * gqa_ref[...]).astype(BF16)
    ckv_c = ckv[:, :KV_LORA]
    kr = ckv[:, KV_LORA:]
    ckvn = (ckv_c * _rms(ckv_c, KV_LORA) * gkva_ref[...]).astype(BF16)
    qf = _dot(cqn, wuq_ref[...])
    kf = _dot(ckvn, wk_ref[...])
    vt = _dot_nt(wv_ref[...], ckvn)
    hh = _dot(hb, wh_ref[:, 2 * HG_WIDTH:])
    hi_ref[...] = hh[:, :HG_WIDTH].astype(BF16)
    hg_ref[...] = (0.5 * hh[:, HG_WIDTH:]).astype(BF16)
    gg = _dot(hb, wg_ref[...])
    ga_ref[...] = (0.5 * gg[:, :d]).astype(BF16)
    gb_ref[...] = (0.5 * gg[:, d:]).astype(BF16)

    gq = gq_ref[...]
    for hd in range(MLA_HEADS):
        sl = slice(hd * HEAD_PAD, (hd + 1) * HEAD_PAD)
        qh = qf[:, sl]
        qn = qh * _rms(qh, QK_DIM) * gq
        q_ref[:, sl] = (_rope(qn, cos_t, sin_lo, sin_hi) * Q_SCALE).astype(BF16)

    gk = gk_ref[...]
    ss_rope = jnp.sum(kr * kr, axis=-1, keepdims=True)
    kr_rot = _rope(kr * gk, cos_t, sin_lo, sin_hi)
    for hd in range(MLA_HEADS):
        sl = slice(hd * HEAD_PAD, (hd + 1) * HEAD_PAD)
        kn = kf[:, sl]
        ss = jnp.sum(kn * kn, axis=-1, keepdims=True) + ss_rope
        k_ref[:, sl] = ((kn * gk + kr_rot) * lax.rsqrt(ss * (1.0 / QK_DIM) + EPS)).astype(BF16)
    first_sub = lax.broadcasted_iota(jnp.int32, (SUBLANES, tm), 0) == 0
    pieces = []
    for hd in range(MLA_HEADS):
        r0 = hd * HEAD_PAD
        pieces += [vt[r0:r0 + V_DIM],
                   jnp.where(first_sub, 1.0, vt[r0 + V_DIM:r0 + V_DIM + SUBLANES]),
                   vt[r0 + V_DIM + SUBLANES:r0 + HEAD_PAD]]
    vt_ref[...] = jnp.concatenate(pieces, axis=0).astype(BF16)


def _mixer_in(x2, pos, invf, mod, g1, wcq, wckv, wh, wg, gqa, wuq, gkva, wk, wv, gq, gk, *,
              seq, tm, tk):
    t, d = x2.shape
    tps = seq // tm
    per_kv = tk // tm
    hw = MLA_HEADS * HEAD_PAD
    row = lambda w: pl.BlockSpec((tm, w), lambda i: (i, 0))
    vt_spec = pl.BlockSpec((None, None, hw, tm),
                           lambda i: (i // tps, (i % tps) // per_kv, 0, (i % tps) % per_kv))
    consts = [g1, wcq, wckv, wh, wg, gqa, wuq, gkva, wk, wv, gq, gk]
    out_shapes = [jax.ShapeDtypeStruct((t, hw), BF16)] * 2 + [
        jax.ShapeDtypeStruct((t // seq, seq // tk, hw, tk), BF16)] + [
        jax.ShapeDtypeStruct((t, HG_WIDTH), BF16), jax.ShapeDtypeStruct((t, HG_WIDTH), F32),
        jax.ShapeDtypeStruct((t, HG_WIDTH), BF16), jax.ShapeDtypeStruct((t, HG_WIDTH), BF16),
        jax.ShapeDtypeStruct((t, d), BF16), jax.ShapeDtypeStruct((t, d), BF16)]
    return pl.pallas_call(
        _mixer_in_kernel,
        grid=(t // tm,),
        in_specs=[row(d),
                  pl.BlockSpec((None, 1, tm), lambda i: (i, 0, 0)),
                  _const_spec(invf.shape),
                  pl.BlockSpec((None, 6, d), lambda i: (i // tps, 0, 0))]
                 + [_const_spec(a.shape) for a in consts],
        out_specs=[row(hw)] * 2 + [vt_spec] + [row(HG_WIDTH)] * 4 + [row(d)] * 2,
        out_shape=out_shapes,
        compiler_params=_params(("arbitrary",)),
        name="mixer_in",
    )(x2, pos, invf, mod, *consts)


def _attn_kernel(bound_ref, q_ref, k_ref, vt_ref, o_ref, m_scr, acc_scr, *, tq, tk, diag, heads):
    i = pl.program_id(2)
    nsub = tq // tk
    head_slices = [slice(hd * HEAD_PAD, (hd + 1) * HEAD_PAD) for hd in range(heads)]
    bound = bound_ref[0]

    def run(bounded):
        acc_scr[...] = jnp.zeros(acc_scr.shape, F32)
        if not bounded:
            m_scr[...] = jnp.full(m_scr.shape, NEG_BIG, F32)

        def block(j, off, nkv, row0, diagonal):
            start = pl.multiple_of(j * tk + off, nkv)
            stack = lambda a, n: jnp.concatenate([a] * (n // SUBLANES), axis=0)
            rows = slice(row0, tq)

            def scores(hd):
                sl = head_slices[hd]
                s = _dot_nt(k_ref[pl.ds(start, nkv), sl], q_ref[rows, sl])
                if diagonal:
                    kv_i = lax.broadcasted_iota(jnp.int32, s.shape, 0)
                    q_i = lax.broadcasted_iota(jnp.int32, s.shape, 1)
                    s = jnp.where(kv_i <= q_i, s, NEG_BIG)
                return s

            def update(hd, s):
                vt = vt_ref[j, head_slices[hd], off:off + nkv]
                if bounded:
                    acc_scr[hd, :, rows] += _dot(vt, jnp.exp2(s - bound).astype(BF16))
                    return
                m_prev = m_scr[hd, :, rows]
                m_new = jnp.maximum(m_prev, jnp.max(s, axis=0, keepdims=True))
                p = jnp.exp2(s - stack(m_new, nkv)).astype(BF16)
                acc_scr[hd, :, rows] = (acc_scr[hd, :, rows]
                                        * stack(jnp.exp2(m_prev - m_new), HEAD_PAD) + _dot(vt, p))
                m_scr[hd, :, rows] = m_new

            ahead = 2
            pending = [scores(hd) for hd in range(min(ahead, heads))]
            for hd in range(heads):
                if hd + ahead < heads:
                    pending.append(scores(hd + ahead))
                update(hd, pending.pop(0))

        def body(j, carry):
            for r in range(nsub):
                block(j * nsub + r, 0, tk, 0, False)
            return carry

        lax.fori_loop(0, i, body, 0)
        for pos in range(0, tq, diag):
            block(i * nsub + pos // tk, pos % tk, diag, pos, True)
        outs = []
        for hd in range(heads):
            acc = acc_scr[hd]
            outs.append((acc / acc[V_DIM:V_DIM + 1, :]).T[:, :V_DIM])
        o_ref[...] = jnp.concatenate(outs, axis=-1).astype(o_ref.dtype)

    no_underflow = 2.0 * bound <= SAFE_EXP2_SPAN
    pl.when(no_underflow)(lambda: run(True))
    pl.when(jnp.logical_not(no_underflow))(lambda: run(False))


def _mla_attention(score_bound, q, k, vt, *, tq, tk, diag, heads):
    bsz, seq, hw = q.shape
    groups = MLA_HEADS // heads
    pw = heads * HEAD_PAD
    return pl.pallas_call(
        functools.partial(_attn_kernel, tq=tq, tk=tk, diag=diag, heads=heads),
        grid=(bsz, groups, seq // tq),
        in_specs=[pl.BlockSpec(memory_space=pltpu.SMEM),
                  pl.BlockSpec((None, tq, pw), lambda b, h, i: (b, i, h)),
                  pl.BlockSpec((None, seq, pw), lambda b, h, i: (b, 0, h)),
                  pl.BlockSpec((None, seq // tk, pw, tk), lambda b, h, i: (b, 0, h, 0))],
        out_specs=pl.BlockSpec((None, tq, heads * V_DIM), lambda b, h, i: (b, i, h)),
        out_shape=jax.ShapeDtypeStruct((bsz, seq, MLA_HEADS * V_DIM), BF16),
        scratch_shapes=[pltpu.VMEM((heads, SUBLANES, tq), F32),
                        pltpu.VMEM((heads, HEAD_PAD, tq), F32)],
        compiler_params=_params(("arbitrary", "arbitrary", "arbitrary")),
        name="mla_attn",
    )(score_bound, q, k, vt)


def _hgrn_sum_masks(c):
    t = np.arange(c)[:, None]
    s = np.arange(c)[None, :]
    blocks = [s <= t, s > t]
    blk = c // 2
    while blk >= 2:
        bnd = (t & ~(2 * blk - 1)) + blk - 1
        blocks.append(((s > bnd) & (s <= t)) | ((s > t) & (s <= bnd)))
        blk //= 2
    em = np.concatenate(blocks, axis=0).astype(np.float32)
    return np.concatenate([em, em], axis=1)


def _hgrn_kernel(hq_ref, hf_ref, hi_ref, hg_ref, lbt_ref, gout_ref, em_ref, o_ref, st_ref, *,
                 layer, chunk):
    @pl.when(pl.program_id(1) == 0)
    def _():
        st_ref[...] = jnp.zeros_like(st_ref)

    for sub in range(hq_ref.shape[0] // chunk):
        rs = pl.ds(sub * chunk, chunk)
        _hgrn_chunk(hq_ref.at[rs], hf_ref.at[rs], hi_ref.at[rs], hg_ref.at[rs], lbt_ref, gout_ref,
                    em_ref, o_ref.at[rs], st_ref, layer=layer)


def _hgrn_chunk(hq_ref, hf_ref, hi_ref, hg_ref, lbt_ref, gout_ref, em_ref, o_ref, st_ref, *, layer):
    c, w = hq_ref.shape
    tab = lbt_ref[...]
    e = jnp.exp(tab - jnp.max(tab, axis=0, keepdims=True))
    lb = jnp.sum(e[1:layer + 2], axis=0, keepdims=True) / jnp.sum(e, axis=0, keepdims=True)

    hq = hq_ref[...].astype(F32)
    q = hq * _twice_sigmoid_of_twice(hq)
    f = lb + (0.5 - 0.5 * lb) * _twice_sigmoid_of_twice(hf_ref[...])
    lf = jnp.log(f)
    kk = 1.0 - f
    v = hi_ref[...]

    lf_hi = lf.astype(BF16)
    lf_mid = (lf - lf_hi.astype(F32)).astype(BF16)
    sums = _dot(em_ref[...], jnp.concatenate([lf_hi, lf_mid], axis=0))
    b = sums[0:c]
    q_dec = (q * jnp.exp(b)).astype(BF16)
    k_last = (kk * jnp.exp(sums[c:2 * c])).astype(BF16)
    b_last = b[c - 1:c, :]

    n_levels = int(np.log2(c))
    sub = lax.broadcasted_iota(jnp.int32, (SUBLANES, w), 0)
    xs = []
    for lvl in range(n_levels):
        blk = 1 << lvl
        if blk >= SUBLANES:
            base = jnp.concatenate([(q if (j % 2) else kk)[j * blk:(j + 1) * blk]
                                    for j in range(c // blk)], axis=0)
        else:
            odd = jnp.concatenate([(sub & blk) != 0] * (c // SUBLANES), axis=0)
            base = jnp.where(odd, q * f, kk) if blk == 1 else jnp.where(odd, q, kk)
        if blk > 1:
            row0 = (2 + n_levels - 1 - lvl) * c
            base = base * jnp.exp(sums[row0:row0 + c])
        xs.append(base.astype(BF16))

    t_i = lax.broadcasted_iota(jnp.int32, (c, c), 0)
    s_i = lax.broadcasted_iota(jnp.int32, (c, c), 1)
    lower_xor = jnp.where(t_i > s_i, t_i ^ s_i, 0)
    level_of = [(lower_xor >> lvl) == 1 for lvl in range(n_levels)]
    on_diag = t_i == s_i
    qk = q * kk
    gout = gout_ref[...]
    head_slices = [slice(hd * HG_DIM, (hd + 1) * HG_DIM) for hd in range(HG_HEADS)]
    attns = [jnp.where(on_diag, jnp.sum(qk[:, hs], axis=-1, keepdims=True), 0.0)
             for hs in head_slices]
    for x, keep in zip(xs, level_of):
        attns = [jnp.where(keep, _dot_nt(x[:, hs], x[:, hs]), a) for hs, a in zip(head_slices, attns)]
    for hd, hs in enumerate(head_slices):
        attn = attns[hd]
        st = st_ref[hd]
        o = _dot_nt(q_dec[:, hs], st.astype(BF16)) + _dot(attn.astype(BF16), v[:, hs])
        hg = hg_ref[:, hs].astype(F32)
        o = o * _rms(o, HG_DIM) * gout * (hg * _twice_sigmoid_of_twice(hg))
        o_ref[:, hs] = o.astype(o_ref.dtype)
        st_ref[hd] = st * jnp.exp(b_last[:, hs]) + _dot_tn(v[:, hs], k_last[:, hs])


def _hgrn2(hq, hf, hi, hg, lb_table, gout, *, chunk, chunks_per_step, layer):
    bsz, seq, w = hq.shape
    rows = chunk * chunks_per_step
    blk = pl.BlockSpec((None, rows, w), lambda b, c: (b, c, 0))
    em = jnp.asarray(_hgrn_sum_masks(chunk), dtype=BF16)
    return pl.pallas_call(
        functools.partial(_hgrn_kernel, layer=layer, chunk=chunk),
        grid=(bsz, seq // rows),
        in_specs=[blk, blk, blk, blk, _const_spec(lb_table.shape), _const_spec(gout.shape),
                  _const_spec(em.shape)],
        out_specs=blk,
        out_shape=jax.ShapeDtypeStruct((bsz, seq, w), BF16),
        scratch_shapes=[pltpu.VMEM((HG_HEADS, HG_DIM, HG_DIM), F32)],
        compiler_params=_params(("arbitrary", "arbitrary")),
        name="hgrn2",
    )(hq, hf, hi, hg, lb_table, gout, em)


def _merge_ffn_kernel(a_ref, o_ref, ga_ref, gb_ref, x_ref, modn_ref, modp_ref, wa_ref, wb_ref, wo_ref,
                      g2_ref, wup_ref, cw_ref, cb_ref, wdn_ref, out_ref, x1_scr, h2_scr, up_scr, *, tps):
    tm, d = x_ref.shape
    dff = wdn_ref.shape[0]
    halo = SUBLANES
    i = pl.program_id(0)
    nxt = i % 2
    prv = 1 - nxt

    @pl.when(i == 0)
    def _():
        rows = 2 * SUBLANES

        def clear(r, carry):
            start = pl.multiple_of(r * rows, rows)
            x1_scr[prv, pl.ds(start, rows), :] = jnp.zeros((rows, d), F32)
            h2_scr[prv, pl.ds(start, rows), :] = jnp.zeros((rows, d), BF16)
            return carry

        lax.fori_loop(0, tm // rows, clear, 0)

    @pl.when(jnp.logical_or(i == 0, (i + tps - 1) % tps == 0))
    def _():
        up_scr[0:halo, :] = jnp.zeros((halo, up_scr.shape[1]), F32)

    ya = _dot(a_ref[...], wa_ref[...])
    yb = _dot(o_ref[...], wb_ref[...])
    h2p = h2_scr[prv]

    def conv_block(cols):
        up = _dot(h2p, wup_ref[:, cols])
        up_scr[halo:halo + tm, cols] = up
        y = cb_ref[:, cols] + cw_ref[CONV_W - 1:CONV_W, cols] * up
        for j in range(CONV_W - 1):
            back = CONV_W - 1 - j
            y = y + cw_ref[j:j + 1, cols] * up_scr[halo - back:halo - back + tm, cols]
        up_scr[0:halo, cols] = up[tm - halo:, :]
        return y

    acts = []
    for off in range(0, dff, FF_BLOCK):
        gate = conv_block(slice(off, off + FF_BLOCK))
        val = conv_block(slice(dff + off, dff + off + FF_BLOCK))
        acts.append((gate * val * _twice_sigmoid_of_twice(gate)).astype(BF16))
    act = jnp.concatenate(acts, axis=1)
    out_ref[...] = x1_scr[prv] + modp_ref[5:6, :] * _dot(act, wdn_ref[...])

    merged = (_twice_sigmoid_of_twice(ga_ref[...].astype(F32)) * ya
              + _twice_sigmoid_of_twice(gb_ref[...].astype(F32)) * yb)
    z1 = _dot(merged.astype(BF16), wo_ref[...])
    x1 = x_ref[...] + modn_ref[2:3, :] * z1
    x1_scr[nxt] = x1
    h2 = (x1 * _rms(x1, d)) * (g2_ref[...] * (1.0 + modn_ref[4:5, :])) + modn_ref[3:4, :]
    h2_scr[nxt] = h2.astype(BF16)


def _merge_ffn(attn, hgo, ga, gb, x2, mod, wa, wb, wo, g2, wup, cw, cb, wdn, *, seq, tm):
    t, d = x2.shape
    tps = seq // tm
    n = t // tm
    cur = lambda i: jnp.minimum(i, n - 1)
    prev = lambda i: jnp.maximum(i - 1, 0)
    row = lambda w: pl.BlockSpec((tm, w), lambda i: (cur(i), 0))
    consts = [wa, wb, wo, g2, wup, cw, cb, wdn]
    return pl.pallas_call(
        functools.partial(_merge_ffn_kernel, tps=tps),
        grid=(n + 1,),
        in_specs=[row(attn.shape[1]), row(hgo.shape[1]), row(d), row(d), row(d),
                  pl.BlockSpec((None, 6, d), lambda i: (cur(i) // tps, 0, 0)),
                  pl.BlockSpec((None, 6, d), lambda i: (prev(i) // tps, 0, 0))]
                 + [_const_spec(a.shape) for a in consts],
        out_specs=pl.BlockSpec((tm, d), lambda i: (prev(i), 0)),
        out_shape=jax.ShapeDtypeStruct((t, d), F32),
        scratch_shapes=[pltpu.VMEM((2, tm, d), F32), pltpu.VMEM((2, tm, d), BF16),
                        pltpu.VMEM((tm + SUBLANES, wup.shape[1]), F32)],
        compiler_params=_params(("arbitrary",)),
        name="merge_ffn",
    )(attn, hgo, ga, gb, x2, mod, mod, *consts)


def _pad_heads(w, used, n_heads):
    rows = w.shape[0]
    w = w.reshape(rows, n_heads, used)
    w = jnp.pad(w, ((0, 0), (0, 0), (0, HEAD_PAD - used)))
    return w.reshape(rows, n_heads * HEAD_PAD)


def _layer(x2, pos, invf, mod, p, *, bsz, seq, layer):
    t, d = x2.shape
    w_in = p["w_in"]
    o_ckv = Q_LORA
    o_h = o_ckv + KV_LORA + ROPE_DIM
    o_g = o_h + 4 * HG_WIDTH
    wcq = w_in[:, :o_ckv].astype(BF16)
    wckv = jnp.concatenate([w_in[:, o_ckv:o_ckv + KV_LORA],
                            jnp.zeros((d, NOPE_DIM), F32),
                            w_in[:, o_ckv + KV_LORA:o_h],
                            jnp.zeros((d, HEAD_PAD - QK_DIM), F32)], axis=1).astype(BF16)
    wh = w_in[:, o_h:o_g].astype(BF16)
    wg = w_in[:, o_g:].astype(BF16)
    wuq = _pad_heads(p["w_uq"], QK_DIM, MLA_HEADS).astype(BF16)
    wukv = p["w_ukv"].reshape(KV_LORA, MLA_HEADS, NOPE_DIM + V_DIM)
    wk = _pad_heads(wukv[:, :, :NOPE_DIM].reshape(KV_LORA, -1), NOPE_DIM, MLA_HEADS).astype(BF16)
    wvt = _pad_heads(wukv[:, :, NOPE_DIM:].reshape(KV_LORA, -1), V_DIM, MLA_HEADS).T.astype(BF16)
    pad_g = lambda g: jnp.pad(g, (0, HEAD_PAD - QK_DIM)).reshape(1, HEAD_PAD)

    tm = min(256, seq)
    tq, tk = min(1024, seq), min(512, seq)
    q, k, vt, hq, hf, hi, hg, ga, gb = _mixer_in(
        x2, pos.reshape(t // tm, 1, tm), invf, mod, p["norm1_g"].reshape(1, d), wcq, wckv, wh,
        wg, p["q_a_norm_g"].reshape(1, -1), wuq, p["kv_a_norm_g"].reshape(1, -1), wk, wvt,
        pad_g(p["q_norm_g"]), pad_g(p["k_norm_g"]), seq=seq, tm=tm, tk=tk)

    hw = MLA_HEADS * HEAD_PAD
    score_bound = (BF16_MARGIN * QK_DIM * Q_SCALE * jnp.max(jnp.abs(p["q_norm_g"]))
                   * jnp.max(jnp.abs(p["k_norm_g"]))).reshape(1).astype(F32)
    attn = _mla_attention(score_bound, q.reshape(bsz, seq, hw), k.reshape(bsz, seq, hw), vt,
                          tq=tq, tk=tk, diag=min(256, tk), heads=4)
    r3 = lambda a: a.reshape(bsz, seq, HG_WIDTH)
    hgo = _hgrn2(r3(hq), r3(hf), r3(hi), r3(hg), p["hg_lower_bound"],
                 p["hg_out_norm_g"].reshape(1, HG_DIM), chunk=128, chunks_per_step=4, layer=layer)

    dff = p["w_down"].shape[0]
    halve_gate = jnp.asarray(np.repeat(np.float32([0.5, 1.0]), dff))
    return _merge_ffn(attn.reshape(t, -1), hgo.reshape(t, -1), ga, gb, x2, mod,
                      p["w_branch_a"].astype(BF16), p["w_branch_b"].astype(BF16),
                      (p["w_out"] * 0.5).astype(BF16), p["norm2_g"].reshape(1, d),
                      p["w_up"].astype(BF16), p["conv_w"] * halve_gate,
                      (p["conv_b"] * halve_gate).reshape(1, -1),
                      p["w_down"].astype(BF16), seq=seq, tm=tm)


def kernel(x, c, positions, w_ada, b_ada, norm1_g, w_in, q_a_norm_g, w_uq, kv_a_norm_g, w_ukv,
           q_norm_g, k_norm_g, hg_lower_bound, hg_out_norm_g, w_branch_a, w_branch_b, w_out,
           norm2_g, w_up, conv_w, conv_b, w_down):
    bsz, seq, d = x.shape
    depth = w_ada.shape[0]
    invf = (ROPE_THETA ** (-jnp.arange(0, ROPE_DIM, 2, dtype=F32) / ROPE_DIM)).reshape(HALF_ROPE, 1)
    x2 = x.reshape(bsz * seq, d)
    pos = positions.reshape(bsz * seq)
    for l in range(depth):
        mod = _ada_mod(c, w_ada, b_ada, l).reshape(bsz, 6, d)
        p = dict(norm1_g=norm1_g[l], w_in=w_in[l], q_a_norm_g=q_a_norm_g[l], w_uq=w_uq[l],
                 kv_a_norm_g=kv_a_norm_g[l], w_ukv=w_ukv[l], q_norm_g=q_norm_g[l],
                 k_norm_g=k_norm_g[l], hg_lower_bound=hg_lower_bound, hg_out_norm_g=hg_out_norm_g[l],
                 w_branch_a=w_branch_a[l], w_branch_b=w_branch_b[l], w_out=w_out[l],
                 norm2_g=norm2_g[l], w_up=w_up[l], conv_w=conv_w[l], conv_b=conv_b[l],
                 w_down=w_down[l])
        x2 = _layer(x2, pos, invf, mod, p, bsz=bsz, seq=seq, layer=l)
    return x2.reshape(bsz, seq, d)
```

```python
import functools
from typing import NamedTuple

import numpy as np
import jax
import jax.numpy as jnp
from jax import lax
from jax.experimental import pallas as pl
from jax.experimental.pallas import tpu as pltpu

F32 = jnp.float32
BF16 = jnp.bfloat16

LANES = 128
SUBLANES = 8
VMEM_LIMIT = 56 * 1024 * 1024

MLA_HEADS = 8
NOPE_DIM = 64
ROPE_DIM = 32
HALF_ROPE = ROPE_DIM // 2
QK_DIM = NOPE_DIM + ROPE_DIM
V_DIM = 64
Q_LORA = 512
KV_LORA = 256
ROPE_THETA = 10000.0
HG_HEADS = 4
HG_DIM = 128
HG_WIDTH = HG_HEADS * HG_DIM
CONV_W = 3
EPS = 1e-6
HEAD_PAD = LANES
FF_BLOCK = 2 * LANES
LOG2E = float(np.log2(np.e))
NEG_BIG = -1e30
Q_SCALE = LOG2E / float(np.sqrt(QK_DIM))
BF16_MARGIN = (1.0 + 2.0 ** -8) ** 2
SAFE_EXP2_SPAN = 80.0


def _dot(a, b):
    return jnp.dot(a, b, preferred_element_type=F32)


def _dot_nt(a, b):
    return lax.dot_general(a, b, (((1,), (1,)), ((), ())), preferred_element_type=F32)


def _dot_tn(a, b):
    return lax.dot_general(a, b, (((0,), (0,)), ((), ())), preferred_element_type=F32)


def _twice_sigmoid_of_twice(h):
    return 1.0 + jnp.tanh(h)


def _rms(x, width):
    return lax.rsqrt(jnp.sum(x * x, axis=-1, keepdims=True) * (1.0 / width) + EPS)


def _const_spec(shape):
    nd = len(shape)
    return pl.BlockSpec(shape, lambda *_: (0,) * nd, pipeline_mode=pl.Buffered(1))


def _params(sem):
    return pltpu.CompilerParams(dimension_semantics=sem, vmem_limit_bytes=VMEM_LIMIT)


def _ada_kernel(c_ref, w_ref, b_ref, o_ref):
    c = c_ref[...]
    w = w_ref[...]
    c_hi = c.astype(BF16)
    c_lo = (c - c_hi.astype(F32)).astype(BF16)
    w_hi = w.astype(BF16)
    w_lo = (w - w_hi.astype(F32)).astype(BF16)
    o_ref[...] = _dot(c_hi, w_hi) + _dot(c_hi, w_lo) + _dot(c_lo, w_hi) + b_ref[...]


def _ada_mod(c, w_ada, b_ada, layer):
    bsz, d = c.shape
    depth, _, n = w_ada.shape
    bn = 1024
    return pl.pallas_call(
        _ada_kernel,
        grid=(n // bn,),
        in_specs=[pl.BlockSpec((bsz, d), lambda j: (0, 0)),
                  pl.BlockSpec((None, d, bn), lambda j: (layer, 0, j)),
                  pl.BlockSpec((None, 1, bn), lambda j: (layer, 0, j))],
        out_specs=pl.BlockSpec((bsz, bn), lambda j: (0, j)),
        out_shape=jax.ShapeDtypeStruct((bsz, n), F32),
        compiler_params=_params(("arbitrary",)),
        name="ada_mod",
    )(c, w_ada, b_ada.reshape(depth, 1, n))


def _rope(t, cos_t, sin_lo, sin_hi):
    return (t * cos_t + pltpu.roll(t, LANES - HALF_ROPE, 1) * sin_lo
            + pltpu.roll(t, HALF_ROPE, 1) * sin_hi)


def _mixer_in_kernel(x_ref, pos_ref, invf_ref, mod_ref, g1_ref, wcq_ref, wckv_ref, wh_ref, wg_ref,
                     gqa_ref, wuq_ref, gkva_ref, wk_ref, wv_ref, gq_ref, gk_ref,
                     q_ref, k_ref, vt_ref, hq_ref, hf_ref, hi_ref, hg_ref, ga_ref, gb_ref):
    tm, d = x_ref.shape
    x = x_ref[...]
    h = (x * _rms(x, d)) * (g1_ref[...] * (1.0 + mod_ref[1:2, :])) + mod_ref[0:1, :]
    hb = h.astype(BF16)

    ang = invf_ref[...] * pos_ref[...].astype(F32)
    cos_h = jnp.cos(ang)
    sin_h = jnp.sin(ang)
    zeros = lambda r: jnp.zeros((r, tm), F32)
    cos_t = jnp.concatenate([jnp.ones((NOPE_DIM, tm), F32), cos_h, cos_h,
                             zeros(HEAD_PAD - QK_DIM)], axis=0).T
    sin_lo = jnp.concatenate([zeros(NOPE_DIM), -sin_h, zeros(HEAD_PAD - NOPE_DIM - HALF_ROPE)],
                             axis=0).T
    sin_hi = jnp.concatenate([zeros(NOPE_DIM + HALF_ROPE), sin_h, zeros(HEAD_PAD - QK_DIM)],
                             axis=0).T

    cq = _dot(hb, wcq_ref[...])
    ckv = _dot(hb, wckv_ref[...])
    hh = _dot(hb, wh_ref[:, :2 * HG_WIDTH])
    hq_ref[...] = (0.5 * hh[:, :HG_WIDTH]).astype(BF16)
    hf_ref[...] = 0.5 * hh[:, HG_WIDTH:]

    cqn = (cq * _rms(cq, Q_LORA) * gqa_ref[...]).astype(BF16)
    ckv_c = ckv[:, :KV_LORA]
    kr = ckv[:, KV_LORA:]
    ckvn = (ckv_c * _rms(ckv_c, KV_LORA) * gkva_ref[...]).astype(BF16)
    qf = _dot(cqn, wuq_ref[...])
    kf = _dot(ckvn, wk_ref[...])
    vt = _dot_nt(wv_ref[...], ckvn)
    hh = _dot(hb, wh_ref[:, 2 * HG_WIDTH:])
    hi_ref[...] = hh[:, :HG_WIDTH].astype(BF16)
    hg_ref[...] = (0.5 * hh[:, HG_WIDTH:]).astype(BF16)
    gg = _dot(hb, wg_ref[...])
    ga_ref[...] = (0.5 * gg[:, :d]).astype(BF16)
    gb_ref[...] = (0.5 * gg[:, d:]).astype(BF16)

    gq = gq_ref[...]
    for hd in range(MLA_HEADS):
        sl = slice(hd * HEAD_PAD, (hd + 1) * HEAD_PAD)
        qh = qf[:, sl]
        qn = qh * _rms(qh, QK_DIM) * gq
        q_ref[:, sl] = (_rope(qn, cos_t, sin_lo, sin_hi) * Q_SCALE).astype(BF16)

    gk = gk_ref[...]
    ss_rope = jnp.sum(kr * kr, axis=-1, keepdims=True)
    kr_rot = _rope(kr * gk, cos_t, sin_lo, sin_hi)
    for hd in range(MLA_HEADS):
        sl = slice(hd * HEAD_PAD, (hd + 1) * HEAD_PAD)
        kn = kf[:, sl]
        ss = jnp.sum(kn * kn, axis=-1, keepdims=True) + ss_rope
        k_ref[:, sl] = ((kn * gk + kr_rot) * lax.rsqrt(ss * (1.0 / QK_DIM) + EPS)).astype(BF16)
    first_sub = lax.broadcasted_iota(jnp.int32, (SUBLANES, tm), 0) == 0
    pieces = []
    for hd in range(MLA_HEADS):
        r0 = hd * HEAD_PAD
        pieces += [vt[r0:r0 + V_DIM],
                   jnp.where(first_sub, 1.0, vt[r0 + V_DIM:r0 + V_DIM + SUBLANES]),
                   vt[r0 + V_DIM + SUBLANES:r0 + HEAD_PAD]]
    vt_ref[...] = jnp.concatenate(pieces, axis=0).astype(BF16)


def _mixer_in(x2, pos, invf, mod, g1, wcq, wckv, wh, wg, gqa, wuq, gkva, wk, wv, gq, gk, *,
              seq, tm, tk):
    t, d = x2.shape
    tps = seq // tm
    per_kv = tk // tm
    hw = MLA_HEADS * HEAD_PAD
    row = lambda w: pl.BlockSpec((tm, w), lambda i: (i, 0))
    vt_spec = pl.BlockSpec((None, None, hw, tm),
                           lambda i: (i // tps, (i % tps) // per_kv, 0, (i % tps) % per_kv))
    consts = [g1, wcq, wckv, wh, wg, gqa, wuq, gkva, wk, wv, gq, gk]
    out_shapes = [jax.ShapeDtypeStruct((t, hw), BF16)] * 2 + [
        jax.ShapeDtypeStruct((t // seq, seq // tk, hw, tk), BF16)] + [
        jax.ShapeDtypeStruct((t, HG_WIDTH), BF16), jax.ShapeDtypeStruct((t, HG_WIDTH), F32),
        jax.ShapeDtypeStruct((t, HG_WIDTH), BF16), jax.ShapeDtypeStruct((t, HG_WIDTH), BF16),
        jax.ShapeDtypeStruct((t, d), BF16), jax.ShapeDtypeStruct((t, d), BF16)]
    return pl.pallas_call(
        _mixer_in_kernel,
        grid=(t // tm,),
        in_specs=[row(d),
                  pl.BlockSpec((None, 1, tm), lambda i: (i, 0, 0)),
                  _const_spec(invf.shape),
                  pl.BlockSpec((None, 6, d), lambda i: (i // tps, 0, 0))]
                 + [_const_spec(a.shape) for a in consts],
        out_specs=[row(hw)] * 2 + [vt_spec] + [row(HG_WIDTH)] * 4 + [row(d)] * 2,
        out_shape=out_shapes,
        compiler_params=_params(("arbitrary",)),
        name="mixer_in",
    )(x2, pos, invf, mod, *consts)


def _attn_kernel(bound_ref, q_ref, k_ref, vt_ref, o_ref, m_scr, acc_scr, *, tq, tk, diag, heads):
    i = pl.program_id(2)
    nsub = tq // tk
    head_slices = [slice(hd * HEAD_PAD, (hd + 1) * HEAD_PAD) for hd in range(heads)]
    bound = bound_ref[0]

    def run(bounded):
        acc_scr[...] = jnp.zeros(acc_scr.shape, F32)
        if not bounded:
            m_scr[...] = jnp.full(m_scr.shape, NEG_BIG, F32)

        def block(j, off, nkv, row0, diagonal):
            start = pl.multiple_of(j * tk + off, nkv)
            stack = lambda a, n: jnp.concatenate([a] * (n // SUBLANES), axis=0)
            rows = slice(row0, tq)

            def scores(hd):
                sl = head_slices[hd]
                s = _dot_nt(k_ref[pl.ds(start, nkv), sl], q_ref[rows, sl])
                if diagonal:
                    kv_i = lax.broadcasted_iota(jnp.int32, s.shape, 0)
                    q_i = lax.broadcasted_iota(jnp.int32, s.shape, 1)
                    s = jnp.where(kv_i <= q_i, s, NEG_BIG)
                return s

            def update(hd, s):
                vt = vt_ref[j, head_slices[hd], off:off + nkv]
                if bounded:
                    acc_scr[hd, :, rows] += _dot(vt, jnp.exp2(s - bound).astype(BF16))
                    return
                m_prev = m_scr[hd, :, rows]
                m_new = jnp.maximum(m_prev, jnp.max(s, axis=0, keepdims=True))
                p = jnp.exp2(s - stack(m_new, nkv)).astype(BF16)
                acc_scr[hd, :, rows] = (acc_scr[hd, :, rows]
                                        * stack(jnp.exp2(m_prev - m_new), HEAD_PAD) + _dot(vt, p))
                m_scr[hd, :, rows] = m_new

            ahead = 2
            pending = [scores(hd) for hd in range(min(ahead, heads))]
            for hd in range(heads):
                if hd + ahead < heads:
                    pending.append(scores(hd + ahead))
                update(hd, pending.pop(0))

        def body(j, carry):
            for r in range(nsub):
                block(j * nsub + r, 0, tk, 0, False)
            return carry

        lax.fori_loop(0, i, body, 0)
        for pos in range(0, tq, diag):
            block(i * nsub + pos // tk, pos % tk, diag, pos, True)
        outs = []
        for hd in range(0, heads, 2):
            pair = [acc_scr[h, :V_DIM, :] / acc_scr[h, V_DIM:V_DIM + 1, :] for h in (hd, hd + 1)]
            outs.append(jnp.concatenate(pair, axis=0).T)
        o_ref[...] = jnp.concatenate(outs, axis=-1).astype(o_ref.dtype)

    no_underflow = 2.0 * bound <= SAFE_EXP2_SPAN
    pl.when(no_underflow)(lambda: run(True))
    pl.when(jnp.logical_not(no_underflow))(lambda: run(False))


def _mla_attention(score_bound, q, k, vt, *, tq, tk, diag, heads):
    bsz, seq, hw = q.shape
    groups = MLA_HEADS // heads
    pw = heads * HEAD_PAD
    return pl.pallas_call(
        functools.partial(_attn_kernel, tq=tq, tk=tk, diag=diag, heads=heads),
        grid=(bsz, groups, seq // tq),
        in_specs=[pl.BlockSpec(memory_space=pltpu.SMEM),
                  pl.BlockSpec((None, tq, pw), lambda b, h, i: (b, i, h)),
                  pl.BlockSpec((None, seq, pw), lambda b, h, i: (b, 0, h)),
                  pl.BlockSpec((None, seq // tk, pw, tk), lambda b, h, i: (b, 0, h, 0))],
        out_specs=pl.BlockSpec((None, tq, heads * V_DIM), lambda b, h, i: (b, i, h)),
        out_shape=jax.ShapeDtypeStruct((bsz, seq, MLA_HEADS * V_DIM), BF16),
        scratch_shapes=[pltpu.VMEM((heads, SUBLANES, tq), F32),
                        pltpu.VMEM((heads, HEAD_PAD, tq), F32)],
        compiler_params=_params(("arbitrary", "arbitrary", "arbitrary")),
        name="mla_attn",
    )(score_bound, q, k, vt)


def _hgrn_sum_masks(c):
    t = np.arange(c)[:, None]
    s = np.arange(c)[None, :]
    blocks = [s <= t, s > t]
    blk = c // 2
    while blk >= 2:
        bnd = (t & ~(2 * blk - 1)) + blk - 1
        blocks.append(((s > bnd) & (s <= t)) | ((s > t) & (s <= bnd)))
        blk //= 2
    em = np.concatenate(blocks, axis=0).astype(np.float32)
    return np.concatenate([em, em], axis=1)


def _hgrn_kernel(hq_ref, hf_ref, hi_ref, hg_ref, lbt_ref, gout_ref, em_ref, o_ref, st_ref, *,
                 layer, chunk):
    @pl.when(pl.program_id(1) == 0)
    def _():
        st_ref[...] = jnp.zeros_like(st_ref)

    for sub in range(hq_ref.shape[0] // chunk):
        rs = pl.ds(sub * chunk, chunk)
        _hgrn_chunk(hq_ref.at[rs], hf_ref.at[rs], hi_ref.at[rs], hg_ref.at[rs], lbt_ref, gout_ref,
                    em_ref, o_ref.at[rs], st_ref, layer=layer)


def _hgrn_chunk(hq_ref, hf_ref, hi_ref, hg_ref, lbt_ref, gout_ref, em_ref, o_ref, st_ref, *, layer):
    c, w = hq_ref.shape
    tab = lbt_ref[...]
    e = jnp.exp(tab - jnp.max(tab, axis=0, keepdims=True))
    lb = jnp.sum(e[1:layer + 2], axis=0, keepdims=True) / jnp.sum(e, axis=0, keepdims=True)

    hq = hq_ref[...].astype(F32)
    q = hq * _twice_sigmoid_of_twice(hq)
    f = lb + (0.5 - 0.5 * lb) * _twice_sigmoid_of_twice(hf_ref[...])
    lf = jnp.log(f)
    kk = 1.0 - f
    v = hi_ref[...]

    lf_hi = lf.astype(BF16)
    lf_mid = (lf - lf_hi.astype(F32)).astype(BF16)
    sums = _dot(em_ref[...], jnp.concatenate([lf_hi, lf_mid], axis=0))
    b = sums[0:c]
    q_dec = (q * jnp.exp(b)).astype(BF16)
    k_last = (kk * jnp.exp(sums[c:2 * c])).astype(BF16)
    b_last = b[c - 1:c, :]

    n_levels = int(np.log2(c))
    sub = lax.broadcasted_iota(jnp.int32, (SUBLANES, w), 0)
    xs = []
    for lvl in range(n_levels):
        blk = 1 << lvl
        if blk >= SUBLANES:
            base = jnp.concatenate([(q if (j % 2) else kk)[j * blk:(j + 1) * blk]
                                    for j in range(c // blk)], axis=0)
        else:
            odd = jnp.concatenate([(sub & blk) != 0] * (c // SUBLANES), axis=0)
            base = jnp.where(odd, q * f, kk) if blk == 1 else jnp.where(odd, q, kk)
        if blk > 1:
            row0 = (2 + n_levels - 1 - lvl) * c
            base = base * jnp.exp(sums[row0:row0 + c])
        xs.append(base.astype(BF16))

    t_i = lax.broadcasted_iota(jnp.int32, (c, c), 0)
    s_i = lax.broadcasted_iota(jnp.int32, (c, c), 1)
    lower_xor = jnp.where(t_i > s_i, t_i ^ s_i, 0)
    level_of = [(lower_xor >> lvl) == 1 for lvl in range(n_levels)]
    on_diag = t_i == s_i
    qk = q * kk
    gout = gout_ref[...]
    head_slices = [slice(hd * HG_DIM, (hd + 1) * HG_DIM) for hd in range(HG_HEADS)]
    attns = [jnp.where(on_diag, jnp.sum(qk[:, hs], axis=-1, keepdims=True), 0.0)
             for hs in head_slices]
    for x, keep in zip(xs, level_of):
        attns = [jnp.where(keep, _dot_nt(x[:, hs], x[:, hs]), a) for hs, a in zip(head_slices, attns)]
    for hd, hs in enumerate(head_slices):
        attn = attns[hd]
        st = st_ref[hd]
        o = _dot_nt(q_dec[:, hs], st.astype(BF16)) + _dot(attn.astype(BF16), v[:, hs])
        hg = hg_ref[:, hs].astype(F32)
        o = o * _rms(o, HG_DIM) * gout * (hg * _twice_sigmoid_of_twice(hg))
        o_ref[:, hs] = o.astype(o_ref.dtype)
        st_ref[hd] = st * jnp.exp(b_last[:, hs]) + _dot_tn(v[:, hs], k_last[:, hs])


def _hgrn2(hq, hf, hi, hg, lb_table, gout, *, chunk, chunks_per_step, layer):
    bsz, seq, w = hq.shape
    rows = chunk * chunks_per_step
    blk = pl.BlockSpec((None, rows, w), lambda b, c: (b, c, 0))
    em = jnp.asarray(_hgrn_sum_masks(chunk), dtype=BF16)
    return pl.pallas_call(
        functools.partial(_hgrn_kernel, layer=layer, chunk=chunk),
        grid=(bsz, seq // rows),
        in_specs=[blk, blk, blk, blk, _const_spec(lb_table.shape), _const_spec(gout.shape),
                  _const_spec(em.shape)],
        out_specs=blk,
        out_shape=jax.ShapeDtypeStruct((bsz, seq, w), BF16),
        scratch_shapes=[pltpu.VMEM((HG_HEADS, HG_DIM, HG_DIM), F32)],
        compiler_params=_params(("arbitrary", "arbitrary")),
        name="hgrn2",
    )(hq, hf, hi, hg, lb_table, gout, em)


def _merge_ffn_kernel(a_ref, o_ref, ga_ref, gb_ref, x_ref, modn_ref, modp_ref, wa_ref, wb_ref, wo_ref,
                      g2_ref, wup_ref, cw_ref, cb_ref, wdn_ref, out_ref, x1_scr, h2_scr, up_scr, *, tps):
    tm, d = x_ref.shape
    dff = wdn_ref.shape[0]
    halo = SUBLANES
    i = pl.program_id(0)
    nxt = i % 2
    prv = 1 - nxt

    @pl.when(i == 0)
    def _():
        rows = 2 * SUBLANES

        def clear(r, carry):
            start = pl.multiple_of(r * rows, rows)
            x1_scr[prv, pl.ds(start, rows), :] = jnp.zeros((rows, d), F32)
            h2_scr[prv, pl.ds(start, rows), :] = jnp.zeros((rows, d), BF16)
            return carry

        lax.fori_loop(0, tm // rows, clear, 0)

    @pl.when(jnp.logical_or(i == 0, (i + tps - 1) % tps == 0))
    def _():
        up_scr[0:halo, :] = jnp.zeros((halo, up_scr.shape[1]), F32)

    ya = _dot(a_ref[...], wa_ref[...])
    yb = _dot(o_ref[...], wb_ref[...])
    h2p = h2_scr[prv]

    def conv_block(cols):
        up = _dot(h2p, wup_ref[:, cols])
        up_scr[halo:halo + tm, cols] = up
        y = cb_ref[:, cols] + cw_ref[CONV_W - 1:CONV_W, cols] * up
        for j in range(CONV_W - 1):
            back = CONV_W - 1 - j
            y = y + cw_ref[j:j + 1, cols] * up_scr[halo - back:halo - back + tm, cols]
        up_scr[0:halo, cols] = up[tm - halo:, :]
        return y

    acts = []
    for off in range(0, dff, FF_BLOCK):
        gate = conv_block(slice(off, off + FF_BLOCK))
        val = conv_block(slice(dff + off, dff + off + FF_BLOCK))
        acts.append((gate * val * _twice_sigmoid_of_twice(gate)).astype(BF16))
    act = jnp.concatenate(acts, axis=1)
    out_ref[...] = x1_scr[prv] + modp_ref[5:6, :] * _dot(act, wdn_ref[...])

    merged = (_twice_sigmoid_of_twice(ga_ref[...].astype(F32)) * ya
              + _twice_sigmoid_of_twice(gb_ref[...].astype(F32)) * yb)
    z1 = _dot(merged.astype(BF16), wo_ref[...])
    x1 = x_ref[...] + modn_ref[2:3, :] * z1
    x1_scr[nxt] = x1
    h2 = (x1 * _rms(x1, d)) * (g2_ref[...] * (1.0 + modn_ref[4:5, :])) + modn_ref[3:4, :]
    h2_scr[nxt] = h2.astype(BF16)


def _merge_ffn(attn, hgo, ga, gb, x2, mod, wa, wb, wo, g2, wup, cw, cb, wdn, *, seq, tm):
    t, d = x2.shape
    tps = seq // tm
    n = t // tm
    cur = lambda i: jnp.minimum(i, n - 1)
    prev = lambda i: jnp.maximum(i - 1, 0)
    row = lambda w: pl.BlockSpec((tm, w), lambda i: (cur(i), 0))
    consts = [wa, wb, wo, g2, wup, cw, cb, wdn]
    return pl.pallas_call(
        functools.partial(_merge_ffn_kernel, tps=tps),
        grid=(n + 1,),
        in_specs=[row(attn.shape[1]), row(hgo.shape[1]), row(d), row(d), row(d),
                  pl.BlockSpec((None, 6, d), lambda i: (cur(i) // tps, 0, 0)),
                  pl.BlockSpec((None, 6, d), lambda i: (prev(i) // tps, 0, 0))]
                 + [_const_spec(a.shape) for a in consts],
        out_specs=pl.BlockSpec((tm, d), lambda i: (prev(i), 0)),
        out_shape=jax.ShapeDtypeStruct((t, d), F32),
        scratch_shapes=[pltpu.VMEM((2, tm, d), F32), pltpu.VMEM((2, tm, d), BF16),
                        pltpu.VMEM((tm + SUBLANES, wup.shape[1]), F32)],
        compiler_params=_params(("arbitrary",)),
        name="merge_ffn",
    )(attn, hgo, ga, gb, x2, mod, mod, *consts)


def _pad_heads(w, used, n_heads):
    rows = w.shape[0]
    w = w.reshape(rows, n_heads, used)
    w = jnp.pad(w, ((0, 0), (0, 0), (0, HEAD_PAD - used)))
    return w.reshape(rows, n_heads * HEAD_PAD)


class _Tiles(NamedTuple):
    token: int
    q: int
    kv: int
    diag: int
    attn_heads: int
    chunk: int
    chunks_per_step: int


def _tiles(seq):
    tl = _Tiles(token=min(256, seq), q=min(1024, seq), kv=min(512, seq), diag=min(256, seq),
                attn_heads=4, chunk=min(128, seq), chunks_per_step=max(1, min(4, seq // 128)))
    assert seq % tl.token == 0 and seq % tl.q == 0 and tl.q % tl.kv == 0 and tl.kv % tl.diag == 0
    assert tl.kv % tl.token == 0 and seq % (tl.chunk * tl.chunks_per_step) == 0
    assert MLA_HEADS % tl.attn_heads == 0
    return tl


def _layer(x2, pos, invf, mod, p, *, bsz, seq, layer):
    t, d = x2.shape
    tl = _tiles(seq)
    w_in = p["w_in"]
    o_ckv = Q_LORA
    o_h = o_ckv + KV_LORA + ROPE_DIM
    o_g = o_h + 4 * HG_WIDTH
    wcq = w_in[:, :o_ckv].astype(BF16)
    wckv = jnp.concatenate([w_in[:, o_ckv:o_ckv + KV_LORA],
                            jnp.zeros((d, NOPE_DIM), F32),
                            w_in[:, o_ckv + KV_LORA:o_h],
                            jnp.zeros((d, HEAD_PAD - QK_DIM), F32)], axis=1).astype(BF16)
    wh = w_in[:, o_h:o_g].astype(BF16)
    wg = w_in[:, o_g:].astype(BF16)
    wuq = _pad_heads(p["w_uq"], QK_DIM, MLA_HEADS).astype(BF16)
    wukv = p["w_ukv"].reshape(KV_LORA, MLA_HEADS, NOPE_DIM + V_DIM)
    wk = _pad_heads(wukv[:, :, :NOPE_DIM].reshape(KV_LORA, -1), NOPE_DIM, MLA_HEADS).astype(BF16)
    wvt = _pad_heads(wukv[:, :, NOPE_DIM:].reshape(KV_LORA, -1), V_DIM, MLA_HEADS).T.astype(BF16)
    pad_g = lambda g: jnp.pad(g, (0, HEAD_PAD - QK_DIM)).reshape(1, HEAD_PAD)

    tm = tl.token
    q, k, vt, hq, hf, hi, hg, ga, gb = _mixer_in(
        x2, pos.reshape(t // tm, 1, tm), invf, mod, p["norm1_g"].reshape(1, d), wcq, wckv, wh,
        wg, p["q_a_norm_g"].reshape(1, -1), wuq, p["kv_a_norm_g"].reshape(1, -1), wk, wvt,
        pad_g(p["q_norm_g"]), pad_g(p["k_norm_g"]), seq=seq, tm=tm, tk=tl.kv)

    hw = MLA_HEADS * HEAD_PAD
    score_bound = (BF16_MARGIN * QK_DIM * Q_SCALE * jnp.max(jnp.abs(p["q_norm_g"]))
                   * jnp.max(jnp.abs(p["k_norm_g"]))).reshape(1).astype(F32)
    attn = _mla_attention(score_bound, q.reshape(bsz, seq, hw), k.reshape(bsz, seq, hw), vt,
                          tq=tl.q, tk=tl.kv, diag=tl.diag, heads=tl.attn_heads)
    r3 = lambda a: a.reshape(bsz, seq, HG_WIDTH)
    hgo = _hgrn2(r3(hq), r3(hf), r3(hi), r3(hg), p["hg_lower_bound"],
                 p["hg_out_norm_g"].reshape(1, HG_DIM), chunk=tl.chunk,
                 chunks_per_step=tl.chunks_per_step, layer=layer)

    dff = p["w_down"].shape[0]
    halve_gate = jnp.asarray(np.repeat(np.float32([0.5, 1.0]), dff))
    return _merge_ffn(attn.reshape(t, -1), hgo.reshape(t, -1), ga, gb, x2, mod,
                      p["w_branch_a"].astype(BF16), p["w_branch_b"].astype(BF16),
                      (p["w_out"] * 0.5).astype(BF16), p["norm2_g"].reshape(1, d),
                      p["w_up"].astype(BF16), p["conv_w"] * halve_gate,
                      (p["conv_b"] * halve_gate).reshape(1, -1),
                      p["w_down"].astype(BF16), seq=seq, tm=tm)


def kernel(x, c, positions, w_ada, b_ada, norm1_g, w_in, q_a_norm_g, w_uq, kv_a_norm_g, w_ukv,
           q_norm_g, k_norm_g, hg_lower_bound, hg_out_norm_g, w_branch_a, w_branch_b, w_out,
           norm2_g, w_up, conv_w, conv_b, w_down):
    bsz, seq, d = x.shape
    depth = w_ada.shape[0]
    invf = (ROPE_THETA ** (-jnp.arange(0, ROPE_DIM, 2, dtype=F32) / ROPE_DIM)).reshape(HALF_ROPE, 1)
    x2 = x.reshape(bsz * seq, d)
    pos = positions.reshape(bsz * seq)
    for l in range(depth):
        mod = _ada_mod(c, w_ada, b_ada, l).reshape(bsz, 6, d)
        p = dict(norm1_g=norm1_g[l], w_in=w_in[l], q_a_norm_g=q_a_norm_g[l], w_uq=w_uq[l],
                 kv_a_norm_g=kv_a_norm_g[l], w_ukv=w_ukv[l], q_norm_g=q_norm_g[l],
                 k_norm_g=k_norm_g[l], hg_lower_bound=hg_lower_bound, hg_out_norm_g=hg_out_norm_g[l],
                 w_branch_a=w_branch_a[l], w_branch_b=w_branch_b[l], w_out=w_out[l],
                 norm2_g=norm2_g[l], w_up=w_up[l], conv_w=conv_w[l], conv_b=conv_b[l],
                 w_down=w_down[l])
        x2 = _layer(x2, pos, invf, mod, p, bsz=bsz, seq=seq, layer=l)
    return x2.reshape(bsz, seq, d)
```

```python
import functools
from typing import NamedTuple

import numpy as np
import jax
import jax.numpy as jnp
from jax import lax
from jax.experimental import pallas as pl
from jax.experimental.pallas import tpu as pltpu

F32 = jnp.float32
BF16 = jnp.bfloat16

LANES = 128
SUBLANES = 8
VMEM_LIMIT = 56 * 1024 * 1024

MLA_HEADS = 8
NOPE_DIM = 64
ROPE_DIM = 32
HALF_ROPE = ROPE_DIM // 2
QK_DIM = NOPE_DIM + ROPE_DIM
V_DIM = 64
Q_LORA = 512
KV_LORA = 256
ROPE_THETA = 10000.0
HG_HEADS = 4
HG_DIM = 128
HG_WIDTH = HG_HEADS * HG_DIM
CONV_W = 3
EPS = 1e-6
HEAD_PAD = LANES
FF_BLOCK = 2 * LANES
LOG2E = float(np.log2(np.e))
NEG_BIG = -1e30
Q_SCALE = LOG2E / float(np.sqrt(QK_DIM))
BF16_MARGIN = (1.0 + 2.0 ** -8) ** 2
SAFE_EXP2_SPAN = 80.0


def _dot(a, b):
    return jnp.dot(a, b, preferred_element_type=F32)


def _dot_nt(a, b):
    return lax.dot_general(a, b, (((1,), (1,)), ((), ())), preferred_element_type=F32)


def _dot_tn(a, b):
    return lax.dot_general(a, b, (((0,), (0,)), ((), ())), preferred_element_type=F32)


def _twice_sigmoid_of_twice(h):
    return 1.0 + jnp.tanh(h)


def _rms(x, width):
    return lax.rsqrt(jnp.sum(x * x, axis=-1, keepdims=True) * (1.0 / width) + EPS)


def _const_spec(shape):
    nd = len(shape)
    return pl.BlockSpec(shape, lambda *_: (0,) * nd, pipeline_mode=pl.Buffered(1))


def _params(sem):
    return pltpu.CompilerParams(dimension_semantics=sem, vmem_limit_bytes=VMEM_LIMIT)


def _ada_kernel(c_ref, w_ref, b_ref, o_ref):
    c = c_ref[...]
    w = w_ref[...]
    c_hi = c.astype(BF16)
    c_lo = (c - c_hi.astype(F32)).astype(BF16)
    w_hi = w.astype(BF16)
    w_lo = (w - w_hi.astype(F32)).astype(BF16)
    o_ref[...] = _dot(c_hi, w_hi) + _dot(c_hi, w_lo) + _dot(c_lo, w_hi) + b_ref[...]


def _ada_mod(c, w_ada, b_ada, layer):
    bsz, d = c.shape
    depth, _, n = w_ada.shape
    bn = 1024
    return pl.pallas_call(
        _ada_kernel,
        grid=(n // bn,),
        in_specs=[pl.BlockSpec((bsz, d), lambda j: (0, 0)),
                  pl.BlockSpec((None, d, bn), lambda j: (layer, 0, j)),
                  pl.BlockSpec((None, 1, bn), lambda j: (layer, 0, j))],
        out_specs=pl.BlockSpec((bsz, bn), lambda j: (0, j)),
        out_shape=jax.ShapeDtypeStruct((bsz, n), F32),
        compiler_params=_params(("arbitrary",)),
        name="ada_mod",
    )(c, w_ada, b_ada.reshape(depth, 1, n))


def _rope(t, cos_t, sin_lo, sin_hi):
    return (t * cos_t + pltpu.roll(t, LANES - HALF_ROPE, 1) * sin_lo
            + pltpu.roll(t, HALF_ROPE, 1) * sin_hi)


def _mixer_in_kernel(x_ref, pos_ref, invf_ref, mod_ref, g1_ref, wcq_ref, wckv_ref, wh_ref, wg_ref,
                     gqa_ref, wuq_ref, gkva_ref, wk_ref, wv_ref, gq_ref, gk_ref,
                     q_ref, k_ref, vt_ref, hq_ref, hf_ref, hi_ref, hg_ref, ga_ref, gb_ref):
    tm, d = x_ref.shape
    x = x_ref[...]
    h = (x * _rms(x, d)) * (g1_ref[...] * (1.0 + mod_ref[1:2, :])) + mod_ref[0:1, :]
    hb = h.astype(BF16)

    ang = invf_ref[...] * pos_ref[...].astype(F32)
    cos_h = jnp.cos(ang)
    sin_h = jnp.sin(ang)
    zeros = lambda r: jnp.zeros((r, tm), F32)
    cos_t = jnp.concatenate([jnp.ones((NOPE_DIM, tm), F32), cos_h, cos_h,
                             zeros(HEAD_PAD - QK_DIM)], axis=0).T
    sin_lo = jnp.concatenate([zeros(NOPE_DIM), -sin_h, zeros(HEAD_PAD - NOPE_DIM - HALF_ROPE)],
                             axis=0).T
    sin_hi = jnp.concatenate([zeros(NOPE_DIM + HALF_ROPE), sin_h, zeros(HEAD_PAD - QK_DIM)],
                             axis=0).T

    cq = _dot(hb, wcq_ref[...])
    ckv = _dot(hb, wckv_ref[...])
    hh = _dot(hb, wh_ref[:, :2 * HG_WIDTH])
    hq_ref[...] = (0.5 * hh[:, :HG_WIDTH]).astype(BF16)
    hf_ref[...] = 0.5 * hh[:, HG_WIDTH:]

    cqn = (cq * _rms(cq, Q_LORA) * gqa_ref[...]).astype(BF16)
    ckv_c = ckv[:, :KV_LORA]
    kr = ckv[:, KV_LORA:]
    ckvn = (ckv_c * _rms(ckv_c, KV_LORA) * gkva_ref[...]).astype(BF16)
    qf = _dot(cqn, wuq_ref[...])
    kf = _dot(ckvn, wk_ref[...])
    vt = _dot_nt(wv_ref[...], ckvn)
    hh = _dot(hb, wh_ref[:, 2 * HG_WIDTH:])
    hi_ref[...] = hh[:, :HG_WIDTH].astype(BF16)
    hg_ref[...] = (0.5 * hh[:, HG_WIDTH:]).astype(BF16)
    gg = _dot(hb, wg_ref[...])
    ga_ref[...] = (0.5 * gg[:, :d]).astype(BF16)
    gb_ref[...] = (0.5 * gg[:, d:]).astype(BF16)

    gq = gq_ref[...]
    for hd in range(MLA_HEADS):
        sl = slice(hd * HEAD_PAD, (hd + 1) * HEAD_PAD)
        qh = qf[:, sl]
        qn = qh * _rms(qh, QK_DIM) * gq
        q_ref[:, sl] = (_rope(qn, cos_t, sin_lo, sin_hi) * Q_SCALE).astype(BF16)

    gk = gk_ref[...]
    ss_rope = jnp.sum(kr * kr, axis=-1, keepdims=True)
    kr_rot = _rope(kr * gk, cos_t, sin_lo, sin_hi)
    for hd in range(MLA_HEADS):
        sl = slice(hd * HEAD_PAD, (hd + 1) * HEAD_PAD)
        kn = kf[:, sl]
        ss = jnp.sum(kn * kn, axis=-1, keepdims=True) + ss_rope
        k_ref[:, sl] = ((kn * gk + kr_rot) * lax.rsqrt(ss * (1.0 / QK_DIM) + EPS)).astype(BF16)
    first_sub = lax.broadcasted_iota(jnp.int32, (SUBLANES, tm), 0) == 0
    pieces = []
    for hd in range(MLA_HEADS):
        r0 = hd * HEAD_PAD
        pieces += [vt[r0:r0 + V_DIM],
                   jnp.where(first_sub, 1.0, vt[r0 + V_DIM:r0 + V_DIM + SUBLANES]),
                   vt[r0 + V_DIM + SUBLANES:r0 + HEAD_PAD]]
    vt_ref[...] = jnp.concatenate(pieces, axis=0).astype(BF16)


def _mixer_in(x2, pos, invf, mod, g1, wcq, wckv, wh, wg, gqa, wuq, gkva, wk, wv, gq, gk, *,
              seq, tm, tk):
    t, d = x2.shape
    tps = seq // tm
    per_kv = tk // tm
    hw = MLA_HEADS * HEAD_PAD
    row = lambda w: pl.BlockSpec((tm, w), lambda i: (i, 0))
    vt_spec = pl.BlockSpec((None, None, hw, tm),
                           lambda i: (i // tps, (i % tps) // per_kv, 0, (i % tps) % per_kv))
    consts = [g1, wcq, wckv, wh, wg, gqa, wuq, gkva, wk, wv, gq, gk]
    out_shapes = [jax.ShapeDtypeStruct((t, hw), BF16)] * 2 + [
        jax.ShapeDtypeStruct((t // seq, seq // tk, hw, tk), BF16)] + [
        jax.ShapeDtypeStruct((t, HG_WIDTH), BF16), jax.ShapeDtypeStruct((t, HG_WIDTH), F32),
        jax.ShapeDtypeStruct((t, HG_WIDTH), BF16), jax.ShapeDtypeStruct((t, HG_WIDTH), BF16),
        jax.ShapeDtypeStruct((t, d), BF16), jax.ShapeDtypeStruct((t, d), BF16)]
    return pl.pallas_call(
        _mixer_in_kernel,
        grid=(t // tm,),
        in_specs=[row(d),
                  pl.BlockSpec((None, 1, tm), lambda i: (i, 0, 0)),
                  _const_spec(invf.shape),
                  pl.BlockSpec((None, 6, d), lambda i: (i // tps, 0, 0))]
                 + [_const_spec(a.shape) for a in consts],
        out_specs=[row(hw)] * 2 + [vt_spec] + [row(HG_WIDTH)] * 4 + [row(d)] * 2,
        out_shape=out_shapes,
        compiler_params=_params(("arbitrary",)),
        name="mixer_in",
    )(x2, pos, invf, mod, *consts)


def _attn_kernel(bound_ref, q_ref, k_ref, vt_ref, o_ref, m_scr, acc_scr, *, tq, tk, diag, heads):
    i = pl.program_id(2)
    nsub = tq // tk
    head_slices = [slice(hd * HEAD_PAD, (hd + 1) * HEAD_PAD) for hd in range(heads)]
    bound = bound_ref[0]

    def run(bounded):
        acc_scr[...] = jnp.zeros(acc_scr.shape, F32)
        if not bounded:
            m_scr[...] = jnp.full(m_scr.shape, NEG_BIG, F32)

        def block(j, off, nkv, row0, diagonal):
            start = pl.multiple_of(j * tk + off, nkv)
            stack = lambda a, n: jnp.concatenate([a] * (n // SUBLANES), axis=0)
            rows = slice(row0, tq)

            def scores(hd):
                sl = head_slices[hd]
                s = _dot_nt(k_ref[pl.ds(start, nkv), sl], q_ref[rows, sl])
                if diagonal:
                    kv_i = lax.broadcasted_iota(jnp.int32, s.shape, 0)
                    q_i = lax.broadcasted_iota(jnp.int32, s.shape, 1)
                    s = jnp.where(kv_i <= q_i, s, NEG_BIG)
                return s

            def update(hd, s):
                vt = vt_ref[j, head_slices[hd], off:off + nkv]
                if bounded:
                    acc_scr[hd, :, rows] += _dot(vt, jnp.exp2(s - bound).astype(BF16))
                    return
                m_prev = m_scr[hd, :, rows]
                m_new = jnp.maximum(m_prev, jnp.max(s, axis=0, keepdims=True))
                p = jnp.exp2(s - stack(m_new, nkv)).astype(BF16)
                acc_scr[hd, :, rows] = (acc_scr[hd, :, rows]
                                        * stack(jnp.exp2(m_prev - m_new), HEAD_PAD) + _dot(vt, p))
                m_scr[hd, :, rows] = m_new

            ahead = 2
            pending = [scores(hd) for hd in range(min(ahead, heads))]
            for hd in range(heads):
                if hd + ahead < heads:
                    pending.append(scores(hd + ahead))
                update(hd, pending.pop(0))

        def body(j, carry):
            for r in range(nsub):
                block(j * nsub + r, 0, tk, 0, False)
            return carry

        lax.fori_loop(0, i, body, 0)
        for pos in range(0, tq, diag):
            block(i * nsub + pos // tk, pos % tk, diag, pos, True)
        outs = []
        for hd in range(0, heads, 2):
            pair = [acc_scr[h, :V_DIM, :] / acc_scr[h, V_DIM:V_DIM + 1, :] for h in (hd, hd + 1)]
            outs.append(jnp.concatenate(pair, axis=0).T)
        o_ref[...] = jnp.concatenate(outs, axis=-1).astype(o_ref.dtype)

    no_underflow = 2.0 * bound <= SAFE_EXP2_SPAN
    pl.when(no_underflow)(lambda: run(True))
    pl.when(jnp.logical_not(no_underflow))(lambda: run(False))


def _mla_attention(score_bound, q, k, vt, *, tq, tk, diag, heads):
    bsz, seq, hw = q.shape
    groups = MLA_HEADS // heads
    pw = heads * HEAD_PAD
    return pl.pallas_call(
        functools.partial(_attn_kernel, tq=tq, tk=tk, diag=diag, heads=heads),
        grid=(bsz, groups, seq // tq),
        in_specs=[pl.BlockSpec(memory_space=pltpu.SMEM),
                  pl.BlockSpec((None, tq, pw), lambda b, h, i: (b, i, h)),
                  pl.BlockSpec((None, seq, pw), lambda b, h, i: (b, 0, h)),
                  pl.BlockSpec((None, seq // tk, pw, tk), lambda b, h, i: (b, 0, h, 0))],
        out_specs=pl.BlockSpec((None, tq, heads * V_DIM), lambda b, h, i: (b, i, h)),
        out_shape=jax.ShapeDtypeStruct((bsz, seq, MLA_HEADS * V_DIM), BF16),
        scratch_shapes=[pltpu.VMEM((heads, SUBLANES, tq), F32),
                        pltpu.VMEM((heads, HEAD_PAD, tq), F32)],
        compiler_params=_params(("arbitrary", "arbitrary", "arbitrary")),
        name="mla_attn",
    )(score_bound, q, k, vt)


def _hgrn_sum_masks(c):
    t = np.arange(c)[:, None]
    s = np.arange(c)[None, :]
    blocks = [s <= t, s > t]
    blk = c // 2
    while blk >= 2:
        bnd = (t & ~(2 * blk - 1)) + blk - 1
        blocks.append(((s > bnd) & (s <= t)) | ((s > t) & (s <= bnd)))
        blk //= 2
    em = np.concatenate(blocks, axis=0).astype(np.float32)
    return np.concatenate([em, em], axis=1)


def _hgrn_kernel(hq_ref, hf_ref, hi_ref, hg_ref, lbt_ref, gout_ref, em_ref, o_ref, st_ref, *,
                 layer, chunk):
    @pl.when(pl.program_id(1) == 0)
    def _():
        st_ref[...] = jnp.zeros_like(st_ref)

    for sub in range(hq_ref.shape[0] // chunk):
        rs = pl.ds(sub * chunk, chunk)
        _hgrn_chunk(hq_ref.at[rs], hf_ref.at[rs], hi_ref.at[rs], hg_ref.at[rs], lbt_ref, gout_ref,
                    em_ref, o_ref.at[rs], st_ref, layer=layer)


def _hgrn_chunk(hq_ref, hf_ref, hi_ref, hg_ref, lbt_ref, gout_ref, em_ref, o_ref, st_ref, *, layer):
    c, w = hq_ref.shape
    tab = lbt_ref[...]
    e = jnp.exp(tab - jnp.max(tab, axis=0, keepdims=True))
    lb = jnp.sum(e[1:layer + 2], axis=0, keepdims=True) / jnp.sum(e, axis=0, keepdims=True)

    hq = hq_ref[...].astype(F32)
    q = hq * _twice_sigmoid_of_twice(hq)
    f = lb + (0.5 - 0.5 * lb) * _twice_sigmoid_of_twice(hf_ref[...])
    lf = jnp.log(f)
    kk = 1.0 - f
    v = hi_ref[...]

    lf_hi = lf.astype(BF16)
    lf_mid = (lf - lf_hi.astype(F32)).astype(BF16)
    sums = _dot(em_ref[...], jnp.concatenate([lf_hi, lf_mid], axis=0))
    b = sums[0:c]
    q_dec = (q * jnp.exp(b)).astype(BF16)
    k_last = (kk * jnp.exp(sums[c:2 * c])).astype(BF16)
    b_last = b[c - 1:c, :]

    n_levels = int(np.log2(c))
    sub = lax.broadcasted_iota(jnp.int32, (SUBLANES, w), 0)
    xs = []
    for lvl in range(n_levels):
        blk = 1 << lvl
        if blk >= SUBLANES:
            base = jnp.concatenate([(q if (j % 2) else kk)[j * blk:(j + 1) * blk]
                                    for j in range(c // blk)], axis=0)
        else:
            odd = jnp.concatenate([(sub & blk) != 0] * (c // SUBLANES), axis=0)
            base = jnp.where(odd, q * f, kk) if blk == 1 else jnp.where(odd, q, kk)
        if blk > 1:
            row0 = (2 + n_levels - 1 - lvl) * c
            base = base * jnp.exp(sums[row0:row0 + c])
        xs.append(base.astype(BF16))

    t_i = lax.broadcasted_iota(jnp.int32, (c, c), 0)
    s_i = lax.broadcasted_iota(jnp.int32, (c, c), 1)
    lower_xor = jnp.where(t_i > s_i, t_i ^ s_i, 0)
    level_of = [(lower_xor >> lvl) == 1 for lvl in range(n_levels)]
    on_diag = t_i == s_i
    qk = q * kk
    gout = gout_ref[...]
    head_slices = [slice(hd * HG_DIM, (hd + 1) * HG_DIM) for hd in range(HG_HEADS)]
    attns = [jnp.where(on_diag, jnp.sum(qk[:, hs], axis=-1, keepdims=True), 0.0)
             for hs in head_slices]
    for x, keep in zip(xs, level_of):
        attns = [jnp.where(keep, _dot_nt(x[:, hs], x[:, hs]), a) for hs, a in zip(head_slices, attns)]
    for hd, hs in enumerate(head_slices):
        attn = attns[hd]
        st = st_ref[hd]
        o = _dot_nt(q_dec[:, hs], st.astype(BF16)) + _dot(attn.astype(BF16), v[:, hs])
        hg = hg_ref[:, hs].astype(F32)
        o = o * _rms(o, HG_DIM) * gout * (hg * _twice_sigmoid_of_twice(hg))
        o_ref[:, hs] = o.astype(o_ref.dtype)
        st_ref[hd] = st * jnp.exp(b_last[:, hs]) + _dot_tn(v[:, hs], k_last[:, hs])


def _hgrn2(hq, hf, hi, hg, lb_table, gout, *, chunk, chunks_per_step, layer):
    bsz, seq, w = hq.shape
    rows = chunk * chunks_per_step
    blk = pl.BlockSpec((None, rows, w), lambda b, c: (b, c, 0))
    em = jnp.asarray(_hgrn_sum_masks(chunk), dtype=BF16)
    return pl.pallas_call(
        functools.partial(_hgrn_kernel, layer=layer, chunk=chunk),
        grid=(bsz, seq // rows),
        in_specs=[blk, blk, blk, blk, _const_spec(lb_table.shape), _const_spec(gout.shape),
                  _const_spec(em.shape)],
        out_specs=blk,
        out_shape=jax.ShapeDtypeStruct((bsz, seq, w), BF16),
        scratch_shapes=[pltpu.VMEM((HG_HEADS, HG_DIM, HG_DIM), F32)],
        compiler_params=_params(("arbitrary", "arbitrary")),
        name="hgrn2",
    )(hq, hf, hi, hg, lb_table, gout, em)


def _merge_ffn_kernel(a_ref, o_ref, ga_ref, gb_ref, x_ref, modn_ref, modp_ref, wa_ref, wb_ref, wo_ref,
                      g2_ref, wup_ref, cw_ref, cb_ref, wdn_ref, out_ref, x1_scr, h2_scr, up_scr, *, tps):
    tm, d = x_ref.shape
    dff = wdn_ref.shape[0]
    halo = SUBLANES
    i = pl.program_id(0)
    nxt = i % 2
    prv = 1 - nxt

    @pl.when(i == 0)
    def _():
        rows = 2 * SUBLANES

        def clear(r, carry):
            start = pl.multiple_of(r * rows, rows)
            x1_scr[prv, pl.ds(start, rows), :] = jnp.zeros((rows, d), F32)
            h2_scr[prv, pl.ds(start, rows), :] = jnp.zeros((rows, d), BF16)
            return carry

        lax.fori_loop(0, tm // rows, clear, 0)

    @pl.when(jnp.logical_or(i == 0, (i + tps - 1) % tps == 0))
    def _():
        up_scr[0:halo, :] = jnp.zeros((halo, up_scr.shape[1]), F32)

    ya = _dot(a_ref[...], wa_ref[...])
    yb = _dot(o_ref[...], wb_ref[...])
    h2p = h2_scr[prv]

    def conv_block(cols):
        up = _dot(h2p, wup_ref[:, cols])
        up_scr[halo:halo + tm, cols] = up
        y = cb_ref[:, cols] + cw_ref[CONV_W - 1:CONV_W, cols] * up
        for j in range(CONV_W - 1):
            back = CONV_W - 1 - j
            y = y + cw_ref[j:j + 1, cols] * up_scr[halo - back:halo - back + tm, cols]
        up_scr[0:halo, cols] = up[tm - halo:, :]
        return y

    acts = []
    for off in range(0, dff, FF_BLOCK):
        gate = conv_block(slice(off, off + FF_BLOCK))
        val = conv_block(slice(dff + off, dff + off + FF_BLOCK))
        acts.append((gate * val * _twice_sigmoid_of_twice(gate)).astype(BF16))
    act = jnp.concatenate(acts, axis=1)
    out_ref[...] = x1_scr[prv] + modp_ref[5:6, :] * _dot(act, wdn_ref[...])

    merged = (_twice_sigmoid_of_twice(ga_ref[...].astype(F32)) * ya
              + _twice_sigmoid_of_twice(gb_ref[...].astype(F32)) * yb)
    z1 = _dot(merged.astype(BF16), wo_ref[...])
    x1 = x_ref[...] + modn_ref[2:3, :] * z1
    x1_scr[nxt] = x1
    h2 = (x1 * _rms(x1, d)) * (g2_ref[...] * (1.0 + modn_ref[4:5, :])) + modn_ref[3:4, :]
    h2_scr[nxt] = h2.astype(BF16)


def _merge_ffn(attn, hgo, ga, gb, x2, mod, wa, wb, wo, g2, wup, cw, cb, wdn, *, seq, tm):
    t, d = x2.shape
    tps = seq // tm
    n = t // tm
    cur = lambda i: jnp.minimum(i, n - 1)
    prev = lambda i: jnp.maximum(i - 1, 0)
    row = lambda w: pl.BlockSpec((tm, w), lambda i: (cur(i), 0))
    consts = [wa, wb, wo, g2, wup, cw, cb, wdn]
    return pl.pallas_call(
        functools.partial(_merge_ffn_kernel, tps=tps),
        grid=(n + 1,),
        in_specs=[row(attn.shape[1]), row(hgo.shape[1]), row(d), row(d), row(d),
                  pl.BlockSpec((None, 6, d), lambda i: (cur(i) // tps, 0, 0)),
                  pl.BlockSpec((None, 6, d), lambda i: (prev(i) // tps, 0, 0))]
                 + [_const_spec(a.shape) for a in consts],
        out_specs=pl.BlockSpec((tm, d), lambda i: (prev(i), 0)),
        out_shape=jax.ShapeDtypeStruct((t, d), F32),
        scratch_shapes=[pltpu.VMEM((2, tm, d), F32), pltpu.VMEM((2, tm, d), BF16),
                        pltpu.VMEM((tm + SUBLANES, wup.shape[1]), F32)],
        compiler_params=_params(("arbitrary",)),
        name="merge_ffn",
    )(attn, hgo, ga, gb, x2, mod, mod, *consts)


def _pad_heads(w, used, n_heads):
    rows = w.shape[0]
    w = w.reshape(rows, n_heads, used)
    w = jnp.pad(w, ((0, 0), (0, 0), (0, HEAD_PAD - used)))
    return w.reshape(rows, n_heads * HEAD_PAD)


class _Tiles(NamedTuple):
    token: int
    q: int
    kv: int
    diag: int
    attn_heads: int
    chunk: int
    chunks_per_step: int


def _tiles(seq):
    tl = _Tiles(token=min(256, seq), q=min(2048, seq), kv=min(512, seq), diag=min(256, seq),
                attn_heads=4, chunk=min(128, seq), chunks_per_step=max(1, min(4, seq // 128)))
    assert seq % tl.token == 0 and seq % tl.q == 0 and tl.q % tl.kv == 0 and tl.kv % tl.diag == 0
    assert tl.kv % tl.token == 0 and seq % (tl.chunk * tl.chunks_per_step) == 0
    assert MLA_HEADS % tl.attn_heads == 0
    return tl


def _layer(x2, pos, invf, mod, p, *, bsz, seq, layer):
    t, d = x2.shape
    tl = _tiles(seq)
    w_in = p["w_in"]
    o_ckv = Q_LORA
    o_h = o_ckv + KV_LORA + ROPE_DIM
    o_g = o_h + 4 * HG_WIDTH
    wcq = w_in[:, :o_ckv].astype(BF16)
    wckv = jnp.concatenate([w_in[:, o_ckv:o_ckv + KV_LORA],
                            jnp.zeros((d, NOPE_DIM), F32),
                            w_in[:, o_ckv + KV_LORA:o_h],
                            jnp.zeros((d, HEAD_PAD - QK_DIM), F32)], axis=1).astype(BF16)
    wh = w_in[:, o_h:o_g].astype(BF16)
    wg = w_in[:, o_g:].astype(BF16)
    wuq = _pad_heads(p["w_uq"], QK_DIM, MLA_HEADS).astype(BF16)
    wukv = p["w_ukv"].reshape(KV_LORA, MLA_HEADS, NOPE_DIM + V_DIM)
    wk = _pad_heads(wukv[:, :, :NOPE_DIM].reshape(KV_LORA, -1), NOPE_DIM, MLA_HEADS).astype(BF16)
    wvt = _pad_heads(wukv[:, :, NOPE_DIM:].reshape(KV_LORA, -1), V_DIM, MLA_HEADS).T.astype(BF16)
    pad_g = lambda g: jnp.pad(g, (0, HEAD_PAD - QK_DIM)).reshape(1, HEAD_PAD)

    tm = tl.token
    q, k, vt, hq, hf, hi, hg, ga, gb = _mixer_in(
        x2, pos.reshape(t // tm, 1, tm), invf, mod, p["norm1_g"].reshape(1, d), wcq, wckv, wh,
        wg, p["q_a_norm_g"].reshape(1, -1), wuq, p["kv_a_norm_g"].reshape(1, -1), wk, wvt,
        pad_g(p["q_norm_g"]), pad_g(p["k_norm_g"]), seq=seq, tm=tm, tk=tl.kv)

    hw = MLA_HEADS * HEAD_PAD
    score_bound = (BF16_MARGIN * QK_DIM * Q_SCALE * jnp.max(jnp.abs(p["q_norm_g"]))
                   * jnp.max(jnp.abs(p["k_norm_g"]))).reshape(1).astype(F32)
    attn = _mla_attention(score_bound, q.reshape(bsz, seq, hw), k.reshape(bsz, seq, hw), vt,
                          tq=tl.q, tk=tl.kv, diag=tl.diag, heads=tl.attn_heads)
    r3 = lambda a: a.reshape(bsz, seq, HG_WIDTH)
    hgo = _hgrn2(r3(hq), r3(hf), r3(hi), r3(hg), p["hg_lower_bound"],
                 p["hg_out_norm_g"].reshape(1, HG_DIM), chunk=tl.chunk,
                 chunks_per_step=tl.chunks_per_step, layer=layer)

    dff = p["w_down"].shape[0]
    halve_gate = jnp.asarray(np.repeat(np.float32([0.5, 1.0]), dff))
    return _merge_ffn(attn.reshape(t, -1), hgo.reshape(t, -1), ga, gb, x2, mod,
                      p["w_branch_a"].astype(BF16), p["w_branch_b"].astype(BF16),
                      (p["w_out"] * 0.5).astype(BF16), p["norm2_g"].reshape(1, d),
                      p["w_up"].astype(BF16), p["conv_w"] * halve_gate,
                      (p["conv_b"] * halve_gate).reshape(1, -1),
                      p["w_down"].astype(BF16), seq=seq, tm=tm)


def kernel(x, c, positions, w_ada, b_ada, norm1_g, w_in, q_a_norm_g, w_uq, kv_a_norm_g, w_ukv,
           q_norm_g, k_norm_g, hg_lower_bound, hg_out_norm_g, w_branch_a, w_branch_b, w_out,
           norm2_g, w_up, conv_w, conv_b, w_down):
    bsz, seq, d = x.shape
    depth = w_ada.shape[0]
    invf = (ROPE_THETA ** (-jnp.arange(0, ROPE_DIM, 2, dtype=F32) / ROPE_DIM)).reshape(HALF_ROPE, 1)
    x2 = x.reshape(bsz * seq, d)
    pos = positions.reshape(bsz * seq)
    for l in range(depth):
        mod = _ada_mod(c, w_ada, b_ada, l).reshape(bsz, 6, d)
        p = dict(norm1_g=norm1_g[l], w_in=w_in[l], q_a_norm_g=q_a_norm_g[l], w_uq=w_uq[l],
                 kv_a_norm_g=kv_a_norm_g[l], w_ukv=w_ukv[l], q_norm_g=q_norm_g[l],
                 k_norm_g=k_norm_g[l], hg_lower_bound=hg_lower_bound, hg_out_norm_g=hg_out_norm_g[l],
                 w_branch_a=w_branch_a[l], w_branch_b=w_branch_b[l], w_out=w_out[l],
                 norm2_g=norm2_g[l], w_up=w_up[l], conv_w=conv_w[l], conv_b=conv_b[l],
                 w_down=w_down[l])
        x2 = _layer(x2, pos, invf, mod, p, bsz=bsz, seq=seq, layer=l)
    return x2.reshape(bsz, seq, d)
```

```python
import functools
from typing import NamedTuple

import numpy as np
import jax
import jax.numpy as jnp
from jax import lax
from jax.experimental import pallas as pl
from jax.experimental.pallas import tpu as pltpu

F32 = jnp.float32
BF16 = jnp.bfloat16

LANES = 128
SUBLANES = 8
VMEM_LIMIT = 56 * 1024 * 1024

MLA_HEADS = 8
NOPE_DIM = 64
ROPE_DIM = 32
HALF_ROPE = ROPE_DIM // 2
QK_DIM = NOPE_DIM + ROPE_DIM
V_DIM = 64
Q_LORA = 512
KV_LORA = 256
ROPE_THETA = 10000.0
HG_HEADS = 4
HG_DIM = 128
HG_WIDTH = HG_HEADS * HG_DIM
CONV_W = 3
EPS = 1e-6
HEAD_PAD = LANES
FF_BLOCK = 2 * LANES
LOG2E = float(np.log2(np.e))
NEG_BIG = -1e30
Q_SCALE = LOG2E / float(np.sqrt(QK_DIM))
BF16_MARGIN = (1.0 + 2.0 ** -8) ** 2
SAFE_EXP2_SPAN = 80.0


def _dot(a, b):
    return jnp.dot(a, b, preferred_element_type=F32)


def _dot_nt(a, b):
    return lax.dot_general(a, b, (((1,), (1,)), ((), ())), preferred_element_type=F32)


def _dot_tn(a, b):
    return lax.dot_general(a, b, (((0,), (0,)), ((), ())), preferred_element_type=F32)


def _twice_sigmoid_of_twice(h):
    return 1.0 + jnp.tanh(h)


def _rms(x, width):
    return lax.rsqrt(jnp.sum(x * x, axis=-1, keepdims=True) * (1.0 / width) + EPS)


def _const_spec(shape):
    nd = len(shape)
    return pl.BlockSpec(shape, lambda *_: (0,) * nd, pipeline_mode=pl.Buffered(1))


def _params(sem):
    return pltpu.CompilerParams(dimension_semantics=sem, vmem_limit_bytes=VMEM_LIMIT)


def _ada_kernel(c_ref, w_ref, b_ref, o_ref):
    c = c_ref[...]
    w = w_ref[...]
    c_hi = c.astype(BF16)
    c_lo = (c - c_hi.astype(F32)).astype(BF16)
    w_hi = w.astype(BF16)
    w_lo = (w - w_hi.astype(F32)).astype(BF16)
    o_ref[...] = _dot(c_hi, w_hi) + _dot(c_hi, w_lo) + _dot(c_lo, w_hi) + b_ref[...]


def _ada_mod(c, w_ada, b_ada, layer):
    bsz, d = c.shape
    depth, _, n = w_ada.shape
    bn = 1024
    return pl.pallas_call(
        _ada_kernel,
        grid=(n // bn,),
        in_specs=[pl.BlockSpec((bsz, d), lambda j: (0, 0)),
                  pl.BlockSpec((None, d, bn), lambda j: (layer, 0, j)),
                  pl.BlockSpec((None, 1, bn), lambda j: (layer, 0, j))],
        out_specs=pl.BlockSpec((bsz, bn), lambda j: (0, j)),
        out_shape=jax.ShapeDtypeStruct((bsz, n), F32),
        compiler_params=_params(("arbitrary",)),
        name="ada_mod",
    )(c, w_ada, b_ada.reshape(depth, 1, n))


def _rope(t, cos_t, sin_lo, sin_hi):
    return (t * cos_t + pltpu.roll(t, LANES - HALF_ROPE, 1) * sin_lo
            + pltpu.roll(t, HALF_ROPE, 1) * sin_hi)


def _mixer_in_kernel(x_ref, pos_ref, invf_ref, mod_ref, g1_ref, wcq_ref, wckv_ref, wh_ref, wg_ref,
                     gqa_ref, wuq_ref, gkva_ref, wk_ref, wv_ref, gq_ref, gk_ref,
                     q_ref, k_ref, vt_ref, hq_ref, hf_ref, hi_ref, hg_ref, ga_ref, gb_ref):
    tm, d = x_ref.shape
    x = x_ref[...]
    h = (x * _rms(x, d)) * (g1_ref[...] * (1.0 + mod_ref[1:2, :])) + mod_ref[0:1, :]
    hb = h.astype(BF16)

    ang = invf_ref[...] * pos_ref[...].astype(F32)
    cos_h = jnp.cos(ang)
    sin_h = jnp.sin(ang)
    zeros = lambda r: jnp.zeros((r, tm), F32)
    cos_t = jnp.concatenate([jnp.ones((NOPE_DIM, tm), F32), cos_h, cos_h,
                             zeros(HEAD_PAD - QK_DIM)], axis=0).T
    sin_lo = jnp.concatenate([zeros(NOPE_DIM), -sin_h, zeros(HEAD_PAD - NOPE_DIM - HALF_ROPE)],
                             axis=0).T
    sin_hi = jnp.concatenate([zeros(NOPE_DIM + HALF_ROPE), sin_h, zeros(HEAD_PAD - QK_DIM)],
                             axis=0).T

    cq = _dot(hb, wcq_ref[...])
    ckv = _dot(hb, wckv_ref[...])
    hh = _dot(hb, wh_ref[:, :2 * HG_WIDTH])
    hq_ref[...] = (0.5 * hh[:, :HG_WIDTH]).astype(BF16)
    hf_ref[...] = 0.5 * hh[:, HG_WIDTH:]

    cqn = (cq * _rms(cq, Q_LORA) * gqa_ref[...]).astype(BF16)
    ckv_c = ckv[:, :KV_LORA]
    kr = ckv[:, KV_LORA:]
    ckvn = (ckv_c * _rms(ckv_c, KV_LORA) * gkva_ref[...]).astype(BF16)
    qf = _dot(cqn, wuq_ref[...])
    kf = _dot(ckvn, wk_ref[...])
    vt = _dot_nt(wv_ref[...], ckvn)
    hh = _dot(hb, wh_ref[:, 2 * HG_WIDTH:])
    hi_ref[...] = hh[:, :HG_WIDTH].astype(BF16)
    hg_ref[...] = (0.5 * hh[:, HG_WIDTH:]).astype(BF16)
    gg = _dot(hb, wg_ref[...])
    ga_ref[...] = (0.5 * gg[:, :d]).astype(BF16)
    gb_ref[...] = (0.5 * gg[:, d:]).astype(BF16)

    gq = gq_ref[...]
    for hd in range(MLA_HEADS):
        sl = slice(hd * HEAD_PAD, (hd + 1) * HEAD_PAD)
        qh = qf[:, sl]
        qn = qh * _rms(qh, QK_DIM) * gq
        q_ref[:, sl] = (_rope(qn, cos_t, sin_lo, sin_hi) * Q_SCALE).astype(BF16)

    gk = gk_ref[...]
    ss_rope = jnp.sum(kr * kr, axis=-1, keepdims=True)
    kr_rot = _rope(kr * gk, cos_t, sin_lo, sin_hi)
    for hd in range(MLA_HEADS):
        sl = slice(hd * HEAD_PAD, (hd + 1) * HEAD_PAD)
        kn = kf[:, sl]
        ss = jnp.sum(kn * kn, axis=-1, keepdims=True) + ss_rope
        k_ref[:, sl] = ((kn * gk + kr_rot) * lax.rsqrt(ss * (1.0 / QK_DIM) + EPS)).astype(BF16)
    first_sub = lax.broadcasted_iota(jnp.int32, (SUBLANES, tm), 0) == 0
    pieces = []
    for hd in range(MLA_HEADS):
        r0 = hd * HEAD_PAD
        pieces += [vt[r0:r0 + V_DIM],
                   jnp.where(first_sub, 1.0, vt[r0 + V_DIM:r0 + V_DIM + SUBLANES]),
                   vt[r0 + V_DIM + SUBLANES:r0 + HEAD_PAD]]
    vt_ref[...] = jnp.concatenate(pieces, axis=0).astype(BF16)


def _mixer_in(x2, pos, invf, mod, g1, wcq, wckv, wh, wg, gqa, wuq, gkva, wk, wv, gq, gk, *,
              seq, tm, tk):
    t, d = x2.shape
    tps = seq // tm
    per_kv = tk // tm
    hw = MLA_HEADS * HEAD_PAD
    row = lambda w: pl.BlockSpec((tm, w), lambda i: (i, 0))
    vt_spec = pl.BlockSpec((None, None, hw, tm),
                           lambda i: (i // tps, (i % tps) // per_kv, 0, (i % tps) % per_kv))
    consts = [g1, wcq, wckv, wh, wg, gqa, wuq, gkva, wk, wv, gq, gk]
    out_shapes = [jax.ShapeDtypeStruct((t, hw), BF16)] * 2 + [
        jax.ShapeDtypeStruct((t // seq, seq // tk, hw, tk), BF16)] + [
        jax.ShapeDtypeStruct((t, HG_WIDTH), BF16), jax.ShapeDtypeStruct((t, HG_WIDTH), F32),
        jax.ShapeDtypeStruct((t, HG_WIDTH), BF16), jax.ShapeDtypeStruct((t, HG_WIDTH), BF16),
        jax.ShapeDtypeStruct((t, d), BF16), jax.ShapeDtypeStruct((t, d), BF16)]
    return pl.pallas_call(
        _mixer_in_kernel,
        grid=(t // tm,),
        in_specs=[row(d),
                  pl.BlockSpec((None, 1, tm), lambda i: (i, 0, 0)),
                  _const_spec(invf.shape),
                  pl.BlockSpec((None, 6, d), lambda i: (i // tps, 0, 0))]
                 + [_const_spec(a.shape) for a in consts],
        out_specs=[row(hw)] * 2 + [vt_spec] + [row(HG_WIDTH)] * 4 + [row(d)] * 2,
        out_shape=out_shapes,
        compiler_params=_params(("arbitrary",)),
        name="mixer_in",
    )(x2, pos, invf, mod, *consts)


def _attn_kernel(bound_ref, q_ref, k_ref, vt_ref, o_ref, m_scr, acc_scr, *, tq, tk, diag, heads):
    i = pl.program_id(2)
    nsub = tq // tk
    head_slices = [slice(hd * HEAD_PAD, (hd + 1) * HEAD_PAD) for hd in range(heads)]
    bound = bound_ref[0]

    def run(bounded):
        acc_scr[...] = jnp.zeros(acc_scr.shape, F32)
        if not bounded:
            m_scr[...] = jnp.full(m_scr.shape, NEG_BIG, F32)

        def block(j, off, nkv, row0, diagonal):
            start = pl.multiple_of(j * tk + off, nkv)
            stack = lambda a, n: jnp.concatenate([a] * (n // SUBLANES), axis=0)
            rows = slice(row0, tq)

            def scores(hd):
                sl = head_slices[hd]
                s = _dot_nt(k_ref[pl.ds(start, nkv), sl], q_ref[rows, sl])
                if diagonal:
                    kv_i = lax.broadcasted_iota(jnp.int32, s.shape, 0)
                    q_i = lax.broadcasted_iota(jnp.int32, s.shape, 1)
                    s = jnp.where(kv_i <= q_i, s, NEG_BIG)
                return s

            def update(hd, s):
                vt = vt_ref[j, head_slices[hd], off:off + nkv]
                if bounded:
                    acc_scr[hd, :, rows] += _dot(vt, jnp.exp2(s - bound).astype(BF16))
                    return
                m_prev = m_scr[hd, :, rows]
                m_new = jnp.maximum(m_prev, jnp.max(s, axis=0, keepdims=True))
                p = jnp.exp2(s - stack(m_new, nkv)).astype(BF16)
                acc_scr[hd, :, rows] = (acc_scr[hd, :, rows]
                                        * stack(jnp.exp2(m_prev - m_new), HEAD_PAD) + _dot(vt, p))
                m_scr[hd, :, rows] = m_new

            ahead = 2
            pending = [scores(hd) for hd in range(min(ahead, heads))]
            for hd in range(heads):
                if hd + ahead < heads:
                    pending.append(scores(hd + ahead))
                update(hd, pending.pop(0))

        def body(j, carry):
            for r in range(nsub):
                block(j * nsub + r, 0, tk, 0, False)
            return carry

        lax.fori_loop(0, i, body, 0)
        for pos in range(0, tq, diag):
            block(i * nsub + pos // tk, pos % tk, diag, pos, True)
        outs = []
        for hd in range(0, heads, 2):
            pair = [acc_scr[h, :V_DIM, :] / acc_scr[h, V_DIM:V_DIM + 1, :] for h in (hd, hd + 1)]
            outs.append(jnp.concatenate(pair, axis=0).T)
        o_ref[...] = jnp.concatenate(outs, axis=-1).astype(o_ref.dtype)

    no_underflow = 2.0 * bound <= SAFE_EXP2_SPAN
    pl.when(no_underflow)(lambda: run(True))
    pl.when(jnp.logical_not(no_underflow))(lambda: run(False))


def _mla_attention(score_bound, q, k, vt, *, tq, tk, diag, heads):
    bsz, seq, hw = q.shape
    groups = MLA_HEADS // heads
    pw = heads * HEAD_PAD
    return pl.pallas_call(
        functools.partial(_attn_kernel, tq=tq, tk=tk, diag=diag, heads=heads),
        grid=(bsz, groups, seq // tq),
        in_specs=[pl.BlockSpec(memory_space=pltpu.SMEM),
                  pl.BlockSpec((None, tq, pw), lambda b, h, i: (b, i, h)),
                  pl.BlockSpec((None, seq, pw), lambda b, h, i: (b, 0, h)),
                  pl.BlockSpec((None, seq // tk, pw, tk), lambda b, h, i: (b, 0, h, 0))],
        out_specs=pl.BlockSpec((None, tq, heads * V_DIM), lambda b, h, i: (b, i, h)),
        out_shape=jax.ShapeDtypeStruct((bsz, seq, MLA_HEADS * V_DIM), BF16),
        scratch_shapes=[pltpu.VMEM((heads, SUBLANES, tq), F32),
                        pltpu.VMEM((heads, HEAD_PAD, tq), F32)],
        compiler_params=_params(("arbitrary", "arbitrary", "arbitrary")),
        name="mla_attn",
    )(score_bound, q, k, vt)


def _hgrn_sum_masks(c):
    t = np.arange(c)[:, None]
    s = np.arange(c)[None, :]
    blocks = [s <= t, s > t]
    blk = c // 2
    while blk >= 2:
        bnd = (t & ~(2 * blk - 1)) + blk - 1
        blocks.append(((s > bnd) & (s <= t)) | ((s > t) & (s <= bnd)))
        blk //= 2
    em = np.concatenate(blocks, axis=0).astype(np.float32)
    return np.concatenate([em, em], axis=1)


def _hgrn_kernel(hq_ref, hf_ref, hi_ref, hg_ref, lbt_ref, gout_ref, em_ref, o_ref, st_ref, *,
                 layer, chunk):
    @pl.when(pl.program_id(1) == 0)
    def _():
        st_ref[...] = jnp.zeros_like(st_ref)

    for sub in range(hq_ref.shape[0] // chunk):
        rs = pl.ds(sub * chunk, chunk)
        _hgrn_chunk(hq_ref.at[rs], hf_ref.at[rs], hi_ref.at[rs], hg_ref.at[rs], lbt_ref, gout_ref,
                    em_ref, o_ref.at[rs], st_ref, layer=layer)


def _hgrn_chunk(hq_ref, hf_ref, hi_ref, hg_ref, lbt_ref, gout_ref, em_ref, o_ref, st_ref, *, layer):
    c, w = hq_ref.shape
    tab = lbt_ref[...]
    e = jnp.exp(tab - jnp.max(tab, axis=0, keepdims=True))
    lb = jnp.sum(e[1:layer + 2], axis=0, keepdims=True) / jnp.sum(e, axis=0, keepdims=True)

    hq = hq_ref[...].astype(F32)
    q = hq * _twice_sigmoid_of_twice(hq)
    f = lb + (0.5 - 0.5 * lb) * _twice_sigmoid_of_twice(hf_ref[...])
    lf = jnp.log2(f)
    kk = 1.0 - f
    v = hi_ref[...]

    lf_hi = lf.astype(BF16)
    lf_mid = (lf - lf_hi.astype(F32)).astype(BF16)
    sums = _dot(em_ref[...], jnp.concatenate([lf_hi, lf_mid], axis=0))
    b = sums[0:c]
    q_dec = (q * jnp.exp2(b)).astype(BF16)
    k_last = (kk * jnp.exp2(sums[c:2 * c])).astype(BF16)
    b_last = b[c - 1:c, :]

    n_levels = int(np.log2(c))
    sub = lax.broadcasted_iota(jnp.int32, (SUBLANES, w), 0)
    xs = []
    for lvl in range(n_levels):
        blk = 1 << lvl
        if blk >= SUBLANES:
            base = jnp.concatenate([(q if (j % 2) else kk)[j * blk:(j + 1) * blk]
                                    for j in range(c // blk)], axis=0)
        else:
            odd = jnp.concatenate([(sub & blk) != 0] * (c // SUBLANES), axis=0)
            base = jnp.where(odd, q * f, kk) if blk == 1 else jnp.where(odd, q, kk)
        if blk > 1:
            row0 = (2 + n_levels - 1 - lvl) * c
            base = base * jnp.exp2(sums[row0:row0 + c])
        xs.append(base.astype(BF16))

    t_i = lax.broadcasted_iota(jnp.int32, (c, c), 0)
    s_i = lax.broadcasted_iota(jnp.int32, (c, c), 1)
    lower_xor = jnp.where(t_i > s_i, t_i ^ s_i, 0)
    level_of = [(lower_xor >> lvl) == 1 for lvl in range(n_levels)]
    on_diag = t_i == s_i
    qk = q * kk
    gout = gout_ref[...]
    head_slices = [slice(hd * HG_DIM, (hd + 1) * HG_DIM) for hd in range(HG_HEADS)]
    attns = [jnp.where(on_diag, jnp.sum(qk[:, hs], axis=-1, keepdims=True), 0.0)
             for hs in head_slices]
    for x, keep in zip(xs, level_of):
        attns = [jnp.where(keep, _dot_nt(x[:, hs], x[:, hs]), a) for hs, a in zip(head_slices, attns)]
    for hd, hs in enumerate(head_slices):
        attn = attns[hd]
        st = st_ref[hd]
        o = _dot_nt(q_dec[:, hs], st.astype(BF16)) + _dot(attn.astype(BF16), v[:, hs])
        hg = hg_ref[:, hs].astype(F32)
        o = o * _rms(o, HG_DIM) * gout * (hg * _twice_sigmoid_of_twice(hg))
        o_ref[:, hs] = o.astype(o_ref.dtype)
        st_ref[hd] = st * jnp.exp2(b_last[:, hs]) + _dot_tn(v[:, hs], k_last[:, hs])


def _hgrn2(hq, hf, hi, hg, lb_table, gout, *, chunk, chunks_per_step, layer):
    bsz, seq, w = hq.shape
    rows = chunk * chunks_per_step
    blk = pl.BlockSpec((None, rows, w), lambda b, c: (b, c, 0))
    em = jnp.asarray(_hgrn_sum_masks(chunk), dtype=BF16)
    return pl.pallas_call(
        functools.partial(_hgrn_kernel, layer=layer, chunk=chunk),
        grid=(bsz, seq // rows),
        in_specs=[blk, blk, blk, blk, _const_spec(lb_table.shape), _const_spec(gout.shape),
                  _const_spec(em.shape)],
        out_specs=blk,
        out_shape=jax.ShapeDtypeStruct((bsz, seq, w), BF16),
        scratch_shapes=[pltpu.VMEM((HG_HEADS, HG_DIM, HG_DIM), F32)],
        compiler_params=_params(("arbitrary", "arbitrary")),
        name="hgrn2",
    )(hq, hf, hi, hg, lb_table, gout, em)


def _merge_ffn_kernel(a_ref, o_ref, ga_ref, gb_ref, x_ref, modn_ref, modp_ref, wa_ref, wb_ref, wo_ref,
                      g2_ref, wup_ref, cw_ref, cb_ref, wdn_ref, out_ref, x1_scr, h2_scr, up_scr, *, tps):
    tm, d = x_ref.shape
    dff = wdn_ref.shape[0]
    halo = SUBLANES
    i = pl.program_id(0)
    nxt = i % 2
    prv = 1 - nxt

    @pl.when(i == 0)
    def _():
        rows = 2 * SUBLANES

        def clear(r, carry):
            start = pl.multiple_of(r * rows, rows)
            x1_scr[prv, pl.ds(start, rows), :] = jnp.zeros((rows, d), F32)
            h2_scr[prv, pl.ds(start, rows), :] = jnp.zeros((rows, d), BF16)
            return carry

        lax.fori_loop(0, tm // rows, clear, 0)

    @pl.when(jnp.logical_or(i == 0, (i + tps - 1) % tps == 0))
    def _():
        up_scr[0:halo, :] = jnp.zeros((halo, up_scr.shape[1]), F32)

    ya = _dot(a_ref[...], wa_ref[...])
    yb = _dot(o_ref[...], wb_ref[...])
    h2p = h2_scr[prv]

    def conv_block(cols):
        up = _dot(h2p, wup_ref[:, cols])
        up_scr[halo:halo + tm, cols] = up
        y = cb_ref[:, cols] + cw_ref[CONV_W - 1:CONV_W, cols] * up
        for j in range(CONV_W - 1):
            back = CONV_W - 1 - j
            y = y + cw_ref[j:j + 1, cols] * up_scr[halo - back:halo - back + tm, cols]
        up_scr[0:halo, cols] = up[tm - halo:, :]
        return y

    acts = []
    for off in range(0, dff, FF_BLOCK):
        gate = conv_block(slice(off, off + FF_BLOCK))
        val = conv_block(slice(dff + off, dff + off + FF_BLOCK))
        acts.append((gate * val * _twice_sigmoid_of_twice(gate)).astype(BF16))
    act = jnp.concatenate(acts, axis=1)
    out_ref[...] = x1_scr[prv] + modp_ref[5:6, :] * _dot(act, wdn_ref[...])

    merged = (_twice_sigmoid_of_twice(ga_ref[...].astype(F32)) * ya
              + _twice_sigmoid_of_twice(gb_ref[...].astype(F32)) * yb)
    z1 = _dot(merged.astype(BF16), wo_ref[...])
    x1 = x_ref[...] + modn_ref[2:3, :] * z1
    x1_scr[nxt] = x1
    h2 = (x1 * _rms(x1, d)) * (g2_ref[...] * (1.0 + modn_ref[4:5, :])) + modn_ref[3:4, :]
    h2_scr[nxt] = h2.astype(BF16)


def _merge_ffn(attn, hgo, ga, gb, x2, mod, wa, wb, wo, g2, wup, cw, cb, wdn, *, seq, tm):
    t, d = x2.shape
    tps = seq // tm
    n = t // tm
    cur = lambda i: jnp.minimum(i, n - 1)
    prev = lambda i: jnp.maximum(i - 1, 0)
    row = lambda w: pl.BlockSpec((tm, w), lambda i: (cur(i), 0))
    consts = [wa, wb, wo, g2, wup, cw, cb, wdn]
    return pl.pallas_call(
        functools.partial(_merge_ffn_kernel, tps=tps),
        grid=(n + 1,),
        in_specs=[row(attn.shape[1]), row(hgo.shape[1]), row(d), row(d), row(d),
                  pl.BlockSpec((None, 6, d), lambda i: (cur(i) // tps, 0, 0)),
                  pl.BlockSpec((None, 6, d), lambda i: (prev(i) // tps, 0, 0))]
                 + [_const_spec(a.shape) for a in consts],
        out_specs=pl.BlockSpec((tm, d), lambda i: (prev(i), 0)),
        out_shape=jax.ShapeDtypeStruct((t, d), F32),
        scratch_shapes=[pltpu.VMEM((2, tm, d), F32), pltpu.VMEM((2, tm, d), BF16),
                        pltpu.VMEM((tm + SUBLANES, wup.shape[1]), F32)],
        compiler_params=_params(("arbitrary",)),
        name="merge_ffn",
    )(attn, hgo, ga, gb, x2, mod, mod, *consts)


def _pad_heads(w, used, n_heads):
    rows = w.shape[0]
    w = w.reshape(rows, n_heads, used)
    w = jnp.pad(w, ((0, 0), (0, 0), (0, HEAD_PAD - used)))
    return w.reshape(rows, n_heads * HEAD_PAD)


class _Tiles(NamedTuple):
    token: int
    q: int
    kv: int
    diag: int
    attn_heads: int
    chunk: int
    chunks_per_step: int


def _tiles(seq):
    tl = _Tiles(token=min(256, seq), q=min(1024, seq), kv=min(512, seq), diag=min(256, seq),
                attn_heads=4, chunk=min(128, seq), chunks_per_step=max(1, min(4, seq // 128)))
    assert seq % tl.token == 0 and seq % tl.q == 0 and tl.q % tl.kv == 0 and tl.kv % tl.diag == 0
    assert tl.kv % tl.token == 0 and seq % (tl.chunk * tl.chunks_per_step) == 0
    assert MLA_HEADS % tl.attn_heads == 0
    return tl


def _layer(x2, pos, invf, mod, p, *, bsz, seq, layer):
    t, d = x2.shape
    tl = _tiles(seq)
    w_in = p["w_in"]
    o_ckv = Q_LORA
    o_h = o_ckv + KV_LORA + ROPE_DIM
    o_g = o_h + 4 * HG_WIDTH
    wcq = w_in[:, :o_ckv].astype(BF16)
    wckv = jnp.concatenate([w_in[:, o_ckv:o_ckv + KV_LORA],
                            jnp.zeros((d, NOPE_DIM), F32),
                            w_in[:, o_ckv + KV_LORA:o_h],
                            jnp.zeros((d, HEAD_PAD - QK_DIM), F32)], axis=1).astype(BF16)
    wh = w_in[:, o_h:o_g].astype(BF16)
    wg = w_in[:, o_g:].astype(BF16)
    wuq = _pad_heads(p["w_uq"], QK_DIM, MLA_HEADS).astype(BF16)
    wukv = p["w_ukv"].reshape(KV_LORA, MLA_HEADS, NOPE_DIM + V_DIM)
    wk = _pad_heads(wukv[:, :, :NOPE_DIM].reshape(KV_LORA, -1), NOPE_DIM, MLA_HEADS).astype(BF16)
    wvt = _pad_heads(wukv[:, :, NOPE_DIM:].reshape(KV_LORA, -1), V_DIM, MLA_HEADS).T.astype(BF16)
    pad_g = lambda g: jnp.pad(g, (0, HEAD_PAD - QK_DIM)).reshape(1, HEAD_PAD)

    tm = tl.token
    q, k, vt, hq, hf, hi, hg, ga, gb = _mixer_in(
        x2, pos.reshape(t // tm, 1, tm), invf, mod, p["norm1_g"].reshape(1, d), wcq, wckv, wh,
        wg, p["q_a_norm_g"].reshape(1, -1), wuq, p["kv_a_norm_g"].reshape(1, -1), wk, wvt,
        pad_g(p["q_norm_g"]), pad_g(p["k_norm_g"]), seq=seq, tm=tm, tk=tl.kv)

    hw = MLA_HEADS * HEAD_PAD
    score_bound = (BF16_MARGIN * QK_DIM * Q_SCALE * jnp.max(jnp.abs(p["q_norm_g"]))
                   * jnp.max(jnp.abs(p["k_norm_g"]))).reshape(1).astype(F32)
    attn = _mla_attention(score_bound, q.reshape(bsz, seq, hw), k.reshape(bsz, seq, hw), vt,
                          tq=tl.q, tk=tl.kv, diag=tl.diag, heads=tl.attn_heads)
    r3 = lambda a: a.reshape(bsz, seq, HG_WIDTH)
    hgo = _hgrn2(r3(hq), r3(hf), r3(hi), r3(hg), p["hg_lower_bound"],
                 p["hg_out_norm_g"].reshape(1, HG_DIM), chunk=tl.chunk,
                 chunks_per_step=tl.chunks_per_step, layer=layer)

    dff = p["w_down"].shape[0]
    halve_gate = jnp.asarray(np.repeat(np.float32([0.5, 1.0]), dff))
    return _merge_ffn(attn.reshape(t, -1), hgo.reshape(t, -1), ga, gb, x2, mod,
                      p["w_branch_a"].astype(BF16), p["w_branch_b"].astype(BF16),
                      (p["w_out"] * 0.5).astype(BF16), p["norm2_g"].reshape(1, d),
                      p["w_up"].astype(BF16), p["conv_w"] * halve_gate,
                      (p["conv_b"] * halve_gate).reshape(1, -1),
                      p["w_down"].astype(BF16), seq=seq, tm=tm)


def kernel(x, c, positions, w_ada, b_ada, norm1_g, w_in, q_a_norm_g, w_uq, kv_a_norm_g, w_ukv,
           q_norm_g, k_norm_g, hg_lower_bound, hg_out_norm_g, w_branch_a, w_branch_b, w_out,
           norm2_g, w_up, conv_w, conv_b, w_down):
    bsz, seq, d = x.shape
    depth = w_ada.shape[0]
    invf = (ROPE_THETA ** (-jnp.arange(0, ROPE_DIM, 2, dtype=F32) / ROPE_DIM)).reshape(HALF_ROPE, 1)
    x2 = x.reshape(bsz * seq, d)
    pos = positions.reshape(bsz * seq)
    for l in range(depth):
        mod = _ada_mod(c, w_ada, b_ada, l).reshape(bsz, 6, d)
        p = dict(norm1_g=norm1_g[l], w_in=w_in[l], q_a_norm_g=q_a_norm_g[l], w_uq=w_uq[l],
                 kv_a_norm_g=kv_a_norm_g[l], w_ukv=w_ukv[l], q_norm_g=q_norm_g[l],
                 k_norm_g=k_norm_g[l], hg_lower_bound=hg_lower_bound, hg_out_norm_g=hg_out_norm_g[l],
                 w_branch_a=w_branch_a[l], w_branch_b=w_branch_b[l], w_out=w_out[l],
                 norm2_g=norm2_g[l], w_up=w_up[l], conv_w=conv_w[l], conv_b=conv_b[l],
                 w_down=w_down[l])
        x2 = _layer(x2, pos, invf, mod, p, bsz=bsz, seq=seq, layer=l)
    return x2.reshape(bsz, seq, d)
```

```python
import functools
from typing import NamedTuple

import numpy as np
import jax
import jax.numpy as jnp
from jax import lax
from jax.experimental import pallas as pl
from jax.experimental.pallas import tpu as pltpu

F32 = jnp.float32
BF16 = jnp.bfloat16

LANES = 128
SUBLANES = 8
VMEM_LIMIT = 56 * 1024 * 1024

MLA_HEADS = 8
NOPE_DIM = 64
ROPE_DIM = 32
HALF_ROPE = ROPE_DIM // 2
QK_DIM = NOPE_DIM + ROPE_DIM
V_DIM = 64
Q_LORA = 512
KV_LORA = 256
ROPE_THETA = 10000.0
HG_HEADS = 4
HG_DIM = 128
HG_WIDTH = HG_HEADS * HG_DIM
CONV_W = 3
EPS = 1e-6
HEAD_PAD = LANES
FF_BLOCK = 2 * LANES
LOG2E = float(np.log2(np.e))
NEG_BIG = -1e30
Q_SCALE = LOG2E / float(np.sqrt(QK_DIM))
BF16_MARGIN = (1.0 + 2.0 ** -8) ** 2
SAFE_EXP2_SPAN = 80.0


def _dot(a, b):
    return jnp.dot(a, b, preferred_element_type=F32)


def _dot_nt(a, b):
    return lax.dot_general(a, b, (((1,), (1,)), ((), ())), preferred_element_type=F32)


def _dot_tn(a, b):
    return lax.dot_general(a, b, (((0,), (0,)), ((), ())), preferred_element_type=F32)


def _twice_sigmoid_of_twice(h):
    return 1.0 + jnp.tanh(h)


def _rms(x, width):
    return lax.rsqrt(jnp.sum(x * x, axis=-1, keepdims=True) * (1.0 / width) + EPS)


def _const_spec(shape):
    nd = len(shape)
    return pl.BlockSpec(shape, lambda *_: (0,) * nd, pipeline_mode=pl.Buffered(1))


def _params(sem):
    return pltpu.CompilerParams(dimension_semantics=sem, vmem_limit_bytes=VMEM_LIMIT)


def _ada_kernel(c_ref, w_ref, b_ref, o_ref):
    c = c_ref[...]
    w = w_ref[...]
    c_hi = c.astype(BF16)
    c_lo = (c - c_hi.astype(F32)).astype(BF16)
    w_hi = w.astype(BF16)
    w_lo = (w - w_hi.astype(F32)).astype(BF16)
    o_ref[...] = _dot(c_hi, w_hi) + _dot(c_hi, w_lo) + _dot(c_lo, w_hi) + b_ref[...]


def _ada_mod(c, w_ada, b_ada, layer):
    bsz, d = c.shape
    depth, _, n = w_ada.shape
    bn = 1024
    return pl.pallas_call(
        _ada_kernel,
        grid=(n // bn,),
        in_specs=[pl.BlockSpec((bsz, d), lambda j: (0, 0)),
                  pl.BlockSpec((None, d, bn), lambda j: (layer, 0, j)),
                  pl.BlockSpec((None, 1, bn), lambda j: (layer, 0, j))],
        out_specs=pl.BlockSpec((bsz, bn), lambda j: (0, j)),
        out_shape=jax.ShapeDtypeStruct((bsz, n), F32),
        compiler_params=_params(("arbitrary",)),
        name="ada_mod",
    )(c, w_ada, b_ada.reshape(depth, 1, n))


def _rope(t, cos_t, sin_lo, sin_hi):
    return (t * cos_t + pltpu.roll(t, LANES - HALF_ROPE, 1) * sin_lo
            + pltpu.roll(t, HALF_ROPE, 1) * sin_hi)


def _mixer_in_kernel(x_ref, pos_ref, invf_ref, mod_ref, g1_ref, win_ref,
                     gqa_ref, wuq_ref, gkva_ref, wk_ref, wv_ref, gq_ref, gk_ref,
                     q_ref, k_ref, vt_ref, hq_ref, hf_ref, hi_ref, hg_ref, gab_ref):
    tm, d = x_ref.shape
    x = x_ref[...]
    h = (x * _rms(x, d)) * (g1_ref[...] * (1.0 + mod_ref[1:2, :])) + mod_ref[0:1, :]
    hb = h.astype(BF16)

    ang = invf_ref[...] * pos_ref[...].astype(F32)
    cos_h = jnp.cos(ang)
    sin_h = jnp.sin(ang)
    zeros = lambda r: jnp.zeros((r, tm), F32)
    cos_t = jnp.concatenate([jnp.ones((NOPE_DIM, tm), F32), cos_h, cos_h,
                             zeros(HEAD_PAD - QK_DIM)], axis=0).T
    sin_lo = jnp.concatenate([zeros(NOPE_DIM), -sin_h, zeros(HEAD_PAD - NOPE_DIM - HALF_ROPE)],
                             axis=0).T
    sin_hi = jnp.concatenate([zeros(NOPE_DIM + HALF_ROPE), sin_h, zeros(HEAD_PAD - QK_DIM)],
                             axis=0).T

    o_kv, o_h = Q_LORA, Q_LORA + KV_LORA + HEAD_PAD
    o_g = o_h + 4 * HG_WIDTH
    cq = _dot(hb, win_ref[:, :o_kv])
    ckv = _dot(hb, win_ref[:, o_kv:o_h])
    hh = _dot(hb, win_ref[:, o_h:o_h + 2 * HG_WIDTH])
    hq_ref[...] = (0.5 * hh[:, :HG_WIDTH]).astype(BF16)
    hf_ref[...] = 0.5 * hh[:, HG_WIDTH:]

    cqn = (cq * _rms(cq, Q_LORA) * gqa_ref[...]).astype(BF16)
    ckv_c = ckv[:, :KV_LORA]
    kr = ckv[:, KV_LORA:]
    ckvn = (ckv_c * _rms(ckv_c, KV_LORA) * gkva_ref[...]).astype(BF16)
    qf = _dot(cqn, wuq_ref[...])
    kf = _dot(ckvn, wk_ref[...])
    vt = _dot_nt(wv_ref[...], ckvn)
    hh = _dot(hb, win_ref[:, o_h + 2 * HG_WIDTH:o_g])
    hi_ref[...] = hh[:, :HG_WIDTH].astype(BF16)
    hg_ref[...] = (0.5 * hh[:, HG_WIDTH:]).astype(BF16)
    gg = _dot(hb, win_ref[:, o_g:])
    gab_ref[...] = (0.5 * gg).astype(BF16)

    gq = gq_ref[...]
    for hd in range(MLA_HEADS):
        sl = slice(hd * HEAD_PAD, (hd + 1) * HEAD_PAD)
        qh = qf[:, sl]
        qn = qh * _rms(qh, QK_DIM) * gq
        q_ref[:, sl] = (_rope(qn, cos_t, sin_lo, sin_hi) * Q_SCALE).astype(BF16)

    gk = gk_ref[...]
    ss_rope = jnp.sum(kr * kr, axis=-1, keepdims=True)
    kr_rot = _rope(kr * gk, cos_t, sin_lo, sin_hi)
    for hd in range(MLA_HEADS):
        sl = slice(hd * HEAD_PAD, (hd + 1) * HEAD_PAD)
        kn = kf[:, sl]
        ss = jnp.sum(kn * kn, axis=-1, keepdims=True) + ss_rope
        k_ref[:, sl] = ((kn * gk + kr_rot) * lax.rsqrt(ss * (1.0 / QK_DIM) + EPS)).astype(BF16)
    first_sub = lax.broadcasted_iota(jnp.int32, (SUBLANES, tm), 0) == 0
    pieces = []
    for hd in range(MLA_HEADS):
        r0 = hd * HEAD_PAD
        pieces += [vt[r0:r0 + V_DIM],
                   jnp.where(first_sub, 1.0, vt[r0 + V_DIM:r0 + V_DIM + SUBLANES]),
                   vt[r0 + V_DIM + SUBLANES:r0 + HEAD_PAD]]
    vt_ref[...] = jnp.concatenate(pieces, axis=0).astype(BF16)


def _mixer_in(x2, pos, invf, mod, g1, win, gqa, wuq, gkva, wk, wv, gq, gk, *,
              seq, tm, tk):
    t, d = x2.shape
    tps = seq // tm
    per_kv = tk // tm
    hw = MLA_HEADS * HEAD_PAD
    row = lambda w: pl.BlockSpec((tm, w), lambda i: (i, 0))
    vt_spec = pl.BlockSpec((None, None, hw, tm),
                           lambda i: (i // tps, (i % tps) // per_kv, 0, (i % tps) % per_kv))
    consts = [g1, win, gqa, wuq, gkva, wk, wv, gq, gk]
    out_shapes = [jax.ShapeDtypeStruct((t, hw), BF16)] * 2 + [
        jax.ShapeDtypeStruct((t // seq, seq // tk, hw, tk), BF16)] + [
        jax.ShapeDtypeStruct((t, HG_WIDTH), BF16), jax.ShapeDtypeStruct((t, HG_WIDTH), F32),
        jax.ShapeDtypeStruct((t, HG_WIDTH), BF16), jax.ShapeDtypeStruct((t, HG_WIDTH), BF16),
        jax.ShapeDtypeStruct((t, 2 * d), BF16)]
    return pl.pallas_call(
        _mixer_in_kernel,
        grid=(t // tm,),
        in_specs=[row(d),
                  pl.BlockSpec((None, 1, tm), lambda i: (i, 0, 0)),
                  _const_spec(invf.shape),
                  pl.BlockSpec((None, 6, d), lambda i: (i // tps, 0, 0))]
                 + [_const_spec(a.shape) for a in consts],
        out_specs=[row(hw)] * 2 + [vt_spec] + [row(HG_WIDTH)] * 4 + [row(2 * d)],
        out_shape=out_shapes,
        compiler_params=_params(("arbitrary",)),
        name="mixer_in",
    )(x2, pos, invf, mod, *consts)


def _attn_kernel(bound_ref, q_ref, k_ref, vt_ref, o_ref, m_scr, acc_scr, *, tq, tk, diag, heads):
    i = pl.program_id(2)
    nsub = tq // tk
    head_slices = [slice(hd * HEAD_PAD, (hd + 1) * HEAD_PAD) for hd in range(heads)]
    bound = bound_ref[0]

    def run(bounded):
        acc_scr[...] = jnp.zeros(acc_scr.shape, F32)
        if not bounded:
            m_scr[...] = jnp.full(m_scr.shape, NEG_BIG, F32)

        def block(j, off, nkv, row0, diagonal):
            start = pl.multiple_of(j * tk + off, nkv)
            stack = lambda a, n: jnp.concatenate([a] * (n // SUBLANES), axis=0)
            rows = slice(row0, tq)

            def scores(hd):
                sl = head_slices[hd]
                s = _dot_nt(k_ref[pl.ds(start, nkv), sl], q_ref[rows, sl])
                if diagonal:
                    kv_i = lax.broadcasted_iota(jnp.int32, s.shape, 0)
                    q_i = lax.broadcasted_iota(jnp.int32, s.shape, 1)
                    s = jnp.where(kv_i <= q_i, s, NEG_BIG)
                return s

            def update(hd, s):
                vt = vt_ref[j, head_slices[hd], off:off + nkv]
                if bounded:
                    acc_scr[hd, :, rows] += _dot(vt, jnp.exp2(s - bound).astype(BF16))
                    return
                m_prev = m_scr[hd, :, rows]
                m_new = jnp.maximum(m_prev, jnp.max(s, axis=0, keepdims=True))
                p = jnp.exp2(s - stack(m_new, nkv)).astype(BF16)
                acc_scr[hd, :, rows] = (acc_scr[hd, :, rows]
                                        * stack(jnp.exp2(m_prev - m_new), HEAD_PAD) + _dot(vt, p))
                m_scr[hd, :, rows] = m_new

            ahead = 2
            pending = [scores(hd) for hd in range(min(ahead, heads))]
            for hd in range(heads):
                if hd + ahead < heads:
                    pending.append(scores(hd + ahead))
                update(hd, pending.pop(0))

        def body(j, carry):
            for r in range(nsub):
                block(j * nsub + r, 0, tk, 0, False)
            return carry

        lax.fori_loop(0, i, body, 0)
        for pos in range(0, tq, diag):
            block(i * nsub + pos // tk, pos % tk, diag, pos, True)
        outs = []
        for hd in range(0, heads, 2):
            pair = [acc_scr[h, :V_DIM, :] / acc_scr[h, V_DIM:V_DIM + 1, :] for h in (hd, hd + 1)]
            outs.append(jnp.concatenate(pair, axis=0).T)
        o_ref[...] = jnp.concatenate(outs, axis=-1).astype(o_ref.dtype)

    no_underflow = 2.0 * bound <= SAFE_EXP2_SPAN
    pl.when(no_underflow)(lambda: run(True))
    pl.when(jnp.logical_not(no_underflow))(lambda: run(False))


def _mla_attention(score_bound, q, k, vt, *, tq, tk, diag, heads):
    bsz, seq, hw = q.shape
    groups = MLA_HEADS // heads
    pw = heads * HEAD_PAD
    return pl.pallas_call(
        functools.partial(_attn_kernel, tq=tq, tk=tk, diag=diag, heads=heads),
        grid=(bsz, groups, seq // tq),
        in_specs=[pl.BlockSpec(memory_space=pltpu.SMEM),
                  pl.BlockSpec((None, tq, pw), lambda b, h, i: (b, i, h)),
                  pl.BlockSpec((None, seq, pw), lambda b, h, i: (b, 0, h)),
                  pl.BlockSpec((None, seq // tk, pw, tk), lambda b, h, i: (b, 0, h, 0))],
        out_specs=pl.BlockSpec((None, tq, heads * V_DIM), lambda b, h, i: (b, i, h)),
        out_shape=jax.ShapeDtypeStruct((bsz, seq, MLA_HEADS * V_DIM), BF16),
        scratch_shapes=[pltpu.VMEM((heads, SUBLANES, tq), F32),
                        pltpu.VMEM((heads, HEAD_PAD, tq), F32)],
        compiler_params=_params(("arbitrary", "arbitrary", "arbitrary")),
        name="mla_attn",
    )(score_bound, q, k, vt)


def _hgrn_sum_masks(c):
    t = np.arange(c)[:, None]
    s = np.arange(c)[None, :]
    blocks = [s <= t]
    blk = c // 2
    while blk >= 2:
        bnd = (t & ~(2 * blk - 1)) + blk - 1
        blocks.append(((s > bnd) & (s <= t)) | ((s > t) & (s <= bnd)))
        blk //= 2
    em = np.concatenate(blocks, axis=0).astype(np.float32)
    return np.concatenate([em, em], axis=1)


def _hgrn_kernel(hq_ref, hf_ref, hi_ref, hg_ref, lbt_ref, gout_ref, em_ref, o_ref, st_ref, *,
                 layer, chunk):
    @pl.when(pl.program_id(1) == 0)
    def _():
        st_ref[...] = jnp.zeros_like(st_ref)

    for sub in range(hq_ref.shape[0] // chunk):
        rs = pl.ds(sub * chunk, chunk)
        _hgrn_chunk(hq_ref.at[rs], hf_ref.at[rs], hi_ref.at[rs], hg_ref.at[rs], lbt_ref, gout_ref,
                    em_ref, o_ref.at[rs], st_ref, layer=layer)


def _hgrn_chunk(hq_ref, hf_ref, hi_ref, hg_ref, lbt_ref, gout_ref, em_ref, o_ref, st_ref, *, layer):
    c, w = hq_ref.shape
    tab = lbt_ref[...]
    e = jnp.exp(tab - jnp.max(tab, axis=0, keepdims=True))
    lb = jnp.sum(e[1:layer + 2], axis=0, keepdims=True) / jnp.sum(e, axis=0, keepdims=True)

    hq = hq_ref[...].astype(F32)
    q = hq * _twice_sigmoid_of_twice(hq)
    f = lb + (0.5 - 0.5 * lb) * _twice_sigmoid_of_twice(hf_ref[...])
    lf = jnp.log2(f)
    kk = 1.0 - f
    v = hi_ref[...]

    lf_hi = lf.astype(BF16)
    lf_mid = (lf - lf_hi.astype(F32)).astype(BF16)
    sums = _dot(em_ref[...], jnp.concatenate([lf_hi, lf_mid], axis=0))
    b = sums[0:c]
    q_dec = (q * jnp.exp2(b)).astype(BF16)
    b_last = b[c - 1:c, :]
    k_last = (kk * jnp.exp2(b_last - b)).astype(BF16)

    n_levels = int(np.log2(c))
    sub = lax.broadcasted_iota(jnp.int32, (SUBLANES, w), 0)
    xs = []
    for lvl in range(n_levels):
        blk = 1 << lvl
        if blk >= SUBLANES:
            base = jnp.concatenate([(q if (j % 2) else kk)[j * blk:(j + 1) * blk]
                                    for j in range(c // blk)], axis=0)
        else:
            odd = jnp.concatenate([(sub & blk) != 0] * (c // SUBLANES), axis=0)
            base = jnp.where(odd, q * f, kk) if blk == 1 else jnp.where(odd, q, kk)
        if blk > 1:
            row0 = (n_levels - lvl) * c
            base = base * jnp.exp2(sums[row0:row0 + c])
        xs.append(base.astype(BF16))

    t_i = lax.broadcasted_iota(jnp.int32, (c, c), 0)
    s_i = lax.broadcasted_iota(jnp.int32, (c, c), 1)
    lower_xor = jnp.where(t_i > s_i, t_i ^ s_i, 0)
    level_of = [(lower_xor >> lvl) == 1 for lvl in range(n_levels)]
    on_diag = t_i == s_i
    qk = q * kk
    gout = gout_ref[...]
    head_slices = [slice(hd * HG_DIM, (hd + 1) * HG_DIM) for hd in range(HG_HEADS)]
    attns = [jnp.where(on_diag, jnp.sum(qk[:, hs], axis=-1, keepdims=True), 0.0)
             for hs in head_slices]
    for x, keep in zip(xs, level_of):
        attns = [jnp.where(keep, _dot_nt(x[:, hs], x[:, hs]), a) for hs, a in zip(head_slices, attns)]
    for hd, hs in enumerate(head_slices):
        attn = attns[hd]
        st = st_ref[hd]
        o = _dot_nt(q_dec[:, hs], st.astype(BF16)) + _dot(attn.astype(BF16), v[:, hs])
        hg = hg_ref[:, hs].astype(F32)
        o = o * _rms(o, HG_DIM) * gout * (hg * _twice_sigmoid_of_twice(hg))
        o_ref[:, hs] = o.astype(o_ref.dtype)
        st_ref[hd] = st * jnp.exp2(b_last[:, hs]) + _dot_tn(v[:, hs], k_last[:, hs])


def _hgrn2(hq, hf, hi, hg, lb_table, gout, *, chunk, chunks_per_step, layer):
    bsz, seq, w = hq.shape
    rows = chunk * chunks_per_step
    blk = pl.BlockSpec((None, rows, w), lambda b, c: (b, c, 0))
    em = jnp.asarray(_hgrn_sum_masks(chunk), dtype=BF16)
    return pl.pallas_call(
        functools.partial(_hgrn_kernel, layer=layer, chunk=chunk),
        grid=(bsz, seq // rows),
        in_specs=[blk, blk, blk, blk, _const_spec(lb_table.shape), _const_spec(gout.shape),
                  _const_spec(em.shape)],
        out_specs=blk,
        out_shape=jax.ShapeDtypeStruct((bsz, seq, w), BF16),
        scratch_shapes=[pltpu.VMEM((HG_HEADS, HG_DIM, HG_DIM), F32)],
        compiler_params=_params(("arbitrary", "arbitrary")),
        name="hgrn2",
    )(hq, hf, hi, hg, lb_table, gout, em)


def _merge_ffn_kernel(a_ref, o_ref, gab_ref, x_ref, modn_ref, modp_ref, wa_ref, wb_ref, wo_ref,
                      g2_ref, wup_ref, cw_ref, cb_ref, wdn_ref, out_ref, x1_scr, h2_scr, up_scr, *, tps):
    tm, d = x_ref.shape
    dff = wdn_ref.shape[0]
    halo = SUBLANES
    i = pl.program_id(0)
    nxt = i % 2
    prv = 1 - nxt

    @pl.when(i == 0)
    def _():
        rows = 2 * SUBLANES

        def clear(r, carry):
            start = pl.multiple_of(r * rows, rows)
            x1_scr[prv, pl.ds(start, rows), :] = jnp.zeros((rows, d), F32)
            h2_scr[prv, pl.ds(start, rows), :] = jnp.zeros((rows, d), BF16)
            return carry

        lax.fori_loop(0, tm // rows, clear, 0)

    @pl.when(jnp.logical_or(i == 0, (i + tps - 1) % tps == 0))
    def _():
        up_scr[0:halo, :] = jnp.zeros((halo, up_scr.shape[1]), F32)

    ya = _dot(a_ref[...], wa_ref[...])
    yb = _dot(o_ref[...], wb_ref[...])
    h2p = h2_scr[prv]

    def conv_block(cols):
        up = _dot(h2p, wup_ref[:, cols])
        up_scr[halo:halo + tm, cols] = up
        y = cb_ref[:, cols] + cw_ref[CONV_W - 1:CONV_W, cols] * up
        for j in range(CONV_W - 1):
            back = CONV_W - 1 - j
            y = y + cw_ref[j:j + 1, cols] * up_scr[halo - back:halo - back + tm, cols]
        up_scr[0:halo, cols] = up[tm - halo:, :]
        return y

    acts = []
    for off in range(0, dff, FF_BLOCK):
        gate = conv_block(slice(off, off + FF_BLOCK))
        val = conv_block(slice(dff + off, dff + off + FF_BLOCK))
        acts.append((gate * val * _twice_sigmoid_of_twice(gate)).astype(BF16))
    act = jnp.concatenate(acts, axis=1)
    out_ref[...] = x1_scr[prv] + modp_ref[5:6, :] * _dot(act, wdn_ref[...])

    merged = (_twice_sigmoid_of_twice(gab_ref[:, :d].astype(F32)) * ya
              + _twice_sigmoid_of_twice(gab_ref[:, d:].astype(F32)) * yb)
    z1 = _dot(merged.astype(BF16), wo_ref[...])
    x1 = x_ref[...] + modn_ref[2:3, :] * z1
    x1_scr[nxt] = x1
    h2 = (x1 * _rms(x1, d)) * (g2_ref[...] * (1.0 + modn_ref[4:5, :])) + modn_ref[3:4, :]
    h2_scr[nxt] = h2.astype(BF16)


def _merge_ffn(attn, hgo, gab, x2, mod, wa, wb, wo, g2, wup, cw, cb, wdn, *, seq, tm):
    t, d = x2.shape
    tps = seq // tm
    n = t // tm
    cur = lambda i: jnp.minimum(i, n - 1)
    prev = lambda i: jnp.maximum(i - 1, 0)
    row = lambda w: pl.BlockSpec((tm, w), lambda i: (cur(i), 0))
    consts = [wa, wb, wo, g2, wup, cw, cb, wdn]
    return pl.pallas_call(
        functools.partial(_merge_ffn_kernel, tps=tps),
        grid=(n + 1,),
        in_specs=[row(attn.shape[1]), row(hgo.shape[1]), row(2 * d), row(d),
                  pl.BlockSpec((None, 6, d), lambda i: (cur(i) // tps, 0, 0)),
                  pl.BlockSpec((None, 6, d), lambda i: (prev(i) // tps, 0, 0))]
                 + [_const_spec(a.shape) for a in consts],
        out_specs=pl.BlockSpec((tm, d), lambda i: (prev(i), 0)),
        out_shape=jax.ShapeDtypeStruct((t, d), F32),
        scratch_shapes=[pltpu.VMEM((2, tm, d), F32), pltpu.VMEM((2, tm, d), BF16),
                        pltpu.VMEM((tm + SUBLANES, wup.shape[1]), F32)],
        compiler_params=_params(("arbitrary",)),
        name="merge_ffn",
    )(attn, hgo, gab, x2, mod, mod, *consts)


def _pad_heads(w, used, n_heads):
    rows = w.shape[0]
    w = w.reshape(rows, n_heads, used)
    w = jnp.pad(w, ((0, 0), (0, 0), (0, HEAD_PAD - used)))
    return w.reshape(rows, n_heads * HEAD_PAD)


class _Tiles(NamedTuple):
    token: int
    q: int
    kv: int
    diag: int
    attn_heads: int
    chunk: int
    chunks_per_step: int


def _tiles(seq):
    tl = _Tiles(token=min(256, seq), q=min(1024, seq), kv=min(512, seq), diag=min(256, seq),
                attn_heads=4, chunk=min(128, seq), chunks_per_step=max(1, min(4, seq // 128)))
    assert seq % tl.token == 0 and seq % tl.q == 0 and tl.q % tl.kv == 0 and tl.kv % tl.diag == 0
    assert tl.kv % tl.token == 0 and seq % (tl.chunk * tl.chunks_per_step) == 0
    assert MLA_HEADS % tl.attn_heads == 0
    return tl


def _layer(x2, pos, invf, mod, p, *, bsz, seq, layer):
    t, d = x2.shape
    tl = _tiles(seq)
    w_in = p["w_in"]
    o_rope = Q_LORA + KV_LORA
    win = jnp.concatenate([w_in[:, :o_rope],
                           jnp.zeros((d, NOPE_DIM), F32),
                           w_in[:, o_rope:o_rope + ROPE_DIM],
                           jnp.zeros((d, HEAD_PAD - QK_DIM), F32),
                           w_in[:, o_rope + ROPE_DIM:]], axis=1).astype(BF16)
    wuq = _pad_heads(p["w_uq"], QK_DIM, MLA_HEADS).astype(BF16)
    wukv = p["w_ukv"].reshape(KV_LORA, MLA_HEADS, NOPE_DIM + V_DIM)
    wk = _pad_heads(wukv[:, :, :NOPE_DIM].reshape(KV_LORA, -1), NOPE_DIM, MLA_HEADS).astype(BF16)
    wvt = _pad_heads(wukv[:, :, NOPE_DIM:].reshape(KV_LORA, -1), V_DIM, MLA_HEADS).T.astype(BF16)
    pad_g = lambda g: jnp.pad(g, (0, HEAD_PAD - QK_DIM)).reshape(1, HEAD_PAD)

    tm = tl.token
    q, k, vt, hq, hf, hi, hg, gab = _mixer_in(
        x2, pos.reshape(t // tm, 1, tm), invf, mod, p["norm1_g"].reshape(1, d), win,
        p["q_a_norm_g"].reshape(1, -1), wuq, p["kv_a_norm_g"].reshape(1, -1), wk, wvt,
        pad_g(p["q_norm_g"]), pad_g(p["k_norm_g"]), seq=seq, tm=tm, tk=tl.kv)

    hw = MLA_HEADS * HEAD_PAD
    score_bound = (BF16_MARGIN * QK_DIM * Q_SCALE * jnp.max(jnp.abs(p["q_norm_g"]))
                   * jnp.max(jnp.abs(p["k_norm_g"]))).reshape(1).astype(F32)
    attn = _mla_attention(score_bound, q.reshape(bsz, seq, hw), k.reshape(bsz, seq, hw), vt,
                          tq=tl.q, tk=tl.kv, diag=tl.diag, heads=tl.attn_heads)
    r3 = lambda a: a.reshape(bsz, seq, HG_WIDTH)
    hgo = _hgrn2(r3(hq), r3(hf), r3(hi), r3(hg), p["hg_lower_bound"],
                 p["hg_out_norm_g"].reshape(1, HG_DIM), chunk=tl.chunk,
                 chunks_per_step=tl.chunks_per_step, layer=layer)

    dff = p["w_down"].shape[0]
    halve_gate = jnp.asarray(np.repeat(np.float32([0.5, 1.0]), dff))
    return _merge_ffn(attn.reshape(t, -1), hgo.reshape(t, -1), gab, x2, mod,
                      p["w_branch_a"].astype(BF16), p["w_branch_b"].astype(BF16),
                      (p["w_out"] * 0.5).astype(BF16), p["norm2_g"].reshape(1, d),
                      p["w_up"].astype(BF16), p["conv_w"] * halve_gate,
                      (p["conv_b"] * halve_gate).reshape(1, -1),
                      p["w_down"].astype(BF16), seq=seq, tm=tm)


def kernel(x, c, positions, w_ada, b_ada, norm1_g, w_in, q_a_norm_g, w_uq, kv_a_norm_g, w_ukv,
           q_norm_g, k_norm_g, hg_lower_bound, hg_out_norm_g, w_branch_a, w_branch_b, w_out,
           norm2_g, w_up, conv_w, conv_b, w_down):
    bsz, seq, d = x.shape
    depth = w_ada.shape[0]
    invf = (ROPE_THETA ** (-jnp.arange(0, ROPE_DIM, 2, dtype=F32) / ROPE_DIM)).reshape(HALF_ROPE, 1)
    x2 = x.reshape(bsz * seq, d)
    pos = positions.reshape(bsz * seq)
    for l in range(depth):
        mod = _ada_mod(c, w_ada, b_ada, l).reshape(bsz, 6, d)
        p = dict(norm1_g=norm1_g[l], w_in=w_in[l], q_a_norm_g=q_a_norm_g[l], w_uq=w_uq[l],
                 kv_a_norm_g=kv_a_norm_g[l], w_ukv=w_ukv[l], q_norm_g=q_norm_g[l],
                 k_norm_g=k_norm_g[l], hg_lower_bound=hg_lower_bound, hg_out_norm_g=hg_out_norm_g[l],
                 w_branch_a=w_branch_a[l], w_branch_b=w_branch_b[l], w_out=w_out[l],
                 norm2_g=norm2_g[l], w_up=w_up[l], conv_w=conv_w[l], conv_b=conv_b[l],
                 w_down=w_down[l])
        x2 = _layer(x2, pos, invf, mod, p, bsz=bsz, seq=seq, layer=l)
    return x2.reshape(bsz, seq, d)
```

```python
import functools
from typing import NamedTuple

import numpy as np
import jax
import jax.numpy as jnp
from jax import lax
from jax.experimental import pallas as pl
from jax.experimental.pallas import tpu as pltpu

F32 = jnp.float32
BF16 = jnp.bfloat16

LANES = 128
SUBLANES = 8
VMEM_LIMIT = 56 * 1024 * 1024

MLA_HEADS = 8
NOPE_DIM = 64
ROPE_DIM = 32
HALF_ROPE = ROPE_DIM // 2
QK_DIM = NOPE_DIM + ROPE_DIM
V_DIM = 64
Q_LORA = 512
KV_LORA = 256
ROPE_THETA = 10000.0
HG_HEADS = 4
HG_DIM = 128
HG_WIDTH = HG_HEADS * HG_DIM
CONV_W = 3
EPS = 1e-6
HEAD_PAD = LANES
FF_BLOCK = 2 * LANES
LOG2E = float(np.log2(np.e))
NEG_BIG = -1e30
Q_SCALE = LOG2E / float(np.sqrt(QK_DIM))
BF16_MARGIN = (1.0 + 2.0 ** -8) ** 2
SAFE_EXP2_SPAN = 80.0


def _dot(a, b):
    return jnp.dot(a, b, preferred_element_type=F32)


def _dot_nt(a, b):
    return lax.dot_general(a, b, (((1,), (1,)), ((), ())), preferred_element_type=F32)


def _dot_tn(a, b):
    return lax.dot_general(a, b, (((0,), (0,)), ((), ())), preferred_element_type=F32)


def _twice_sigmoid_of_twice(h):
    return 1.0 + jnp.tanh(h)


def _rms(x, width):
    return lax.rsqrt(jnp.sum(x * x, axis=-1, keepdims=True) * (1.0 / width) + EPS)


def _const_spec(shape):
    nd = len(shape)
    return pl.BlockSpec(shape, lambda *_: (0,) * nd, pipeline_mode=pl.Buffered(1))


def _params(sem):
    return pltpu.CompilerParams(dimension_semantics=sem, vmem_limit_bytes=VMEM_LIMIT)


def _ada_kernel(c_ref, w_ref, b_ref, o_ref):
    c = c_ref[...]
    w = w_ref[...]
    c_hi = c.astype(BF16)
    c_lo = (c - c_hi.astype(F32)).astype(BF16)
    w_hi = w.astype(BF16)
    w_lo = (w - w_hi.astype(F32)).astype(BF16)
    o_ref[...] = _dot(c_hi, w_hi) + _dot(c_hi, w_lo) + _dot(c_lo, w_hi) + b_ref[...]


def _ada_mod(c, w_ada, b_ada, layer):
    bsz, d = c.shape
    depth, _, n = w_ada.shape
    bn = 1024
    return pl.pallas_call(
        _ada_kernel,
        grid=(n // bn,),
        in_specs=[pl.BlockSpec((bsz, d), lambda j: (0, 0)),
                  pl.BlockSpec((None, d, bn), lambda j: (layer, 0, j)),
                  pl.BlockSpec((None, 1, bn), lambda j: (layer, 0, j))],
        out_specs=pl.BlockSpec((bsz, bn), lambda j: (0, j)),
        out_shape=jax.ShapeDtypeStruct((bsz, n), F32),
        compiler_params=_params(("arbitrary",)),
        name="ada_mod",
    )(c, w_ada, b_ada.reshape(depth, 1, n))


def _rope(t, cos_t, sin_lo, sin_hi):
    return (t * cos_t + pltpu.roll(t, LANES - HALF_ROPE, 1) * sin_lo
            + pltpu.roll(t, HALF_ROPE, 1) * sin_hi)


def _mixer_in_kernel(x_ref, pos_ref, invf_ref, mod_ref, g1_ref, wcq_ref, wckv_ref, wh_ref, wg_ref,
                     gqa_ref, wuq_ref, gkva_ref, wk_ref, wv_ref, gq_ref, gk_ref,
                     q_ref, k_ref, vt_ref, hq_ref, hf_ref, hi_ref, hg_ref, ga_ref, gb_ref):
    tm, d = x_ref.shape
    x = x_ref[...]
    h = (x * _rms(x, d)) * (g1_ref[...] * (1.0 + mod_ref[1:2, :])) + mod_ref[0:1, :]
    hb = h.astype(BF16)

    ang = invf_ref[...] * pos_ref[...].astype(F32)
    cos_h = jnp.cos(ang)
    sin_h = jnp.sin(ang)
    zeros = lambda r: jnp.zeros((r, tm), F32)
    cos_t = jnp.concatenate([jnp.ones((NOPE_DIM, tm), F32), cos_h, cos_h,
                             zeros(HEAD_PAD - QK_DIM)], axis=0).T
    sin_lo = jnp.concatenate([zeros(NOPE_DIM), -sin_h, zeros(HEAD_PAD - NOPE_DIM - HALF_ROPE)],
                             axis=0).T
    sin_hi = jnp.concatenate([zeros(NOPE_DIM + HALF_ROPE), sin_h, zeros(HEAD_PAD - QK_DIM)],
                             axis=0).T

    cq = _dot(hb, wcq_ref[...])
    ckv = _dot(hb, wckv_ref[...])
    hh = _dot(hb, wh_ref[:, :2 * HG_WIDTH])
    hq_ref[...] = (0.5 * hh[:, :HG_WIDTH]).astype(BF16)
    hf_ref[...] = 0.5 * hh[:, HG_WIDTH:]

    cqn = (cq * _rms(cq, Q_LORA) * gqa_ref[...]).astype(BF16)
    ckv_c = ckv[:, :KV_LORA]
    kr = ckv[:, KV_LORA:]
    ckvn = (ckv_c * _rms(ckv_c, KV_LORA) * gkva_ref[...]).astype(BF16)
    qf = _dot(cqn, wuq_ref[...])
    kf = _dot(ckvn, wk_ref[...])
    vt = _dot_nt(wv_ref[...], ckvn)
    hh = _dot(hb, wh_ref[:, 2 * HG_WIDTH:])
    hi_ref[...] = hh[:, :HG_WIDTH].astype(BF16)
    hg_ref[...] = (0.5 * hh[:, HG_WIDTH:]).astype(BF16)
    gg = _dot(hb, wg_ref[...])
    ga_ref[...] = (0.5 * gg[:, :d]).astype(BF16)
    gb_ref[...] = (0.5 * gg[:, d:]).astype(BF16)

    gq = gq_ref[...]
    for hd in range(MLA_HEADS):
        sl = slice(hd * HEAD_PAD, (hd + 1) * HEAD_PAD)
        qh = qf[:, sl]
        qn = qh * _rms(qh, QK_DIM) * gq
        q_ref[:, sl] = (_rope(qn, cos_t, sin_lo, sin_hi) * Q_SCALE).astype(BF16)

    gk = gk_ref[...]
    ss_rope = jnp.sum(kr * kr, axis=-1, keepdims=True)
    kr_rot = _rope(kr * gk, cos_t, sin_lo, sin_hi)
    for hd in range(MLA_HEADS):
        sl = slice(hd * HEAD_PAD, (hd + 1) * HEAD_PAD)
        kn = kf[:, sl]
        ss = jnp.sum(kn * kn, axis=-1, keepdims=True) + ss_rope
        k_ref[:, sl] = ((kn * gk + kr_rot) * lax.rsqrt(ss * (1.0 / QK_DIM) + EPS)).astype(BF16)
    first_sub = lax.broadcasted_iota(jnp.int32, (SUBLANES, tm), 0) == 0
    pieces = []
    for hd in range(MLA_HEADS):
        r0 = hd * HEAD_PAD
        pieces += [vt[r0:r0 + V_DIM],
                   jnp.where(first_sub, 1.0, vt[r0 + V_DIM:r0 + V_DIM + SUBLANES]),
                   vt[r0 + V_DIM + SUBLANES:r0 + HEAD_PAD]]
    vt_ref[...] = jnp.concatenate(pieces, axis=0).astype(BF16)


def _mixer_in(x2, pos, invf, mod, g1, wcq, wckv, wh, wg, gqa, wuq, gkva, wk, wv, gq, gk, *,
              seq, tm, tk):
    t, d = x2.shape
    tps = seq // tm
    per_kv = tk // tm
    hw = MLA_HEADS * HEAD_PAD
    row = lambda w: pl.BlockSpec((tm, w), lambda i: (i, 0))
    vt_spec = pl.BlockSpec((None, None, hw, tm),
                           lambda i: (i // tps, (i % tps) // per_kv, 0, (i % tps) % per_kv))
    consts = [g1, wcq, wckv, wh, wg, gqa, wuq, gkva, wk, wv, gq, gk]
    out_shapes = [jax.ShapeDtypeStruct((t, hw), BF16)] * 2 + [
        jax.ShapeDtypeStruct((t // seq, seq // tk, hw, tk), BF16)] + [
        jax.ShapeDtypeStruct((t, HG_WIDTH), BF16), jax.ShapeDtypeStruct((t, HG_WIDTH), F32),
        jax.ShapeDtypeStruct((t, HG_WIDTH), BF16), jax.ShapeDtypeStruct((t, HG_WIDTH), BF16),
        jax.ShapeDtypeStruct((t, d), BF16), jax.ShapeDtypeStruct((t, d), BF16)]
    return pl.pallas_call(
        _mixer_in_kernel,
        grid=(t // tm,),
        in_specs=[row(d),
                  pl.BlockSpec((None, 1, tm), lambda i: (i, 0, 0)),
                  _const_spec(invf.shape),
                  pl.BlockSpec((None, 6, d), lambda i: (i // tps, 0, 0))]
                 + [_const_spec(a.shape) for a in consts],
        out_specs=[row(hw)] * 2 + [vt_spec] + [row(HG_WIDTH)] * 4 + [row(d)] * 2,
        out_shape=out_shapes,
        compiler_params=_params(("arbitrary",)),
        name="mixer_in",
    )(x2, pos, invf, mod, *consts)


def _attn_kernel(bound_ref, q_ref, k_ref, vt_ref, o_ref, m_scr, acc_scr, *, tq, tk, diag, heads):
    i = pl.program_id(2)
    nsub = tq // tk
    head_slices = [slice(hd * HEAD_PAD, (hd + 1) * HEAD_PAD) for hd in range(heads)]
    bound = bound_ref[0]

    def run(bounded):
        acc_scr[...] = jnp.zeros(acc_scr.shape, F32)
        if not bounded:
            m_scr[...] = jnp.full(m_scr.shape, NEG_BIG, F32)

        def block(j, off, nkv, row0, diagonal):
            start = pl.multiple_of(j * tk + off, nkv)
            stack = lambda a, n: jnp.concatenate([a] * (n // SUBLANES), axis=0)
            rows = slice(row0, tq)

            def scores(hd):
                sl = head_slices[hd]
                s = _dot_nt(k_ref[pl.ds(start, nkv), sl], q_ref[rows, sl])
                if diagonal:
                    kv_i = lax.broadcasted_iota(jnp.int32, s.shape, 0)
                    q_i = lax.broadcasted_iota(jnp.int32, s.shape, 1)
                    s = jnp.where(kv_i <= q_i, s, NEG_BIG)
                return s

            def update(hd, s):
                vt = vt_ref[j, head_slices[hd], off:off + nkv]
                if bounded:
                    acc_scr[hd, :, rows] += _dot(vt, jnp.exp2(s - bound).astype(BF16))
                    return
                m_prev = m_scr[hd, :, rows]
                m_new = jnp.maximum(m_prev, jnp.max(s, axis=0, keepdims=True))
                p = jnp.exp2(s - stack(m_new, nkv)).astype(BF16)
                acc_scr[hd, :, rows] = (acc_scr[hd, :, rows]
                                        * stack(jnp.exp2(m_prev - m_new), HEAD_PAD) + _dot(vt, p))
                m_scr[hd, :, rows] = m_new

            ahead = 2
            pending = [scores(hd) for hd in range(min(ahead, heads))]
            for hd in range(heads):
                if hd + ahead < heads:
                    pending.append(scores(hd + ahead))
                update(hd, pending.pop(0))

        def body(j, carry):
            for r in range(nsub):
                block(j * nsub + r, 0, tk, 0, False)
            return carry

        lax.fori_loop(0, i, body, 0)
        for pos in range(0, tq, diag):
            block(i * nsub + pos // tk, pos % tk, diag, pos, True)
        outs = []
        for hd in range(0, heads, 2):
            pair = [acc_scr[h, :V_DIM, :] / acc_scr[h, V_DIM:V_DIM + 1, :] for h in (hd, hd + 1)]
            outs.append(jnp.concatenate(pair, axis=0).T)
        o_ref[...] = jnp.concatenate(outs, axis=-1).astype(o_ref.dtype)

    no_underflow = 2.0 * bound <= SAFE_EXP2_SPAN
    pl.when(no_underflow)(lambda: run(True))
    pl.when(jnp.logical_not(no_underflow))(lambda: run(False))


def _mla_attention(score_bound, q, k, vt, *, tq, tk, diag, heads):
    bsz, seq, hw = q.shape
    groups = MLA_HEADS // heads
    pw = heads * HEAD_PAD
    return pl.pallas_call(
        functools.partial(_attn_kernel, tq=tq, tk=tk, diag=diag, heads=heads),
        grid=(bsz, groups, seq // tq),
        in_specs=[pl.BlockSpec(memory_space=pltpu.SMEM),
                  pl.BlockSpec((None, tq, pw), lambda b, h, i: (b, i, h)),
                  pl.BlockSpec((None, seq, pw), lambda b, h, i: (b, 0, h)),
                  pl.BlockSpec((None, seq // tk, pw, tk), lambda b, h, i: (b, 0, h, 0))],
        out_specs=pl.BlockSpec((None, tq, heads * V_DIM), lambda b, h, i: (b, i, h)),
        out_shape=jax.ShapeDtypeStruct((bsz, seq, MLA_HEADS * V_DIM), BF16),
        scratch_shapes=[pltpu.VMEM((heads, SUBLANES, tq), F32),
                        pltpu.VMEM((heads, HEAD_PAD, tq), F32)],
        compiler_params=_params(("arbitrary", "arbitrary", "arbitrary")),
        name="mla_attn",
    )(score_bound, q, k, vt)


def _hgrn_sum_masks(c):
    t = np.arange(c)[:, None]
    s = np.arange(c)[None, :]
    blocks = [s <= t, s > t]
    blk = c // 2
    while blk >= 2:
        bnd = (t & ~(2 * blk - 1)) + blk - 1
        blocks.append(((s > bnd) & (s <= t)) | ((s > t) & (s <= bnd)))
        blk //= 2
    em = np.concatenate(blocks, axis=0).astype(np.float32)
    return np.concatenate([em, em], axis=1)


def _hgrn_kernel(hq_ref, hf_ref, hi_ref, hg_ref, lbt_ref, gout_ref, em_ref, o_ref, st_ref, xt_scr, *,
                 layer, chunk):
    @pl.when(pl.program_id(1) == 0)
    def _():
        st_ref[...] = jnp.zeros_like(st_ref)

    for sub in range(hq_ref.shape[0] // chunk):
        rs = pl.ds(sub * chunk, chunk)
        _hgrn_chunk(hq_ref.at[rs], hf_ref.at[rs], hi_ref.at[rs], hg_ref.at[rs], lbt_ref, gout_ref,
                    em_ref, o_ref.at[rs], st_ref, xt_scr, layer=layer)


def _hgrn_chunk(hq_ref, hf_ref, hi_ref, hg_ref, lbt_ref, gout_ref, em_ref, o_ref, st_ref, xt_scr, *, layer):
    c, w = hq_ref.shape
    tab = lbt_ref[...]
    e = jnp.exp(tab - jnp.max(tab, axis=0, keepdims=True))
    lb = jnp.sum(e[1:layer + 2], axis=0, keepdims=True) / jnp.sum(e, axis=0, keepdims=True)

    hq = hq_ref[...].astype(F32)
    q = hq * _twice_sigmoid_of_twice(hq)
    f = lb + (0.5 - 0.5 * lb) * _twice_sigmoid_of_twice(hf_ref[...])
    lf = jnp.log2(f)
    kk = 1.0 - f
    v = hi_ref[...]

    lf_hi = lf.astype(BF16)
    lf_mid = (lf - lf_hi.astype(F32)).astype(BF16)
    sums = _dot(em_ref[...], jnp.concatenate([lf_hi, lf_mid], axis=0))
    b = sums[0:c]
    q_dec = (q * jnp.exp2(b)).astype(BF16)
    k_last = (kk * jnp.exp2(sums[c:2 * c])).astype(BF16)
    b_last = b[c - 1:c, :]

    n_levels = int(np.log2(c))
    sub = lax.broadcasted_iota(jnp.int32, (SUBLANES, w), 0)
    xs = []
    for lvl in range(n_levels):
        blk = 1 << lvl
        if blk >= SUBLANES:
            base = jnp.concatenate([(q if (j % 2) else kk)[j * blk:(j + 1) * blk]
                                    for j in range(c // blk)], axis=0)
        else:
            odd = jnp.concatenate([(sub & blk) != 0] * (c // SUBLANES), axis=0)
            base = jnp.where(odd, q * f, kk) if blk == 1 else jnp.where(odd, q, kk)
        if blk > 1:
            row0 = (2 + n_levels - 1 - lvl) * c
            base = base * jnp.exp2(sums[row0:row0 + c])
        xs.append(base.astype(BF16))

    t_i = lax.broadcasted_iota(jnp.int32, (c, c), 0)
    s_i = lax.broadcasted_iota(jnp.int32, (c, c), 1)
    lower_xor = jnp.where(t_i > s_i, t_i ^ s_i, 0)
    level_of = [(lower_xor >> lvl) == 1 for lvl in range(n_levels)]
    on_diag = t_i == s_i
    qk = q * kk
    gout = gout_ref[...]
    head_slices = [slice(hd * HG_DIM, (hd + 1) * HG_DIM) for hd in range(HG_HEADS)]
    attns = [jnp.where(on_diag, jnp.sum(qk[:, hs], axis=-1, keepdims=True), 0.0)
             for hs in head_slices]
    for lvl, (x, keep) in enumerate(zip(xs, level_of)):
        for hs in head_slices:
            xt_scr[lvl, hs, :] = x[:, hs].T
        attns = [jnp.where(keep, _dot(x[:, hs], xt_scr[lvl, hs, :]), a)
                 for hs, a in zip(head_slices, attns)]
    for hd, hs in enumerate(head_slices):
        attn = attns[hd]
        st = st_ref[hd]
        o = _dot_nt(q_dec[:, hs], st.astype(BF16)) + _dot(attn.astype(BF16), v[:, hs])
        hg = hg_ref[:, hs].astype(F32)
        o = o * _rms(o, HG_DIM) * gout * (hg * _twice_sigmoid_of_twice(hg))
        o_ref[:, hs] = o.astype(o_ref.dtype)
        st_ref[hd] = st * jnp.exp2(b_last[:, hs]) + _dot_tn(v[:, hs], k_last[:, hs])


def _hgrn2(hq, hf, hi, hg, lb_table, gout, *, chunk, chunks_per_step, layer):
    bsz, seq, w = hq.shape
    rows = chunk * chunks_per_step
    blk = pl.BlockSpec((None, rows, w), lambda b, c: (b, c, 0))
    em = jnp.asarray(_hgrn_sum_masks(chunk), dtype=BF16)
    return pl.pallas_call(
        functools.partial(_hgrn_kernel, layer=layer, chunk=chunk),
        grid=(bsz, seq // rows),
        in_specs=[blk, blk, blk, blk, _const_spec(lb_table.shape), _const_spec(gout.shape),
                  _const_spec(em.shape)],
        out_specs=blk,
        out_shape=jax.ShapeDtypeStruct((bsz, seq, w), BF16),
        scratch_shapes=[pltpu.VMEM((HG_HEADS, HG_DIM, HG_DIM), F32),
                        pltpu.VMEM((int(np.log2(chunk)), w, chunk), BF16)],
        compiler_params=_params(("arbitrary", "arbitrary")),
        name="hgrn2",
    )(hq, hf, hi, hg, lb_table, gout, em)


def _merge_ffn_kernel(a_ref, o_ref, ga_ref, gb_ref, x_ref, modn_ref, modp_ref, wa_ref, wb_ref, wo_ref,
                      g2_ref, wup_ref, cw_ref, cb_ref, wdn_ref, out_ref, x1_scr, h2_scr, up_scr, *, tps):
    tm, d = x_ref.shape
    dff = wdn_ref.shape[0]
    halo = SUBLANES
    i = pl.program_id(0)
    nxt = i % 2
    prv = 1 - nxt

    @pl.when(i == 0)
    def _():
        rows = 2 * SUBLANES

        def clear(r, carry):
            start = pl.multiple_of(r * rows, rows)
            x1_scr[prv, pl.ds(start, rows), :] = jnp.zeros((rows, d), F32)
            h2_scr[prv, pl.ds(start, rows), :] = jnp.zeros((rows, d), BF16)
            return carry

        lax.fori_loop(0, tm // rows, clear, 0)

    @pl.when(jnp.logical_or(i == 0, (i + tps - 1) % tps == 0))
    def _():
        up_scr[0:halo, :] = jnp.zeros((halo, up_scr.shape[1]), F32)

    ya = _dot(a_ref[...], wa_ref[...])
    yb = _dot(o_ref[...], wb_ref[...])
    h2p = h2_scr[prv]

    def conv_block(cols):
        up = _dot(h2p, wup_ref[:, cols])
        up_scr[halo:halo + tm, cols] = up
        y = cb_ref[:, cols] + cw_ref[CONV_W - 1:CONV_W, cols] * up
        for j in range(CONV_W - 1):
            back = CONV_W - 1 - j
            y = y + cw_ref[j:j + 1, cols] * up_scr[halo - back:halo - back + tm, cols]
        up_scr[0:halo, cols] = up[tm - halo:, :]
        return y

    acts = []
    for off in range(0, dff, FF_BLOCK):
        gate = conv_block(slice(off, off + FF_BLOCK))
        val = conv_block(slice(dff + off, dff + off + FF_BLOCK))
        acts.append((gate * val * _twice_sigmoid_of_twice(gate)).astype(BF16))
    act = jnp.concatenate(acts, axis=1)
    out_ref[...] = x1_scr[prv] + modp_ref[5:6, :] * _dot(act, wdn_ref[...])

    merged = (_twice_sigmoid_of_twice(ga_ref[...].astype(F32)) * ya
              + _twice_sigmoid_of_twice(gb_ref[...].astype(F32)) * yb)
    z1 = _dot(merged.astype(BF16), wo_ref[...])
    x1 = x_ref[...] + modn_ref[2:3, :] * z1
    x1_scr[nxt] = x1
    h2 = (x1 * _rms(x1, d)) * (g2_ref[...] * (1.0 + modn_ref[4:5, :])) + modn_ref[3:4, :]
    h2_scr[nxt] = h2.astype(BF16)


def _merge_ffn(attn, hgo, ga, gb, x2, mod, wa, wb, wo, g2, wup, cw, cb, wdn, *, seq, tm):
    t, d = x2.shape
    tps = seq // tm
    n = t // tm
    cur = lambda i: jnp.minimum(i, n - 1)
    prev = lambda i: jnp.maximum(i - 1, 0)
    row = lambda w: pl.BlockSpec((tm, w), lambda i: (cur(i), 0))
    consts = [wa, wb, wo, g2, wup, cw, cb, wdn]
    return pl.pallas_call(
        functools.partial(_merge_ffn_kernel, tps=tps),
        grid=(n + 1,),
        in_specs=[row(attn.shape[1]), row(hgo.shape[1]), row(d), row(d), row(d),
                  pl.BlockSpec((None, 6, d), lambda i: (cur(i) // tps, 0, 0)),
                  pl.BlockSpec((None, 6, d), lambda i: (prev(i) // tps, 0, 0))]
                 + [_const_spec(a.shape) for a in consts],
        out_specs=pl.BlockSpec((tm, d), lambda i: (prev(i), 0)),
        out_shape=jax.ShapeDtypeStruct((t, d), F32),
        scratch_shapes=[pltpu.VMEM((2, tm, d), F32), pltpu.VMEM((2, tm, d), BF16),
                        pltpu.VMEM((tm + SUBLANES, wup.shape[1]), F32)],
        compiler_params=_params(("arbitrary",)),
        name="merge_ffn",
    )(attn, hgo, ga, gb, x2, mod, mod, *consts)


def _pad_heads(w, used, n_heads):
    rows = w.shape[0]
    w = w.reshape(rows, n_heads, used)
    w = jnp.pad(w, ((0, 0), (0, 0), (0, HEAD_PAD - used)))
    return w.reshape(rows, n_heads * HEAD_PAD)


class _Tiles(NamedTuple):
    token: int
    q: int
    kv: int
    diag: int
    attn_heads: int
    chunk: int
    chunks_per_step: int


def _tiles(seq):
    tl = _Tiles(token=min(256, seq), q=min(1024, seq), kv=min(512, seq), diag=min(256, seq),
                attn_heads=4, chunk=min(128, seq), chunks_per_step=max(1, min(4, seq // 128)))
    assert seq % tl.token == 0 and seq % tl.q == 0 and tl.q % tl.kv == 0 and tl.kv % tl.diag == 0
    assert tl.kv % tl.token == 0 and seq % (tl.chunk * tl.chunks_per_step) == 0
    assert MLA_HEADS % tl.attn_heads == 0
    return tl


def _layer(x2, pos, invf, mod, p, *, bsz, seq, layer):
    t, d = x2.shape
    tl = _tiles(seq)
    w_in = p["w_in"]
    o_ckv = Q_LORA
    o_h = o_ckv + KV_LORA + ROPE_DIM
    o_g = o_h + 4 * HG_WIDTH
    wcq = w_in[:, :o_ckv].astype(BF16)
    wckv = jnp.concatenate([w_in[:, o_ckv:o_ckv + KV_LORA],
                            jnp.zeros((d, NOPE_DIM), F32),
                            w_in[:, o_ckv + KV_LORA:o_h],
                            jnp.zeros((d, HEAD_PAD - QK_DIM), F32)], axis=1).astype(BF16)
    wh = w_in[:, o_h:o_g].astype(BF16)
    wg = w_in[:, o_g:].astype(BF16)
    wuq = _pad_heads(p["w_uq"], QK_DIM, MLA_HEADS).astype(BF16)
    wukv = p["w_ukv"].reshape(KV_LORA, MLA_HEADS, NOPE_DIM + V_DIM)
    wk = _pad_heads(wukv[:, :, :NOPE_DIM].reshape(KV_LORA, -1), NOPE_DIM, MLA_HEADS).astype(BF16)
    wvt = _pad_heads(wukv[:, :, NOPE_DIM:].reshape(KV_LORA, -1), V_DIM, MLA_HEADS).T.astype(BF16)
    pad_g = lambda g: jnp.pad(g, (0, HEAD_PAD - QK_DIM)).reshape(1, HEAD_PAD)

    tm = tl.token
    q, k, vt, hq, hf, hi, hg, ga, gb = _mixer_in(
        x2, pos.reshape(t // tm, 1, tm), invf, mod, p["norm1_g"].reshape(1, d), wcq, wckv, wh,
        wg, p["q_a_norm_g"].reshape(1, -1), wuq, p["kv_a_norm_g"].reshape(1, -1), wk, wvt,
        pad_g(p["q_norm_g"]), pad_g(p["k_norm_g"]), seq=seq, tm=tm, tk=tl.kv)

    hw = MLA_HEADS * HEAD_PAD
    score_bound = (BF16_MARGIN * QK_DIM * Q_SCALE * jnp.max(jnp.abs(p["q_norm_g"]))
                   * jnp.max(jnp.abs(p["k_norm_g"]))).reshape(1).astype(F32)
    attn = _mla_attention(score_bound, q.reshape(bsz, seq, hw), k.reshape(bsz, seq, hw), vt,
                          tq=tl.q, tk=tl.kv, diag=tl.diag, heads=tl.attn_heads)
    r3 = lambda a: a.reshape(bsz, seq, HG_WIDTH)
    hgo = _hgrn2(r3(hq), r3(hf), r3(hi), r3(hg), p["hg_lower_bound"],
                 p["hg_out_norm_g"].reshape(1, HG_DIM), chunk=tl.chunk,
                 chunks_per_step=tl.chunks_per_step, layer=layer)

    dff = p["w_down"].shape[0]
    halve_gate = jnp.asarray(np.repeat(np.float32([0.5, 1.0]), dff))
    return _merge_ffn(attn.reshape(t, -1), hgo.reshape(t, -1), ga, gb, x2, mod,
                      p["w_branch_a"].astype(BF16), p["w_branch_b"].astype(BF16),
                      (p["w_out"] * 0.5).astype(BF16), p["norm2_g"].reshape(1, d),
                      p["w_up"].astype(BF16), p["conv_w"] * halve_gate,
                      (p["conv_b"] * halve_gate).reshape(1, -1),
                      p["w_down"].astype(BF16), seq=seq, tm=tm)


def kernel(x, c, positions, w_ada, b_ada, norm1_g, w_in, q_a_norm_g, w_uq, kv_a_norm_g, w_ukv,
           q_norm_g, k_norm_g, hg_lower_bound, hg_out_norm_g, w_branch_a, w_branch_b, w_out,
           norm2_g, w_up, conv_w, conv_b, w_down):
    bsz, seq, d = x.shape
    depth = w_ada.shape[0]
    invf = (ROPE_THETA ** (-jnp.arange(0, ROPE_DIM, 2, dtype=F32) / ROPE_DIM)).reshape(HALF_ROPE, 1)
    x2 = x.reshape(bsz * seq, d)
    pos = positions.reshape(bsz * seq)
    for l in range(depth):
        mod = _ada_mod(c, w_ada, b_ada, l).reshape(bsz, 6, d)
        p = dict(norm1_g=norm1_g[l], w_in=w_in[l], q_a_norm_g=q_a_norm_g[l], w_uq=w_uq[l],
                 kv_a_norm_g=kv_a_norm_g[l], w_ukv=w_ukv[l], q_norm_g=q_norm_g[l],
                 k_norm_g=k_norm_g[l], hg_lower_bound=hg_lower_bound, hg_out_norm_g=hg_out_norm_g[l],
                 w_branch_a=w_branch_a[l], w_branch_b=w_branch_b[l], w_out=w_out[l],
                 norm2_g=norm2_g[l], w_up=w_up[l], conv_w=conv_w[l], conv_b=conv_b[l],
                 w_down=w_down[l])
        x2 = _layer(x2, pos, invf, mod, p, bsz=bsz, seq=seq, layer=l)
    return x2.reshape(bsz, seq, d)
```

```python
import functools
from typing import NamedTuple

import numpy as np
import jax
import jax.numpy as jnp
from jax import lax
from jax.experimental import pallas as pl
from jax.experimental.pallas import tpu as pltpu

F32 = jnp.float32
BF16 = jnp.bfloat16

LANES = 128
SUBLANES = 8
VMEM_LIMIT = 56 * 1024 * 1024

MLA_HEADS = 8
NOPE_DIM = 64
ROPE_DIM = 32
HALF_ROPE = ROPE_DIM // 2
QK_DIM = NOPE_DIM + ROPE_DIM
V_DIM = 64
Q_LORA = 512
KV_LORA = 256
ROPE_THETA = 10000.0
HG_HEADS = 4
HG_DIM = 128
HG_WIDTH = HG_HEADS * HG_DIM
CONV_W = 3
EPS = 1e-6
HEAD_PAD = LANES
FF_BLOCK = 2 * LANES
LOG2E = float(np.log2(np.e))
NEG_BIG = -1e30
Q_SCALE = LOG2E / float(np.sqrt(QK_DIM))
BF16_MARGIN = (1.0 + 2.0 ** -8) ** 2
SAFE_EXP2_SPAN = 80.0


def _dot(a, b):
    return jnp.dot(a, b, preferred_element_type=F32)


def _dot_nt(a, b):
    return lax.dot_general(a, b, (((1,), (1,)), ((), ())), preferred_element_type=F32)


def _dot_tn(a, b):
    return lax.dot_general(a, b, (((0,), (0,)), ((), ())), preferred_element_type=F32)


def _twice_sigmoid_of_twice(h):
    return 1.0 + jnp.tanh(h)


def _rms(x, width):
    return lax.rsqrt(jnp.sum(x * x, axis=-1, keepdims=True) * (1.0 / width) + EPS)


def _const_spec(shape):
    nd = len(shape)
    return pl.BlockSpec(shape, lambda *_: (0,) * nd, pipeline_mode=pl.Buffered(1))


def _params(sem):
    return pltpu.CompilerParams(dimension_semantics=sem, vmem_limit_bytes=VMEM_LIMIT)


def _ada_kernel(c_ref, w_ref, b_ref, o_ref):
    c = c_ref[...]
    w = w_ref[...]
    c_hi = c.astype(BF16)
    c_lo = (c - c_hi.astype(F32)).astype(BF16)
    w_hi = w.astype(BF16)
    w_lo = (w - w_hi.astype(F32)).astype(BF16)
    o_ref[...] = _dot(c_hi, w_hi) + _dot(c_hi, w_lo) + _dot(c_lo, w_hi) + b_ref[...]


def _ada_mod(c, w_ada, b_ada, layer):
    bsz, d = c.shape
    depth, _, n = w_ada.shape
    bn = 1024
    return pl.pallas_call(
        _ada_kernel,
        grid=(n // bn,),
        in_specs=[pl.BlockSpec((bsz, d), lambda j: (0, 0)),
                  pl.BlockSpec((None, d, bn), lambda j: (layer, 0, j)),
                  pl.BlockSpec((None, 1, bn), lambda j: (layer, 0, j))],
        out_specs=pl.BlockSpec((bsz, bn), lambda j: (0, j)),
        out_shape=jax.ShapeDtypeStruct((bsz, n), F32),
        compiler_params=_params(("arbitrary",)),
        name="ada_mod",
    )(c, w_ada, b_ada.reshape(depth, 1, n))


def _rope(t, cos_t, sin_lo, sin_hi):
    return (t * cos_t + pltpu.roll(t, LANES - HALF_ROPE, 1) * sin_lo
            + pltpu.roll(t, HALF_ROPE, 1) * sin_hi)


def _mixer_in_kernel(x_ref, pos_ref, invf_ref, mod_ref, g1_ref, wcq_ref, wckv_ref, wh_ref, wg_ref,
                     gqa_ref, wuq_ref, gkva_ref, wk_ref, wv_ref, gq_ref, gk_ref,
                     q_ref, k_ref, vt_ref, hq_ref, hf_ref, hi_ref, hg_ref, ga_ref, gb_ref):
    tm, d = x_ref.shape
    x = x_ref[...]
    h = (x * _rms(x, d)) * (g1_ref[...] * (1.0 + mod_ref[1:2, :])) + mod_ref[0:1, :]
    hb = h.astype(BF16)

    ang = invf_ref[...] * pos_ref[...].astype(F32)
    cos_h = jnp.cos(ang)
    sin_h = jnp.sin(ang)
    zeros = lambda r: jnp.zeros((r, tm), F32)
    cos_t = jnp.concatenate([jnp.ones((NOPE_DIM, tm), F32), cos_h, cos_h,
                             zeros(HEAD_PAD - QK_DIM)], axis=0).T
    sin_lo = jnp.concatenate([zeros(NOPE_DIM), -sin_h, zeros(HEAD_PAD - NOPE_DIM - HALF_ROPE)],
                             axis=0).T
    sin_hi = jnp.concatenate([zeros(NOPE_DIM + HALF_ROPE), sin_h, zeros(HEAD_PAD - QK_DIM)],
                             axis=0).T

    cq = _dot(hb, wcq_ref[...])
    ckv = _dot(hb, wckv_ref[...])
    hh = _dot(hb, wh_ref[:, :2 * HG_WIDTH])
    hq_ref[...] = (0.5 * hh[:, :HG_WIDTH]).astype(BF16)
    hf_ref[...] = 0.5 * hh[:, HG_WIDTH:]

    cqn = (cq * _rms(cq, Q_LORA) * gqa_ref[...]).astype(BF16)
    ckv_c = ckv[:, :KV_LORA]
    kr = ckv[:, KV_LORA:]
    ckvn = (ckv_c * _rms(ckv_c, KV_LORA) * gkva_ref[...]).astype(BF16)
    qf = _dot(cqn, wuq_ref[...])
    kf = _dot(ckvn, wk_ref[...])
    vt = _dot_nt(wv_ref[...], ckvn)
    hh = _dot(hb, wh_ref[:, 2 * HG_WIDTH:])
    hi_ref[...] = hh[:, :HG_WIDTH].astype(BF16)
    hg_ref[...] = (0.5 * hh[:, HG_WIDTH:]).astype(BF16)
    gg = _dot(hb, wg_ref[...])
    ga_ref[...] = (0.5 * gg[:, :d]).astype(BF16)
    gb_ref[...] = (0.5 * gg[:, d:]).astype(BF16)

    gq = gq_ref[...]
    for hd in range(MLA_HEADS):
        sl = slice(hd * HEAD_PAD, (hd + 1) * HEAD_PAD)
        qh = qf[:, sl]
        qn = qh * _rms(qh, QK_DIM) * gq
        q_ref[:, sl] = (_rope(qn, cos_t, sin_lo, sin_hi) * Q_SCALE).astype(BF16)

    gk = gk_ref[...]
    ss_rope = jnp.sum(kr * kr, axis=-1, keepdims=True)
    kr_rot = _rope(kr * gk, cos_t, sin_lo, sin_hi)
    for hd in range(MLA_HEADS):
        sl = slice(hd * HEAD_PAD, (hd + 1) * HEAD_PAD)
        kn = kf[:, sl]
        ss = jnp.sum(kn * kn, axis=-1, keepdims=True) + ss_rope
        k_ref[:, sl] = ((kn * gk + kr_rot) * lax.rsqrt(ss * (1.0 / QK_DIM) + EPS)).astype(BF16)
    first_sub = lax.broadcasted_iota(jnp.int32, (SUBLANES, tm), 0) == 0
    pieces = []
    for hd in range(MLA_HEADS):
        r0 = hd * HEAD_PAD
        pieces += [vt[r0:r0 + V_DIM],
                   jnp.where(first_sub, 1.0, vt[r0 + V_DIM:r0 + V_DIM + SUBLANES]),
                   vt[r0 + V_DIM + SUBLANES:r0 + HEAD_PAD]]
    vt_ref[...] = jnp.concatenate(pieces, axis=0).astype(BF16)


def _mixer_in(x2, pos, invf, mod, g1, wcq, wckv, wh, wg, gqa, wuq, gkva, wk, wv, gq, gk, *,
              seq, tm, tk):
    t, d = x2.shape
    tps = seq // tm
    per_kv = tk // tm
    hw = MLA_HEADS * HEAD_PAD
    row = lambda w: pl.BlockSpec((tm, w), lambda i: (i, 0))
    vt_spec = pl.BlockSpec((None, None, hw, tm),
                           lambda i: (i // tps, (i % tps) // per_kv, 0, (i % tps) % per_kv))
    consts = [g1, wcq, wckv, wh, wg, gqa, wuq, gkva, wk, wv, gq, gk]
    out_shapes = [jax.ShapeDtypeStruct((t, hw), BF16)] * 2 + [
        jax.ShapeDtypeStruct((t // seq, seq // tk, hw, tk), BF16)] + [
        jax.ShapeDtypeStruct((t, HG_WIDTH), BF16), jax.ShapeDtypeStruct((t, HG_WIDTH), F32),
        jax.ShapeDtypeStruct((t, HG_WIDTH), BF16), jax.ShapeDtypeStruct((t, HG_WIDTH), BF16),
        jax.ShapeDtypeStruct((t, d), BF16), jax.ShapeDtypeStruct((t, d), BF16)]
    return pl.pallas_call(
        _mixer_in_kernel,
        grid=(t // tm,),
        in_specs=[row(d),
                  pl.BlockSpec((None, 1, tm), lambda i: (i, 0, 0)),
                  _const_spec(invf.shape),
                  pl.BlockSpec((None, 6, d), lambda i: (i // tps, 0, 0))]
                 + [_const_spec(a.shape) for a in consts],
        out_specs=[row(hw)] * 2 + [vt_spec] + [row(HG_WIDTH)] * 4 + [row(d)] * 2,
        out_shape=out_shapes,
        compiler_params=_params(("arbitrary",)),
        name="mixer_in",
    )(x2, pos, invf, mod, *consts)


def _attn_kernel(bound_ref, q_ref, k_ref, vt_ref, o_ref, m_scr, acc_scr, *, tq, tk, diag, heads):
    i = pl.program_id(2)
    nsub = tq // tk
    head_slices = [slice(hd * HEAD_PAD, (hd + 1) * HEAD_PAD) for hd in range(heads)]
    bound = bound_ref[0]

    def run(bounded):
        if not bounded:
            acc_scr[...] = jnp.zeros(acc_scr.shape, F32)
            m_scr[...] = jnp.full(m_scr.shape, NEG_BIG, F32)

        def block(j, off, nkv, row0, diagonal, first=False):
            start = pl.multiple_of(j * tk + off, nkv)
            stack = lambda a, n: jnp.concatenate([a] * (n // SUBLANES), axis=0)
            rows = slice(row0, tq)

            def scores(hd):
                sl = head_slices[hd]
                s = _dot_nt(k_ref[pl.ds(start, nkv), sl], q_ref[rows, sl])
                if diagonal:
                    kv_i = lax.broadcasted_iota(jnp.int32, s.shape, 0)
                    q_i = lax.broadcasted_iota(jnp.int32, s.shape, 1)
                    s = jnp.where(kv_i <= q_i, s, NEG_BIG)
                return s

            def update(hd, s):
                vt = vt_ref[j, head_slices[hd], off:off + nkv]
                if bounded:
                    pv = _dot(vt, jnp.exp2(s - bound).astype(BF16))
                    acc_scr[hd, :, rows] = pv if first else acc_scr[hd, :, rows] + pv
                    return
                m_prev = m_scr[hd, :, rows]
                m_new = jnp.maximum(m_prev, jnp.max(s, axis=0, keepdims=True))
                p = jnp.exp2(s - stack(m_new, nkv)).astype(BF16)
                acc_scr[hd, :, rows] = (acc_scr[hd, :, rows]
                                        * stack(jnp.exp2(m_prev - m_new), HEAD_PAD) + _dot(vt, p))
                m_scr[hd, :, rows] = m_new

            ahead = 2
            pending = [scores(hd) for hd in range(min(ahead, heads))]
            for hd in range(heads):
                if hd + ahead < heads:
                    pending.append(scores(hd + ahead))
                update(hd, pending.pop(0))

        def body(j, carry):
            for r in range(nsub):
                block(j * nsub + r, 0, tk, 0, False)
            return carry

        for pos in range(0, tq, diag):
            block(i * nsub + pos // tk, pos % tk, diag, pos, True, first=bounded and pos == 0)
        lax.fori_loop(0, i, body, 0)
        outs = []
        for hd in range(0, heads, 2):
            pair = [acc_scr[h, :V_DIM, :] / acc_scr[h, V_DIM:V_DIM + 1, :] for h in (hd, hd + 1)]
            outs.append(jnp.concatenate(pair, axis=0).T)
        o_ref[...] = jnp.concatenate(outs, axis=-1).astype(o_ref.dtype)

    no_underflow = 2.0 * bound <= SAFE_EXP2_SPAN
    pl.when(no_underflow)(lambda: run(True))
    pl.when(jnp.logical_not(no_underflow))(lambda: run(False))


def _mla_attention(score_bound, q, k, vt, *, tq, tk, diag, heads):
    bsz, seq, hw = q.shape
    groups = MLA_HEADS // heads
    pw = heads * HEAD_PAD
    return pl.pallas_call(
        functools.partial(_attn_kernel, tq=tq, tk=tk, diag=diag, heads=heads),
        grid=(bsz, groups, seq // tq),
        in_specs=[pl.BlockSpec(memory_space=pltpu.SMEM),
                  pl.BlockSpec((None, tq, pw), lambda b, h, i: (b, i, h)),
                  pl.BlockSpec((None, seq, pw), lambda b, h, i: (b, 0, h)),
                  pl.BlockSpec((None, seq // tk, pw, tk), lambda b, h, i: (b, 0, h, 0))],
        out_specs=pl.BlockSpec((None, tq, heads * V_DIM), lambda b, h, i: (b, i, h)),
        out_shape=jax.ShapeDtypeStruct((bsz, seq, MLA_HEADS * V_DIM), BF16),
        scratch_shapes=[pltpu.VMEM((heads, SUBLANES, tq), F32),
                        pltpu.VMEM((heads, HEAD_PAD, tq), F32)],
        compiler_params=_params(("arbitrary", "arbitrary", "arbitrary")),
        name="mla_attn",
    )(score_bound, q, k, vt)


def _hgrn_sum_masks(c):
    t = np.arange(c)[:, None]
    s = np.arange(c)[None, :]
    blocks = [s <= t, s > t]
    blk = c // 2
    while blk >= 2:
        bnd = (t & ~(2 * blk - 1)) + blk - 1
        blocks.append(((s > bnd) & (s <= t)) | ((s > t) & (s <= bnd)))
        blk //= 2
    em = np.concatenate(blocks, axis=0).astype(np.float32)
    return np.concatenate([em, em], axis=1)


def _hgrn_kernel(hq_ref, hf_ref, hi_ref, hg_ref, lbt_ref, gout_ref, em_ref, o_ref, st_ref, *,
                 layer, chunk):
    @pl.when(pl.program_id(1) == 0)
    def _():
        st_ref[...] = jnp.zeros_like(st_ref)

    for sub in range(hq_ref.shape[0] // chunk):
        rs = pl.ds(sub * chunk, chunk)
        _hgrn_chunk(hq_ref.at[rs], hf_ref.at[rs], hi_ref.at[rs], hg_ref.at[rs], lbt_ref, gout_ref,
                    em_ref, o_ref.at[rs], st_ref, layer=layer)


def _hgrn_chunk(hq_ref, hf_ref, hi_ref, hg_ref, lbt_ref, gout_ref, em_ref, o_ref, st_ref, *, layer):
    c, w = hq_ref.shape
    tab = lbt_ref[...]
    e = jnp.exp(tab - jnp.max(tab, axis=0, keepdims=True))
    lb = jnp.sum(e[1:layer + 2], axis=0, keepdims=True) / jnp.sum(e, axis=0, keepdims=True)

    hq = hq_ref[...].astype(F32)
    q = hq * _twice_sigmoid_of_twice(hq)
    f = lb + (0.5 - 0.5 * lb) * _twice_sigmoid_of_twice(hf_ref[...])
    lf = jnp.log2(f)
    kk = 1.0 - f
    v = hi_ref[...]

    lf_hi = lf.astype(BF16)
    lf_mid = (lf - lf_hi.astype(F32)).astype(BF16)
    sums = _dot(em_ref[...], jnp.concatenate([lf_hi, lf_mid], axis=0))
    b = sums[0:c]
    q_dec = (q * jnp.exp2(b)).astype(BF16)
    k_last = (kk * jnp.exp2(sums[c:2 * c])).astype(BF16)
    b_last = b[c - 1:c, :]

    n_levels = int(np.log2(c))
    sub = lax.broadcasted_iota(jnp.int32, (SUBLANES, w), 0)
    xs = []
    for lvl in range(n_levels):
        blk = 1 << lvl
        if blk >= SUBLANES:
            base = jnp.concatenate([(q if (j % 2) else kk)[j * blk:(j + 1) * blk]
                                    for j in range(c // blk)], axis=0)
        else:
            odd = jnp.concatenate([(sub & blk) != 0] * (c // SUBLANES), axis=0)
            base = jnp.where(odd, q * f, kk) if blk == 1 else jnp.where(odd, q, kk)
        if blk > 1:
            row0 = (2 + n_levels - 1 - lvl) * c
            base = base * jnp.exp2(sums[row0:row0 + c])
        xs.append(base.astype(BF16))

    t_i = lax.broadcasted_iota(jnp.int32, (c, c), 0)
    s_i = lax.broadcasted_iota(jnp.int32, (c, c), 1)
    lower_xor = jnp.where(t_i > s_i, t_i ^ s_i, 0)
    level_of = [(lower_xor >> lvl) == 1 for lvl in range(n_levels)]
    on_diag = t_i == s_i
    qk = q * kk
    gout = gout_ref[...]
    head_slices = [slice(hd * HG_DIM, (hd + 1) * HG_DIM) for hd in range(HG_HEADS)]
    attns = [jnp.where(on_diag, jnp.sum(qk[:, hs], axis=-1, keepdims=True), 0.0)
             for hs in head_slices]
    for x, keep in zip(xs, level_of):
        attns = [jnp.where(keep, _dot_nt(x[:, hs], x[:, hs]), a) for hs, a in zip(head_slices, attns)]
    for hd, hs in enumerate(head_slices):
        attn = attns[hd]
        st = st_ref[hd]
        o = _dot_nt(q_dec[:, hs], st.astype(BF16)) + _dot(attn.astype(BF16), v[:, hs])
        hg = hg_ref[:, hs].astype(F32)
        o = o * _rms(o, HG_DIM) * gout * (hg * _twice_sigmoid_of_twice(hg))
        o_ref[:, hs] = o.astype(o_ref.dtype)
        st_ref[hd] = st * jnp.exp2(b_last[:, hs]) + _dot_tn(v[:, hs], k_last[:, hs])


def _hgrn2(hq, hf, hi, hg, lb_table, gout, *, chunk, chunks_per_step, layer):
    bsz, seq, w = hq.shape
    rows = chunk * chunks_per_step
    blk = pl.BlockSpec((None, rows, w), lambda b, c: (b, c, 0))
    em = jnp.asarray(_hgrn_sum_masks(chunk), dtype=BF16)
    return pl.pallas_call(
        functools.partial(_hgrn_kernel, layer=layer, chunk=chunk),
        grid=(bsz, seq // rows),
        in_specs=[blk, blk, blk, blk, _const_spec(lb_table.shape), _const_spec(gout.shape),
                  _const_spec(em.shape)],
        out_specs=blk,
        out_shape=jax.ShapeDtypeStruct((bsz, seq, w), BF16),
        scratch_shapes=[pltpu.VMEM((HG_HEADS, HG_DIM, HG_DIM), F32)],
        compiler_params=_params(("arbitrary", "arbitrary")),
        name="hgrn2",
    )(hq, hf, hi, hg, lb_table, gout, em)


def _merge_ffn_kernel(a_ref, o_ref, ga_ref, gb_ref, x_ref, modn_ref, modp_ref, wa_ref, wb_ref, wo_ref,
                      g2_ref, wup_ref, cw_ref, cb_ref, wdn_ref, out_ref, x1_scr, h2_scr, up_scr, *, tps):
    tm, d = x_ref.shape
    dff = wdn_ref.shape[0]
    halo = SUBLANES
    i = pl.program_id(0)
    nxt = i % 2
    prv = 1 - nxt

    @pl.when(i == 0)
    def _():
        rows = 2 * SUBLANES

        def clear(r, carry):
            start = pl.multiple_of(r * rows, rows)
            x1_scr[prv, pl.ds(start, rows), :] = jnp.zeros((rows, d), F32)
            h2_scr[prv, pl.ds(start, rows), :] = jnp.zeros((rows, d), BF16)
            return carry

        lax.fori_loop(0, tm // rows, clear, 0)

    @pl.when(jnp.logical_or(i == 0, (i + tps - 1) % tps == 0))
    def _():
        up_scr[0:halo, :] = jnp.zeros((halo, up_scr.shape[1]), F32)

    ya = _dot(a_ref[...], wa_ref[...])
    yb = _dot(o_ref[...], wb_ref[...])
    h2p = h2_scr[prv]

    def conv_block(cols):
        up = _dot(h2p, wup_ref[:, cols])
        up_scr[halo:halo + tm, cols] = up
        y = cb_ref[:, cols] + cw_ref[CONV_W - 1:CONV_W, cols] * up
        for j in range(CONV_W - 1):
            back = CONV_W - 1 - j
            y = y + cw_ref[j:j + 1, cols] * up_scr[halo - back:halo - back + tm, cols]
        up_scr[0:halo, cols] = up[tm - halo:, :]
        return y

    acts = []
    for off in range(0, dff, FF_BLOCK):
        gate = conv_block(slice(off, off + FF_BLOCK))
        val = conv_block(slice(dff + off, dff + off + FF_BLOCK))
        acts.append((gate * val * _twice_sigmoid_of_twice(gate)).astype(BF16))
    act = jnp.concatenate(acts, axis=1)
    out_ref[...] = x1_scr[prv] + modp_ref[5:6, :] * _dot(act, wdn_ref[...])

    merged = (_twice_sigmoid_of_twice(ga_ref[...].astype(F32)) * ya
              + _twice_sigmoid_of_twice(gb_ref[...].astype(F32)) * yb)
    z1 = _dot(merged.astype(BF16), wo_ref[...])
    x1 = x_ref[...] + modn_ref[2:3, :] * z1
    x1_scr[nxt] = x1
    h2 = (x1 * _rms(x1, d)) * (g2_ref[...] * (1.0 + modn_ref[4:5, :])) + modn_ref[3:4, :]
    h2_scr[nxt] = h2.astype(BF16)


def _merge_ffn(attn, hgo, ga, gb, x2, mod, wa, wb, wo, g2, wup, cw, cb, wdn, *, seq, tm):
    t, d = x2.shape
    tps = seq // tm
    n = t // tm
    cur = lambda i: jnp.minimum(i, n - 1)
    prev = lambda i: jnp.maximum(i - 1, 0)
    row = lambda w: pl.BlockSpec((tm, w), lambda i: (cur(i), 0))
    consts = [wa, wb, wo, g2, wup, cw, cb, wdn]
    return pl.pallas_call(
        functools.partial(_merge_ffn_kernel, tps=tps),
        grid=(n + 1,),
        in_specs=[row(attn.shape[1]), row(hgo.shape[1]), row(d), row(d), row(d),
                  pl.BlockSpec((None, 6, d), lambda i: (cur(i) // tps, 0, 0)),
                  pl.BlockSpec((None, 6, d), lambda i: (prev(i) // tps, 0, 0))]
                 + [_const_spec(a.shape) for a in consts],
        out_specs=pl.BlockSpec((tm, d), lambda i: (prev(i), 0)),
        out_shape=jax.ShapeDtypeStruct((t, d), F32),
        scratch_shapes=[pltpu.VMEM((2, tm, d), F32), pltpu.VMEM((2, tm, d), BF16),
                        pltpu.VMEM((tm + SUBLANES, wup.shape[1]), F32)],
        compiler_params=_params(("arbitrary",)),
        name="merge_ffn",
    )(attn, hgo, ga, gb, x2, mod, mod, *consts)


def _pad_heads(w, used, n_heads):
    rows = w.shape[0]
    w = w.reshape(rows, n_heads, used)
    w = jnp.pad(w, ((0, 0), (0, 0), (0, HEAD_PAD - used)))
    return w.reshape(rows, n_heads * HEAD_PAD)


class _Tiles(NamedTuple):
    token: int
    q: int
    kv: int
    diag: int
    attn_heads: int
    chunk: int
    chunks_per_step: int


def _tiles(seq):
    tl = _Tiles(token=min(256, seq), q=min(1024, seq), kv=min(512, seq), diag=min(256, seq),
                attn_heads=4, chunk=min(128, seq), chunks_per_step=max(1, min(8, seq // 128)))
    assert seq % tl.token == 0 and seq % tl.q == 0 and tl.q % tl.kv == 0 and tl.kv % tl.diag == 0
    assert tl.kv % tl.token == 0 and seq % (tl.chunk * tl.chunks_per_step) == 0
    assert MLA_HEADS % tl.attn_heads == 0
    return tl


def _layer(x2, pos, invf, mod, p, *, bsz, seq, layer):
    t, d = x2.shape
    tl = _tiles(seq)
    w_in = p["w_in"]
    o_ckv = Q_LORA
    o_h = o_ckv + KV_LORA + ROPE_DIM
    o_g = o_h + 4 * HG_WIDTH
    wcq = w_in[:, :o_ckv].astype(BF16)
    wckv = jnp.concatenate([w_in[:, o_ckv:o_ckv + KV_LORA],
                            jnp.zeros((d, NOPE_DIM), F32),
                            w_in[:, o_ckv + KV_LORA:o_h],
                            jnp.zeros((d, HEAD_PAD - QK_DIM), F32)], axis=1).astype(BF16)
    wh = w_in[:, o_h:o_g].astype(BF16)
    wg = w_in[:, o_g:].astype(BF16)
    wuq = _pad_heads(p["w_uq"], QK_DIM, MLA_HEADS).astype(BF16)
    wukv = p["w_ukv"].reshape(KV_LORA, MLA_HEADS, NOPE_DIM + V_DIM)
    wk = _pad_heads(wukv[:, :, :NOPE_DIM].reshape(KV_LORA, -1), NOPE_DIM, MLA_HEADS).astype(BF16)
    wvt = _pad_heads(wukv[:, :, NOPE_DIM:].reshape(KV_LORA, -1), V_DIM, MLA_HEADS).T.astype(BF16)
    pad_g = lambda g: jnp.pad(g, (0, HEAD_PAD - QK_DIM)).reshape(1, HEAD_PAD)

    tm = tl.token
    q, k, vt, hq, hf, hi, hg, ga, gb = _mixer_in(
        x2, pos.reshape(t // tm, 1, tm), invf, mod, p["norm1_g"].reshape(1, d), wcq, wckv, wh,
        wg, p["q_a_norm_g"].reshape(1, -1), wuq, p["kv_a_norm_g"].reshape(1, -1), wk, wvt,
        pad_g(p["q_norm_g"]), pad_g(p["k_norm_g"]), seq=seq, tm=tm, tk=tl.kv)

    hw = MLA_HEADS * HEAD_PAD
    score_bound = (BF16_MARGIN * QK_DIM * Q_SCALE * jnp.max(jnp.abs(p["q_norm_g"]))
                   * jnp.max(jnp.abs(p["k_norm_g"]))).reshape(1).astype(F32)
    attn = _mla_attention(score_bound, q.reshape(bsz, seq, hw), k.reshape(bsz, seq, hw), vt,
                          tq=tl.q, tk=tl.kv, diag=tl.diag, heads=tl.attn_heads)
    r3 = lambda a: a.reshape(bsz, seq, HG_WIDTH)
    hgo = _hgrn2(r3(hq), r3(hf), r3(hi), r3(hg), p["hg_lower_bound"],
                 p["hg_out_norm_g"].reshape(1, HG_DIM), chunk=tl.chunk,
                 chunks_per_step=tl.chunks_per_step, layer=layer)

    dff = p["w_down"].shape[0]
    halve_gate = jnp.asarray(np.repeat(np.float32([0.5, 1.0]), dff))
    return _merge_ffn(attn.reshape(t, -1), hgo.reshape(t, -1), ga, gb, x2, mod,
                      p["w_branch_a"].astype(BF16), p["w_branch_b"].astype(BF16),
                      (p["w_out"] * 0.5).astype(BF16), p["norm2_g"].reshape(1, d),
                      p["w_up"].astype(BF16), p["conv_w"] * halve_gate,
                      (p["conv_b"] * halve_gate).reshape(1, -1),
                      p["w_down"].astype(BF16), seq=seq, tm=tm)


def kernel(x, c, positions, w_ada, b_ada, norm1_g, w_in, q_a_norm_g, w_uq, kv_a_norm_g, w_ukv,
           q_norm_g, k_norm_g, hg_lower_bound, hg_out_norm_g, w_branch_a, w_branch_b, w_out,
           norm2_g, w_up, conv_w, conv_b, w_down):
    bsz, seq, d = x.shape
    depth = w_ada.shape[0]
    invf = (ROPE_THETA ** (-jnp.arange(0, ROPE_DIM, 2, dtype=F32) / ROPE_DIM)).reshape(HALF_ROPE, 1)
    x2 = x.reshape(bsz * seq, d)
    pos = positions.reshape(bsz * seq)
    for l in range(depth):
        mod = _ada_mod(c, w_ada, b_ada, l).reshape(bsz, 6, d)
        p = dict(norm1_g=norm1_g[l], w_in=w_in[l], q_a_norm_g=q_a_norm_g[l], w_uq=w_uq[l],
                 kv_a_norm_g=kv_a_norm_g[l], w_ukv=w_ukv[l], q_norm_g=q_norm_g[l],
                 k_norm_g=k_norm_g[l], hg_lower_bound=hg_lower_bound, hg_out_norm_g=hg_out_norm_g[l],
                 w_branch_a=w_branch_a[l], w_branch_b=w_branch_b[l], w_out=w_out[l],
                 norm2_g=norm2_g[l], w_up=w_up[l], conv_w=conv_w[l], conv_b=conv_b[l],
                 w_down=w_down[l])
        x2 = _layer(x2, pos, invf, mod, p, bsz=bsz, seq=seq, layer=l)
    return x2.reshape(bsz, seq, d)
```

```python
import functools
from typing import NamedTuple

import numpy as np
import jax
import jax.numpy as jnp
from jax import lax
from jax.experimental import pallas as pl
from jax.experimental.pallas import tpu as pltpu

F32 = jnp.float32
BF16 = jnp.bfloat16

LANES = 128
SUBLANES = 8
VMEM_LIMIT = 56 * 1024 * 1024
FFN_VMEM_LIMIT = 60 * 1024 * 1024

MLA_HEADS = 8
NOPE_DIM = 64
ROPE_DIM = 32
HALF_ROPE = ROPE_DIM // 2
QK_DIM = NOPE_DIM + ROPE_DIM
V_DIM = 64
Q_LORA = 512
KV_LORA = 256
ROPE_THETA = 10000.0
HG_HEADS = 4
HG_DIM = 128
HG_WIDTH = HG_HEADS * HG_DIM
CONV_W = 3
EPS = 1e-6
HEAD_PAD = LANES
FF_BLOCK = 2 * LANES
LOG2E = float(np.log2(np.e))
NEG_BIG = -1e30
Q_SCALE = LOG2E / float(np.sqrt(QK_DIM))
BF16_MARGIN = (1.0 + 2.0 ** -8) ** 2
SAFE_EXP2_SPAN = 80.0


def _dot(a, b):
    return jnp.dot(a, b, preferred_element_type=F32)


def _dot_nt(a, b):
    return lax.dot_general(a, b, (((1,), (1,)), ((), ())), preferred_element_type=F32)


def _dot_tn(a, b):
    return lax.dot_general(a, b, (((0,), (0,)), ((), ())), preferred_element_type=F32)


def _twice_sigmoid_of_twice(h):
    return 1.0 + jnp.tanh(h)


def _rms(x, width):
    return lax.rsqrt(jnp.sum(x * x, axis=-1, keepdims=True) * (1.0 / width) + EPS)


def _const_spec(shape):
    nd = len(shape)
    return pl.BlockSpec(shape, lambda *_: (0,) * nd, pipeline_mode=pl.Buffered(1))


def _params(sem, vmem_limit=VMEM_LIMIT):
    return pltpu.CompilerParams(dimension_semantics=sem, vmem_limit_bytes=vmem_limit)


def _ada_kernel(c_ref, w_ref, b_ref, o_ref):
    c = c_ref[...]
    w = w_ref[...]
    c_hi = c.astype(BF16)
    c_lo = (c - c_hi.astype(F32)).astype(BF16)
    w_hi = w.astype(BF16)
    w_lo = (w - w_hi.astype(F32)).astype(BF16)
    o_ref[...] = _dot(c_hi, w_hi) + _dot(c_hi, w_lo) + _dot(c_lo, w_hi) + b_ref[...]


def _ada_mod(c, w_ada, b_ada, layer):
    bsz, d = c.shape
    depth, _, n = w_ada.shape
    bn = 1024
    return pl.pallas_call(
        _ada_kernel,
        grid=(n // bn,),
        in_specs=[pl.BlockSpec((bsz, d), lambda j: (0, 0)),
                  pl.BlockSpec((None, d, bn), lambda j: (layer, 0, j)),
                  pl.BlockSpec((None, 1, bn), lambda j: (layer, 0, j))],
        out_specs=pl.BlockSpec((bsz, bn), lambda j: (0, j)),
        out_shape=jax.ShapeDtypeStruct((bsz, n), F32),
        compiler_params=_params(("arbitrary",)),
        name="ada_mod",
    )(c, w_ada, b_ada.reshape(depth, 1, n))


def _rope(t, cos_t, sin_lo, sin_hi):
    return (t * cos_t + pltpu.roll(t, LANES - HALF_ROPE, 1) * sin_lo
            + pltpu.roll(t, HALF_ROPE, 1) * sin_hi)


def _mixer_in_kernel(x_ref, pos_ref, invf_ref, mod_ref, g1_ref, wcq_ref, wckv_ref, wh_ref, wg_ref,
                     gqa_ref, wuq_ref, gkva_ref, wk_ref, wv_ref, gq_ref, gk_ref,
                     q_ref, k_ref, vt_ref, hq_ref, hf_ref, hi_ref, hg_ref, ga_ref, gb_ref):
    tm, d = x_ref.shape
    x = x_ref[...]
    h = (x * _rms(x, d)) * (g1_ref[...] * (1.0 + mod_ref[1:2, :])) + mod_ref[0:1, :]
    hb = h.astype(BF16)

    ang = invf_ref[...] * pos_ref[...].astype(F32)
    cos_h = jnp.cos(ang)
    sin_h = jnp.sin(ang)
    zeros = lambda r: jnp.zeros((r, tm), F32)
    cos_t = jnp.concatenate([jnp.ones((NOPE_DIM, tm), F32), cos_h, cos_h,
                             zeros(HEAD_PAD - QK_DIM)], axis=0).T
    sin_lo = jnp.concatenate([zeros(NOPE_DIM), -sin_h, zeros(HEAD_PAD - NOPE_DIM - HALF_ROPE)],
                             axis=0).T
    sin_hi = jnp.concatenate([zeros(NOPE_DIM + HALF_ROPE), sin_h, zeros(HEAD_PAD - QK_DIM)],
                             axis=0).T

    cq = _dot(hb, wcq_ref[...])
    ckv = _dot(hb, wckv_ref[...])
    hh = _dot(hb, wh_ref[:, :2 * HG_WIDTH])
    hq_ref[...] = (0.5 * hh[:, :HG_WIDTH]).astype(BF16)
    hf_ref[...] = 0.5 * hh[:, HG_WIDTH:]

    cqn = (cq * _rms(cq, Q_LORA) * gqa_ref[...]).astype(BF16)
    ckv_c = ckv[:, :KV_LORA]
    kr = ckv[:, KV_LORA:]
    ckvn = (ckv_c * _rms(ckv_c, KV_LORA) * gkva_ref[...]).astype(BF16)
    qf = _dot(cqn, wuq_ref[...])
    kf = _dot(ckvn, wk_ref[...])
    vt = _dot_nt(wv_ref[...], ckvn)
    hh = _dot(hb, wh_ref[:, 2 * HG_WIDTH:])
    hi_ref[...] = hh[:, :HG_WIDTH].astype(BF16)
    hg_ref[...] = (0.5 * hh[:, HG_WIDTH:]).astype(BF16)
    gg = _dot(hb, wg_ref[...])
    ga_ref[...] = (0.5 * gg[:, :d]).astype(BF16)
    gb_ref[...] = (0.5 * gg[:, d:]).astype(BF16)

    gq = gq_ref[...]
    for hd in range(MLA_HEADS):
        sl = slice(hd * HEAD_PAD, (hd + 1) * HEAD_PAD)
        qh = qf[:, sl]
        qn = qh * _rms(qh, QK_DIM) * gq
        q_ref[:, sl] = (_rope(qn, cos_t, sin_lo, sin_hi) * Q_SCALE).astype(BF16)

    gk = gk_ref[...]
    ss_rope = jnp.sum(kr * kr, axis=-1, keepdims=True)
    kr_rot = _rope(kr * gk, cos_t, sin_lo, sin_hi)
    for hd in range(MLA_HEADS):
        sl = slice(hd * HEAD_PAD, (hd + 1) * HEAD_PAD)
        kn = kf[:, sl]
        ss = jnp.sum(kn * kn, axis=-1, keepdims=True) + ss_rope
        k_ref[:, sl] = ((kn * gk + kr_rot) * lax.rsqrt(ss * (1.0 / QK_DIM) + EPS)).astype(BF16)
    first_sub = lax.broadcasted_iota(jnp.int32, (SUBLANES, tm), 0) == 0
    pieces = []
    for hd in range(MLA_HEADS):
        r0 = hd * HEAD_PAD
        pieces += [vt[r0:r0 + V_DIM],
                   jnp.where(first_sub, 1.0, vt[r0 + V_DIM:r0 + V_DIM + SUBLANES]),
                   vt[r0 + V_DIM + SUBLANES:r0 + HEAD_PAD]]
    vt_ref[...] = jnp.concatenate(pieces, axis=0).astype(BF16)


def _mixer_in(x2, pos, invf, mod, g1, wcq, wckv, wh, wg, gqa, wuq, gkva, wk, wv, gq, gk, *,
              seq, tm, tk):
    t, d = x2.shape
    tps = seq // tm
    per_kv = tk // tm
    hw = MLA_HEADS * HEAD_PAD
    row = lambda w: pl.BlockSpec((tm, w), lambda i: (i, 0))
    vt_spec = pl.BlockSpec((None, None, hw, tm),
                           lambda i: (i // tps, (i % tps) // per_kv, 0, (i % tps) % per_kv))
    consts = [g1, wcq, wckv, wh, wg, gqa, wuq, gkva, wk, wv, gq, gk]
    out_shapes = [jax.ShapeDtypeStruct((t, hw), BF16)] * 2 + [
        jax.ShapeDtypeStruct((t // seq, seq // tk, hw, tk), BF16)] + [
        jax.ShapeDtypeStruct((t, HG_WIDTH), BF16), jax.ShapeDtypeStruct((t, HG_WIDTH), F32),
        jax.ShapeDtypeStruct((t, HG_WIDTH), BF16), jax.ShapeDtypeStruct((t, HG_WIDTH), BF16),
        jax.ShapeDtypeStruct((t, d), BF16), jax.ShapeDtypeStruct((t, d), BF16)]
    return pl.pallas_call(
        _mixer_in_kernel,
        grid=(t // tm,),
        in_specs=[row(d),
                  pl.BlockSpec((None, 1, tm), lambda i: (i, 0, 0)),
                  _const_spec(invf.shape),
                  pl.BlockSpec((None, 6, d), lambda i: (i // tps, 0, 0))]
                 + [_const_spec(a.shape) for a in consts],
        out_specs=[row(hw)] * 2 + [vt_spec] + [row(HG_WIDTH)] * 4 + [row(d)] * 2,
        out_shape=out_shapes,
        compiler_params=_params(("arbitrary",)),
        name="mixer_in",
    )(x2, pos, invf, mod, *consts)


def _attn_kernel(bound_ref, q_ref, k_ref, vt_ref, o_ref, m_scr, acc_scr, *, tq, tk, diag, heads):
    i = pl.program_id(2)
    nsub = tq // tk
    head_slices = [slice(hd * HEAD_PAD, (hd + 1) * HEAD_PAD) for hd in range(heads)]
    bound = bound_ref[0]

    def run(bounded):
        if not bounded:
            acc_scr[...] = jnp.zeros(acc_scr.shape, F32)
            m_scr[...] = jnp.full(m_scr.shape, NEG_BIG, F32)

        def block(j, off, nkv, row0, diagonal, first=False):
            start = pl.multiple_of(j * tk + off, nkv)
            stack = lambda a, n: jnp.concatenate([a] * (n // SUBLANES), axis=0)
            rows = slice(row0, tq)

            def scores(hd):
                sl = head_slices[hd]
                s = _dot_nt(k_ref[pl.ds(start, nkv), sl], q_ref[rows, sl])
                if diagonal:
                    kv_i = lax.broadcasted_iota(jnp.int32, s.shape, 0)
                    q_i = lax.broadcasted_iota(jnp.int32, s.shape, 1)
                    s = jnp.where(kv_i <= q_i, s, NEG_BIG)
                return s

            def update(hd, s):
                vt = vt_ref[j, head_slices[hd], off:off + nkv]
                if bounded:
                    pv = _dot(vt, jnp.exp2(s - bound).astype(BF16))
                    acc_scr[hd, :, rows] = pv if first else acc_scr[hd, :, rows] + pv
                    return
                m_prev = m_scr[hd, :, rows]
                m_new = jnp.maximum(m_prev, jnp.max(s, axis=0, keepdims=True))
                p = jnp.exp2(s - stack(m_new, nkv)).astype(BF16)
                acc_scr[hd, :, rows] = (acc_scr[hd, :, rows]
                                        * stack(jnp.exp2(m_prev - m_new), HEAD_PAD) + _dot(vt, p))
                m_scr[hd, :, rows] = m_new

            ahead = 2
            pending = [scores(hd) for hd in range(min(ahead, heads))]
            for hd in range(heads):
                if hd + ahead < heads:
                    pending.append(scores(hd + ahead))
                update(hd, pending.pop(0))

        def body(j, carry):
            for r in range(nsub):
                block(j * nsub + r, 0, tk, 0, False)
            return carry

        for pos in range(0, tq, diag):
            block(i * nsub + pos // tk, pos % tk, diag, pos, True, first=bounded and pos == 0)
        lax.fori_loop(0, i, body, 0)
        outs = []
        for hd in range(0, heads, 2):
            pair = [acc_scr[h, :V_DIM, :] / acc_scr[h, V_DIM:V_DIM + 1, :] for h in (hd, hd + 1)]
            outs.append(jnp.concatenate(pair, axis=0).T)
        o_ref[...] = jnp.concatenate(outs, axis=-1).astype(o_ref.dtype)

    no_underflow = 2.0 * bound <= SAFE_EXP2_SPAN
    pl.when(no_underflow)(lambda: run(True))
    pl.when(jnp.logical_not(no_underflow))(lambda: run(False))


def _mla_attention(score_bound, q, k, vt, *, tq, tk, diag, heads):
    bsz, seq, hw = q.shape
    groups = MLA_HEADS // heads
    pw = heads * HEAD_PAD
    return pl.pallas_call(
        functools.partial(_attn_kernel, tq=tq, tk=tk, diag=diag, heads=heads),
        grid=(bsz, groups, seq // tq),
        in_specs=[pl.BlockSpec(memory_space=pltpu.SMEM),
                  pl.BlockSpec((None, tq, pw), lambda b, h, i: (b, i, h)),
                  pl.BlockSpec((None, seq, pw), lambda b, h, i: (b, 0, h)),
                  pl.BlockSpec((None, seq // tk, pw, tk), lambda b, h, i: (b, 0, h, 0))],
        out_specs=pl.BlockSpec((None, tq, heads * V_DIM), lambda b, h, i: (b, i, h)),
        out_shape=jax.ShapeDtypeStruct((bsz, seq, MLA_HEADS * V_DIM), BF16),
        scratch_shapes=[pltpu.VMEM((heads, SUBLANES, tq), F32),
                        pltpu.VMEM((heads, HEAD_PAD, tq), F32)],
        compiler_params=_params(("arbitrary", "arbitrary", "arbitrary")),
        name="mla_attn",
    )(score_bound, q, k, vt)


def _hgrn_sum_masks(c):
    t = np.arange(c)[:, None]
    s = np.arange(c)[None, :]
    blocks = [s <= t, s > t]
    blk = c // 2
    while blk >= 2:
        bnd = (t & ~(2 * blk - 1)) + blk - 1
        blocks.append(((s > bnd) & (s <= t)) | ((s > t) & (s <= bnd)))
        blk //= 2
    em = np.concatenate(blocks, axis=0).astype(np.float32)
    return np.concatenate([em, em], axis=1)


def _hgrn_kernel(hq_ref, hf_ref, hi_ref, hg_ref, lbt_ref, gout_ref, em_ref, o_ref, st_ref, *,
                 layer, chunk):
    @pl.when(pl.program_id(1) == 0)
    def _():
        st_ref[...] = jnp.zeros_like(st_ref)

    for sub in range(hq_ref.shape[0] // chunk):
        rs = pl.ds(sub * chunk, chunk)
        _hgrn_chunk(hq_ref.at[rs], hf_ref.at[rs], hi_ref.at[rs], hg_ref.at[rs], lbt_ref, gout_ref,
                    em_ref, o_ref.at[rs], st_ref, layer=layer)


def _hgrn_chunk(hq_ref, hf_ref, hi_ref, hg_ref, lbt_ref, gout_ref, em_ref, o_ref, st_ref, *, layer):
    c, w = hq_ref.shape
    tab = lbt_ref[...]
    e = jnp.exp(tab - jnp.max(tab, axis=0, keepdims=True))
    lb = jnp.sum(e[1:layer + 2], axis=0, keepdims=True) / jnp.sum(e, axis=0, keepdims=True)

    hq = hq_ref[...].astype(F32)
    q = hq * _twice_sigmoid_of_twice(hq)
    f = lb + (0.5 - 0.5 * lb) * _twice_sigmoid_of_twice(hf_ref[...])
    lf = jnp.log2(f)
    kk = 1.0 - f
    v = hi_ref[...]

    lf_hi = lf.astype(BF16)
    lf_mid = (lf - lf_hi.astype(F32)).astype(BF16)
    sums = _dot(em_ref[...], jnp.concatenate([lf_hi, lf_mid], axis=0))
    b = sums[0:c]
    q_dec = (q * jnp.exp2(b)).astype(BF16)
    k_last = (kk * jnp.exp2(sums[c:2 * c])).astype(BF16)
    b_last = b[c - 1:c, :]

    n_levels = int(np.log2(c))
    sub = lax.broadcasted_iota(jnp.int32, (SUBLANES, w), 0)
    xs = []
    for lvl in range(n_levels):
        blk = 1 << lvl
        if blk >= SUBLANES:
            base = jnp.concatenate([(q if (j % 2) else kk)[j * blk:(j + 1) * blk]
                                    for j in range(c // blk)], axis=0)
        else:
            odd = jnp.concatenate([(sub & blk) != 0] * (c // SUBLANES), axis=0)
            base = jnp.where(odd, q * f, kk) if blk == 1 else jnp.where(odd, q, kk)
        if blk > 1:
            row0 = (2 + n_levels - 1 - lvl) * c
            base = base * jnp.exp2(sums[row0:row0 + c])
        xs.append(base.astype(BF16))

    t_i = lax.broadcasted_iota(jnp.int32, (c, c), 0)
    s_i = lax.broadcasted_iota(jnp.int32, (c, c), 1)
    lower_xor = jnp.where(t_i > s_i, t_i ^ s_i, 0)
    level_of = [(lower_xor >> lvl) == 1 for lvl in range(n_levels)]
    on_diag = t_i == s_i
    qk = q * kk
    gout = gout_ref[...]
    head_slices = [slice(hd * HG_DIM, (hd + 1) * HG_DIM) for hd in range(HG_HEADS)]
    attns = [jnp.where(on_diag, jnp.sum(qk[:, hs], axis=-1, keepdims=True), 0.0)
             for hs in head_slices]
    for x, keep in zip(xs, level_of):
        attns = [jnp.where(keep, _dot_nt(x[:, hs], x[:, hs]), a) for hs, a in zip(head_slices, attns)]
    for hd, hs in enumerate(head_slices):
        attn = attns[hd]
        st = st_ref[hd]
        o = _dot_nt(q_dec[:, hs], st.astype(BF16)) + _dot(attn.astype(BF16), v[:, hs])
        hg = hg_ref[:, hs].astype(F32)
        o = o * _rms(o, HG_DIM) * gout * (hg * _twice_sigmoid_of_twice(hg))
        o_ref[:, hs] = o.astype(o_ref.dtype)
        st_ref[hd] = st * jnp.exp2(b_last[:, hs]) + _dot_tn(v[:, hs], k_last[:, hs])


def _hgrn2(hq, hf, hi, hg, lb_table, gout, *, chunk, chunks_per_step, layer):
    bsz, seq, w = hq.shape
    rows = chunk * chunks_per_step
    blk = pl.BlockSpec((None, rows, w), lambda b, c: (b, c, 0))
    em = jnp.asarray(_hgrn_sum_masks(chunk), dtype=BF16)
    return pl.pallas_call(
        functools.partial(_hgrn_kernel, layer=layer, chunk=chunk),
        grid=(bsz, seq // rows),
        in_specs=[blk, blk, blk, blk, _const_spec(lb_table.shape), _const_spec(gout.shape),
                  _const_spec(em.shape)],
        out_specs=blk,
        out_shape=jax.ShapeDtypeStruct((bsz, seq, w), BF16),
        scratch_shapes=[pltpu.VMEM((HG_HEADS, HG_DIM, HG_DIM), F32)],
        compiler_params=_params(("arbitrary", "arbitrary")),
        name="hgrn2",
    )(hq, hf, hi, hg, lb_table, gout, em)


def _merge_ffn_kernel(a_ref, o_ref, ga_ref, gb_ref, x_ref, modn_ref, modp_ref, wa_ref, wb_ref, wo_ref,
                      g2_ref, wup_ref, cw_ref, cb_ref, wdn_ref, out_ref, x1_scr, h2_scr, up_scr, *, tps):
    tm, d = x_ref.shape
    dff = wdn_ref.shape[0]
    halo = SUBLANES
    i = pl.program_id(0)
    nxt = i % 2
    prv = 1 - nxt

    @pl.when(i == 0)
    def _():
        rows = 2 * SUBLANES

        def clear(r, carry):
            start = pl.multiple_of(r * rows, rows)
            x1_scr[prv, pl.ds(start, rows), :] = jnp.zeros((rows, d), F32)
            h2_scr[prv, pl.ds(start, rows), :] = jnp.zeros((rows, d), BF16)
            return carry

        lax.fori_loop(0, tm // rows, clear, 0)

    @pl.when(jnp.logical_or(i == 0, (i + tps - 1) % tps == 0))
    def _():
        up_scr[0:halo, :] = jnp.zeros((halo, up_scr.shape[1]), F32)

    ya = _dot(a_ref[...], wa_ref[...])
    yb = _dot(o_ref[...], wb_ref[...])
    h2p = h2_scr[prv]

    def conv_block(cols):
        up = _dot(h2p, wup_ref[:, cols])
        up_scr[halo:halo + tm, cols] = up
        y = cb_ref[:, cols] + cw_ref[CONV_W - 1:CONV_W, cols] * up
        for j in range(CONV_W - 1):
            back = CONV_W - 1 - j
            y = y + cw_ref[j:j + 1, cols] * up_scr[halo - back:halo - back + tm, cols]
        up_scr[0:halo, cols] = up[tm - halo:, :]
        return y

    acts = []
    for off in range(0, dff, FF_BLOCK):
        gate = conv_block(slice(off, off + FF_BLOCK))
        val = conv_block(slice(dff + off, dff + off + FF_BLOCK))
        acts.append((gate * val * _twice_sigmoid_of_twice(gate)).astype(BF16))
    act = jnp.concatenate(acts, axis=1)
    out_ref[...] = x1_scr[prv] + modp_ref[5:6, :] * _dot(act, wdn_ref[...])

    merged = (_twice_sigmoid_of_twice(ga_ref[...].astype(F32)) * ya
              + _twice_sigmoid_of_twice(gb_ref[...].astype(F32)) * yb)
    z1 = _dot(merged.astype(BF16), wo_ref[...])
    x1 = x_ref[...] + modn_ref[2:3, :] * z1
    x1_scr[nxt] = x1
    h2 = (x1 * _rms(x1, d)) * (g2_ref[...] * (1.0 + modn_ref[4:5, :])) + modn_ref[3:4, :]
    h2_scr[nxt] = h2.astype(BF16)


def _merge_ffn(attn, hgo, ga, gb, x2, mod, wa, wb, wo, g2, wup, cw, cb, wdn, *, seq, tm):
    t, d = x2.shape
    tps = seq // tm
    n = t // tm
    cur = lambda i: jnp.minimum(i, n - 1)
    prev = lambda i: jnp.maximum(i - 1, 0)
    row = lambda w: pl.BlockSpec((tm, w), lambda i: (cur(i), 0))
    consts = [wa, wb, wo, g2, wup, cw, cb, wdn]
    return pl.pallas_call(
        functools.partial(_merge_ffn_kernel, tps=tps),
        grid=(n + 1,),
        in_specs=[row(attn.shape[1]), row(hgo.shape[1]), row(d), row(d), row(d),
                  pl.BlockSpec((None, 6, d), lambda i: (cur(i) // tps, 0, 0)),
                  pl.BlockSpec((None, 6, d), lambda i: (prev(i) // tps, 0, 0))]
                 + [_const_spec(a.shape) for a in consts],
        out_specs=pl.BlockSpec((tm, d), lambda i: (prev(i), 0)),
        out_shape=jax.ShapeDtypeStruct((t, d), F32),
        scratch_shapes=[pltpu.VMEM((2, tm, d), F32), pltpu.VMEM((2, tm, d), BF16),
                        pltpu.VMEM((tm + SUBLANES, wup.shape[1]), F32)],
        compiler_params=_params(("arbitrary",), FFN_VMEM_LIMIT),
        name="merge_ffn",
    )(attn, hgo, ga, gb, x2, mod, mod, *consts)


def _pad_heads(w, used, n_heads):
    rows = w.shape[0]
    w = w.reshape(rows, n_heads, used)
    w = jnp.pad(w, ((0, 0), (0, 0), (0, HEAD_PAD - used)))
    return w.reshape(rows, n_heads * HEAD_PAD)


class _Tiles(NamedTuple):
    token: int
    ffn_token: int
    q: int
    kv: int
    diag: int
    attn_heads: int
    chunk: int
    chunks_per_step: int


def _tiles(seq):
    tl = _Tiles(token=min(256, seq), ffn_token=min(512, seq), q=min(1024, seq), kv=min(512, seq), diag=min(256, seq),
                attn_heads=4, chunk=min(128, seq), chunks_per_step=max(1, min(8, seq // 128)))
    assert seq % tl.token == 0 and seq % tl.q == 0 and tl.q % tl.kv == 0 and tl.kv % tl.diag == 0
    assert seq % tl.ffn_token == 0
    assert tl.kv % tl.token == 0 and seq % (tl.chunk * tl.chunks_per_step) == 0
    assert MLA_HEADS % tl.attn_heads == 0
    return tl


def _layer(x2, pos, invf, mod, p, *, bsz, seq, layer):
    t, d = x2.shape
    tl = _tiles(seq)
    w_in = p["w_in"]
    o_ckv = Q_LORA
    o_h = o_ckv + KV_LORA + ROPE_DIM
    o_g = o_h + 4 * HG_WIDTH
    wcq = w_in[:, :o_ckv].astype(BF16)
    wckv = jnp.concatenate([w_in[:, o_ckv:o_ckv + KV_LORA],
                            jnp.zeros((d, NOPE_DIM), F32),
                            w_in[:, o_ckv + KV_LORA:o_h],
                            jnp.zeros((d, HEAD_PAD - QK_DIM), F32)], axis=1).astype(BF16)
    wh = w_in[:, o_h:o_g].astype(BF16)
    wg = w_in[:, o_g:].astype(BF16)
    wuq = _pad_heads(p["w_uq"], QK_DIM, MLA_HEADS).astype(BF16)
    wukv = p["w_ukv"].reshape(KV_LORA, MLA_HEADS, NOPE_DIM + V_DIM)
    wk = _pad_heads(wukv[:, :, :NOPE_DIM].reshape(KV_LORA, -1), NOPE_DIM, MLA_HEADS).astype(BF16)
    wvt = _pad_heads(wukv[:, :, NOPE_DIM:].reshape(KV_LORA, -1), V_DIM, MLA_HEADS).T.astype(BF16)
    pad_g = lambda g: jnp.pad(g, (0, HEAD_PAD - QK_DIM)).reshape(1, HEAD_PAD)

    tm = tl.token
    q, k, vt, hq, hf, hi, hg, ga, gb = _mixer_in(
        x2, pos.reshape(t // tm, 1, tm), invf, mod, p["norm1_g"].reshape(1, d), wcq, wckv, wh,
        wg, p["q_a_norm_g"].reshape(1, -1), wuq, p["kv_a_norm_g"].reshape(1, -1), wk, wvt,
        pad_g(p["q_norm_g"]), pad_g(p["k_norm_g"]), seq=seq, tm=tm, tk=tl.kv)

    hw = MLA_HEADS * HEAD_PAD
    score_bound = (BF16_MARGIN * QK_DIM * Q_SCALE * jnp.max(jnp.abs(p["q_norm_g"]))
                   * jnp.max(jnp.abs(p["k_norm_g"]))).reshape(1).astype(F32)
    attn = _mla_attention(score_bound, q.reshape(bsz, seq, hw), k.reshape(bsz, seq, hw), vt,
                          tq=tl.q, tk=tl.kv, diag=tl.diag, heads=tl.attn_heads)
    r3 = lambda a: a.reshape(bsz, seq, HG_WIDTH)
    hgo = _hgrn2(r3(hq), r3(hf), r3(hi), r3(hg), p["hg_lower_bound"],
                 p["hg_out_norm_g"].reshape(1, HG_DIM), chunk=tl.chunk,
                 chunks_per_step=tl.chunks_per_step, layer=layer)

    dff = p["w_down"].shape[0]
    halve_gate = jnp.asarray(np.repeat(np.float32([0.5, 1.0]), dff))
    return _merge_ffn(attn.reshape(t, -1), hgo.reshape(t, -1), ga, gb, x2, mod,
                      p["w_branch_a"].astype(BF16), p["w_branch_b"].astype(BF16),
                      (p["w_out"] * 0.5).astype(BF16), p["norm2_g"].reshape(1, d),
                      p["w_up"].astype(BF16), p["conv_w"] * halve_gate,
                      (p["conv_b"] * halve_gate).reshape(1, -1),
                      p["w_down"].astype(BF16), seq=seq, tm=tl.ffn_token)


def kernel(x, c, positions, w_ada, b_ada, norm1_g, w_in, q_a_norm_g, w_uq, kv_a_norm_g, w_ukv,
           q_norm_g, k_norm_g, hg_lower_bound, hg_out_norm_g, w_branch_a, w_branch_b, w_out,
           norm2_g, w_up, conv_w, conv_b, w_down):
    bsz, seq, d = x.shape
    depth = w_ada.shape[0]
    invf = (ROPE_THETA ** (-jnp.arange(0, ROPE_DIM, 2, dtype=F32) / ROPE_DIM)).reshape(HALF_ROPE, 1)
    x2 = x.reshape(bsz * seq, d)
    pos = positions.reshape(bsz * seq)
    for l in range(depth):
        mod = _ada_mod(c, w_ada, b_ada, l).reshape(bsz, 6, d)
        p = dict(norm1_g=norm1_g[l], w_in=w_in[l], q_a_norm_g=q_a_norm_g[l], w_uq=w_uq[l],
                 kv_a_norm_g=kv_a_norm_g[l], w_ukv=w_ukv[l], q_norm_g=q_norm_g[l],
                 k_norm_g=k_norm_g[l], hg_lower_bound=hg_lower_bound, hg_out_norm_g=hg_out_norm_g[l],
                 w_branch_a=w_branch_a[l], w_branch_b=w_branch_b[l], w_out=w_out[l],
                 norm2_g=norm2_g[l], w_up=w_up[l], conv_w=conv_w[l], conv_b=conv_b[l],
                 w_down=w_down[l])
        x2 = _layer(x2, pos, invf, mod, p, bsz=bsz, seq=seq, layer=l)
    return x2.reshape(bsz, seq, d)
```
